```python
import jax, jax.numpy as jnp
from jax import lax
import numpy as np

D_MODEL = 4096
BATCH = 4
SEQ = 2048
DEPTH = 2
DEC_BATCH = 32
DEC_SEQ = 4
PAST_LEN = 16384
PAGE_SIZE = 128

HEAD_DIM = 64
D_ATTN = D_MODEL // 2
N_Q_HEADS = D_ATTN // HEAD_DIM
N_KV_HEADS = N_Q_HEADS // 8
GQA_GROUP = N_Q_HEADS // N_KV_HEADS
D_KV = N_KV_HEADS * HEAD_DIM
WINDOW = 128
D_SGU = D_MODEL // 2
N_SGU_GROUPS = 8
SGU_GROUP_DIM = D_SGU // N_SGU_GROUPS
CHUNK = 128
ALPHA = (2 * DEPTH) ** 0.25
BETA = (8 * DEPTH) ** -0.25
LN_EPS = 1e-5
COL_SIZES = (D_ATTN, D_KV, D_KV, D_ATTN, D_SGU, D_SGU, D_SGU, D_MODEL, D_MODEL)
IN_COLS = D_ATTN + 2 * D_KV + D_ATTN + 3 * D_SGU + 2 * D_MODEL

kernel_name = "hybrid_swa_sink_chunk_sgu_decoder_step"


def layer_norm(x, gain, bias):
    xf = x.astype(jnp.float32)
    mu = jnp.mean(xf, axis=-1, keepdims=True)
    var = jnp.mean(jnp.square(xf - mu), axis=-1, keepdims=True)
    return ((xf - mu) * lax.rsqrt(var + LN_EPS)).astype(x.dtype) * gain + bias


def split_cols(proj):
    idx = np.cumsum(np.array(COL_SIZES))[:-1].tolist()
    return jnp.split(proj, idx, axis=-1)


def window_attention(q, k, v, sinks, q_pos, k_pos):
    scores = jnp.einsum('bnqkgd,bnlkd->bnkgql', q, k).astype(jnp.float32) * (HEAD_DIM ** -0.5)
    rel = q_pos[:, :, None] - k_pos[:, None, :]
    valid = (rel >= 0) & (rel < WINDOW) & (k_pos[:, None, :] >= 0)
    scores = jnp.where(valid[None, :, None, None], scores, -jnp.inf)
    sink = jnp.broadcast_to(sinks.astype(jnp.float32).reshape(1, 1, N_KV_HEADS, GQA_GROUP, 1, 1),
                            scores.shape[:-1] + (1,))
    probs = jax.nn.softmax(jnp.concatenate([scores, sink], axis=-1), axis=-1)[..., :-1]
    return jnp.einsum('bnkgql,bnlkd->bnqkgd', probs.astype(v.dtype), v)


def trunk_layer(x, c, w_ada, b_ada, w_in, sinks, sgu_g, sgu_b, w_s, b_s, w_pa, w_pb, w_o, ln_g, ln_b,
                cache_k=None, cache_v=None):
    B, T, _ = x.shape
    mod = jax.nn.silu(c) @ w_ada + b_ada
    shift, scale, gate = jnp.split(mod, 3, axis=-1)
    h = x * (1 + scale[:, None]) + shift[:, None]
    q, k, v, z_a, u, v_b, z_b, g_a, g_b = split_cols(h @ w_in)
    q = q.reshape(B, T, N_KV_HEADS, GQA_GROUP, HEAD_DIM)
    k = k.reshape(B, T, N_KV_HEADS, HEAD_DIM)
    v = v.reshape(B, T, N_KV_HEADS, HEAD_DIM)
    v_b = layer_norm(v_b, sgu_g, sgu_b)
    w_masked = w_s * jnp.tril(jnp.ones((CHUNK, CHUNK), w_s.dtype))
    if cache_k is None:
        nb = T // WINDOW
        pad = ((0, 0), (WINDOW, 0), (0, 0), (0, 0))
        kp = jnp.pad(k, pad).reshape(B, nb + 1, WINDOW, N_KV_HEADS, HEAD_DIM)
        vp = jnp.pad(v, pad).reshape(B, nb + 1, WINDOW, N_KV_HEADS, HEAD_DIM)
        k_blk = jnp.concatenate([kp[:, :-1], kp[:, 1:]], axis=2)
        v_blk = jnp.concatenate([vp[:, :-1], vp[:, 1:]], axis=2)
        q_blk = q.reshape(B, nb, WINDOW, N_KV_HEADS, GQA_GROUP, HEAD_DIM)
        starts = jnp.arange(nb) * WINDOW
        q_pos = starts[:, None] + jnp.arange(WINDOW)[None]
        k_pos = starts[:, None] - WINDOW + jnp.arange(2 * WINDOW)[None]
        attn = window_attention(q_blk, k_blk, v_blk, sinks, q_pos, k_pos)
        vc = v_b.reshape(B, T // CHUNK, CHUNK, N_SGU_GROUPS, SGU_GROUP_DIM)
        s = jnp.einsum('gts,bnsgc->bntgc', w_masked, vc) + b_s.T[None, None, :, :, None]
    else:
        k_blk = jnp.concatenate([cache_k, k], axis=1)[:, None]
        v_blk = jnp.concatenate([cache_v, v], axis=1)[:, None]
        q_pos = (PAST_LEN + jnp.arange(T))[None]
        k_pos = (PAST_LEN - WINDOW + jnp.arange(WINDOW + T))[None]
        attn = window_attention(q[:, None], k_blk, v_blk, sinks, q_pos, k_pos)
        vc = v_b.reshape(B, T, N_SGU_GROUPS, SGU_GROUP_DIM)
        s = jnp.einsum('gts,bsgc->btgc', w_masked[:, :T, :T], vc) + b_s[:, :T].T[None, :, :, None]
    attn = attn.reshape(B, T, D_ATTN) * jax.nn.silu(z_a)
    out_b = u * s.reshape(B, T, D_SGU) * jax.nn.silu(z_b)
    merged = jax.nn.sigmoid(g_a) * (attn @ w_pa) + jax.nn.sigmoid(g_b) * (out_b @ w_pb)
    y = merged @ w_o
    x = layer_norm(ALPHA * x + gate[:, None] * y, ln_g, ln_b)
    return x, k, v, v_b


def setup_inputs(seed: int = 0) -> dict:
    key = jax.random.key(seed)
    ks = jax.random.split(key, 20)
    f32 = jnp.float32
    nrm = lambda k, shape, s: jax.random.normal(k, shape, f32) * s
    return {
        "x_prompt": nrm(ks[0], (BATCH, SEQ, D_MODEL), 1.0),
        "x_sample": nrm(ks[1], (DEC_BATCH, DEC_SEQ, D_MODEL), 1.0),
        "cache_k": nrm(ks[2], (DEPTH, DEC_BATCH, WINDOW, N_KV_HEADS, HEAD_DIM), 1.0),
        "cache_v": nrm(ks[3], (DEPTH, DEC_BATCH, WINDOW, N_KV_HEADS, HEAD_DIM), 1.0),
        "c_prompt": nrm(ks[4], (BATCH, D_MODEL), 1.0),
        "c_sample": nrm(ks[5], (DEC_BATCH, D_MODEL), 1.0),
        "w_ada": nrm(ks[6], (DEPTH, D_MODEL, 3 * D_MODEL), 0.5 * D_MODEL ** -0.5),
        "b_ada": nrm(ks[7], (DEPTH, 3 * D_MODEL), 0.02),
        "w_in": nrm(ks[8], (DEPTH, D_MODEL, IN_COLS), D_MODEL ** -0.5),
        "attn_sinks": nrm(ks[9], (DEPTH, N_Q_HEADS), 1.0),
        "sgu_ln_gain": 1.0 + nrm(ks[10], (DEPTH, D_SGU), 0.05),
        "sgu_ln_bias": nrm(ks[11], (DEPTH, D_SGU), 0.02),
        "sgu_w_s": nrm(ks[12], (DEPTH, N_SGU_GROUPS, CHUNK, CHUNK), 0.5 * CHUNK ** -0.5),
        "sgu_b_s": 1.0 + nrm(ks[13], (DEPTH, N_SGU_GROUPS, CHUNK), 0.1),
        "w_pa": nrm(ks[14], (DEPTH, D_ATTN, D_MODEL), BETA * D_ATTN ** -0.5),
        "w_pb": nrm(ks[15], (DEPTH, D_SGU, D_MODEL), BETA * D_SGU ** -0.5),
        "w_o": nrm(ks[16], (DEPTH, D_MODEL, D_MODEL), BETA * D_MODEL ** -0.5),
        "ln_gain": 1.0 + nrm(ks[17], (DEPTH, D_MODEL), 0.05),
        "ln_bias": nrm(ks[18], (DEPTH, D_MODEL), 0.02),
    }


def reference(x_prompt, x_sample, cache_k, cache_v, c_prompt, c_sample, w_ada, b_ada, w_in, attn_sinks,
              sgu_ln_gain, sgu_ln_bias, sgu_w_s, sgu_b_s, w_pa, w_pb, w_o, ln_gain, ln_bias):
    xp, xs = x_prompt, x_sample
    win_k_p, win_v_p, new_k_s, new_v_s, sgu_v_s = [], [], [], [], []
    for l in range(DEPTH):
        params = (w_ada[l], b_ada[l], w_in[l], attn_sinks[l], sgu_ln_gain[l], sgu_ln_bias[l],
                  sgu_w_s[l], sgu_b_s[l], w_pa[l], w_pb[l], w_o[l], ln_gain[l], ln_bias[l])
        xp, kp, vp, _ = trunk_layer(xp, c_prompt, *params)
        xs, ks_, vs_, vbs = trunk_layer(xs, c_sample, *params, cache_k=cache_k[l], cache_v=cache_v[l])
        win_k_p.append(kp[:, -WINDOW:])
        win_v_p.append(vp[:, -WINDOW:])
        new_k_s.append(ks_)
        new_v_s.append(vs_)
        sgu_v_s.append(vbs)
    win_k_prompt = jnp.stack(win_k_p)
    win_v_prompt = jnp.stack(win_v_p)
    new_k_sample = jnp.stack(new_k_s)
    new_v_sample = jnp.stack(new_v_s)
    sgu_v_sample = jnp.stack(sgu_v_s)
    return (xp, xs, win_k_prompt, win_v_prompt, new_k_sample, new_v_sample, sgu_v_sample)
```

```python
import functools

import jax
import jax.numpy as jnp
from jax import lax
from jax.experimental import pallas as pl
from jax.experimental.pallas import tpu as pltpu

D_MODEL = 4096
BATCH = 4
SEQ = 2048
DEPTH = 2
DEC_BATCH = 32
DEC_SEQ = 4
HEAD_DIM = 64
D_ATTN = D_MODEL // 2
N_Q_HEADS = D_ATTN // HEAD_DIM
N_KV_HEADS = N_Q_HEADS // 8
GQA_GROUP = N_Q_HEADS // N_KV_HEADS
D_KV = N_KV_HEADS * HEAD_DIM
WINDOW = 128
D_SGU = D_MODEL // 2
N_SGU_GROUPS = 8
SGU_GROUP_DIM = D_SGU // N_SGU_GROUPS
CHUNK = 128
ALPHA = (2 * DEPTH) ** 0.25
LN_EPS = 1e-5
IN_COLS = D_ATTN + 2 * D_KV + D_ATTN + 3 * D_SGU + 2 * D_MODEL

ROWS_P = BATCH * SEQ
ROWS_S = DEC_BATCH * DEC_SEQ
ROWS = ROWS_P + ROWS_S
MOD_ROWS = ROWS_S + 8
PROJ_COLS = IN_COLS - 2 * D_KV

TN = 512
TM = 1024
N_ROW_TILES = pl.cdiv(ROWS, TM)
KV_TILE = D_ATTN // TN
LN_TM = 256
SEQ_GROUP = 8
NEG = -1e30

VMEM_LIMIT = 56 * 1024 * 1024

bf16 = jnp.bfloat16
f32 = jnp.float32


def _params(n_axes, vmem=VMEM_LIMIT):
    return pltpu.CompilerParams(dimension_semantics=("arbitrary",) * n_axes, vmem_limit_bytes=vmem)


def _silu(x):
    return x * jax.nn.sigmoid(x)


def _div_pow2(x, n):
    assert n & (n - 1) == 0
    return x >> (n.bit_length() - 1)


def _mod_pow2(x, n):
    assert n & (n - 1) == 0
    return x & (n - 1)


def _ada_kernel(c_ref, w_ref, b_ref, o_ref):
    a = _silu(c_ref[...]).astype(bf16)
    o_ref[...] = jnp.dot(a, w_ref[...].astype(bf16), preferred_element_type=f32) + b_ref[...]


def _ada(c_all, w_ada, b_ada):
    tiles_per_part = D_MODEL // TN
    return pl.pallas_call(
        _ada_kernel,
        grid=(DEPTH, 3 * tiles_per_part),
        in_specs=[
            pl.BlockSpec((MOD_ROWS, D_MODEL), lambda l, j: (0, 0)),
            pl.BlockSpec((None, D_MODEL, TN), lambda l, j: (l, 0, j)),
            pl.BlockSpec((None, 1, TN), lambda l, j: (l, 0, j)),
        ],
        out_specs=pl.BlockSpec((None, None, MOD_ROWS, TN),
                               lambda l, j: (l, j // tiles_per_part, 0, j % tiles_per_part)),
        out_shape=jax.ShapeDtypeStruct((DEPTH, 3, MOD_ROWS, D_MODEL), f32),
        compiler_params=_params(2),
        name="ada",
    )(c_all, w_ada, b_ada.reshape(DEPTH, 1, 3 * D_MODEL))


def _prompt_mod_row(ref, part, blk, blocks_per_batch):
    return ref[part, pl.ds(ROWS_S + blk // blocks_per_batch, 1), :]


def _modulate_kernel(xp_ref, xs_ref, mod_ref, h_ref):
    i = pl.program_id(0)
    n_p = ROWS_P // LN_TM

    @pl.when(i < n_p)
    def _():
        shift = _prompt_mod_row(mod_ref, 0, i, SEQ // LN_TM)
        scale = _prompt_mod_row(mod_ref, 1, i, SEQ // LN_TM)
        h_ref[...] = (xp_ref[...] * (1.0 + scale) + shift).astype(bf16)

    @pl.when(i == n_p)
    def _():
        shift = mod_ref[0, 0:ROWS_S, :]
        scale = mod_ref[1, 0:ROWS_S, :]
        h_ref[0:ROWS_S, :] = (xs_ref[...] * (1.0 + scale) + shift).astype(bf16)


def _modulate(xp, xs, mod):
    n_p = ROWS_P // LN_TM
    return pl.pallas_call(
        _modulate_kernel,
        grid=(n_p + 1,),
        in_specs=[
            pl.BlockSpec((LN_TM, D_MODEL), lambda i: (jnp.minimum(i, n_p - 1), 0)),
            pl.BlockSpec((ROWS_S, D_MODEL), lambda i: (0, 0)),
            pl.BlockSpec((None, 2, MOD_ROWS, D_MODEL), lambda i: (0, 0, 0, 0)),
        ],
        out_specs=pl.BlockSpec((LN_TM, D_MODEL), lambda i: (i, 0)),
        out_shape=jax.ShapeDtypeStruct((ROWS, D_MODEL), bf16),
        compiler_params=_params(1),
        name="modulate",
    )(xp, xs, mod)


def _row_tile(i):
    return (i + N_ROW_TILES - 1) % N_ROW_TILES


def _matmul_kernel(x_ref, w_ref, o_ref, w_bf):
    i = pl.program_id(1)

    @pl.when(i == 0)
    def _():
        w_bf[...] = w_ref[...].astype(bf16)
        o_ref[0:ROWS_S, :] = jnp.dot(x_ref[0:ROWS_S, :], w_bf[...],
                                     preferred_element_type=f32).astype(o_ref.dtype)

    @pl.when(i > 0)
    def _():
        o_ref[...] = jnp.dot(x_ref[...], w_bf[...], preferred_element_type=f32).astype(o_ref.dtype)


def _matmul(x, w, layer, n_tiles, w_tile_of, out_dtype, name):
    k = x.shape[1]
    return pl.pallas_call(
        _matmul_kernel,
        grid=(n_tiles, N_ROW_TILES),
        in_specs=[
            pl.BlockSpec((TM, k), lambda j, i: (_row_tile(i), 0)),
            pl.BlockSpec((None, k, TN), lambda j, i: (layer, 0, w_tile_of(j))),
        ],
        out_specs=pl.BlockSpec((TM, TN), lambda j, i: (_row_tile(i), j)),
        out_shape=jax.ShapeDtypeStruct((ROWS, n_tiles * TN), out_dtype),
        scratch_shapes=[pltpu.VMEM((k, TN), bf16)],
        compiler_params=_params(2),
        name=name,
    )(x, w)


def _attention(q_ref, k_all, v_all, mask, sink_ref, a_scr, t, lk):
    lane = lax.broadcasted_iota(jnp.int32, (lk, 2 * HEAD_DIM), 1)
    nt = (((1,), (1,)), ((), ()))
    pairs = GQA_GROUP // 2
    for kv_pair in range(N_KV_HEADS // 2):
        cols = slice(kv_pair * 2 * HEAD_DIM, (kv_pair + 1) * 2 * HEAD_DIM)
        kp, vp = k_all[:, cols], v_all[:, cols]
        for par in range(2):
            kv = 2 * kv_pair + par
            keep = (lane < HEAD_DIM) if par == 0 else (lane >= HEAD_DIM)
            k_own = jnp.where(keep, kp, 0.0)
            v_own = jnp.where(keep, vp, 0.0)
            k_oth = pltpu.roll(k_own, HEAD_DIM, 1)
            v_oth = pltpu.roll(v_own, HEAD_DIM, 1)
            if par == 0:
                k2 = jnp.concatenate([k_own, k_oth], axis=0).astype(bf16)
                v2 = jnp.concatenate([v_own, v_oth], axis=0).astype(bf16)
            else:
                k2 = jnp.concatenate([k_oth, k_own], axis=0).astype(bf16)
                v2 = jnp.concatenate([v_oth, v_own], axis=0).astype(bf16)
            base = kv * GQA_GROUP * HEAD_DIM
            q4 = jnp.concatenate(
                [q_ref[:, base + p * 128: base + (p + 1) * 128] for p in range(pairs)], axis=0)
            q4 = q4 * jnp.asarray(HEAD_DIM ** -0.5, bf16)
            s_all = lax.dot_general(q4, k2, nt, preferred_element_type=f32)
            rows = []
            for p in range(pairs):
                halves = []
                for h in range(2):
                    s = jnp.where(mask, s_all[p * t:(p + 1) * t, h * lk:(h + 1) * lk], NEG)
                    sink = sink_ref[kv * GQA_GROUP + 2 * p + h]
                    m = jnp.maximum(jnp.max(s, axis=1, keepdims=True), sink)
                    e = jnp.exp(s - m)
                    denom = jnp.sum(e, axis=1, keepdims=True) + jnp.exp(sink - m)
                    halves.append((e * (1.0 / denom)).astype(bf16))
                rows.append(jnp.concatenate(halves, axis=1))
            probs = jnp.concatenate(rows, axis=0)
            o = jnp.dot(probs, v2, preferred_element_type=f32)
            for p in range(pairs):
                a_scr[:, base + p * 128: base + (p + 1) * 128] = o[p * t:(p + 1) * t, :]


def _sgu_norm(vb_ref, gain_ref, bias_ref):
    vb = vb_ref[...].astype(f32)
    mu = jnp.mean(vb, axis=1, keepdims=True)
    d = vb - mu
    var = jnp.mean(d * d, axis=1, keepdims=True)
    return d * lax.rsqrt(var + LN_EPS) * gain_ref[...] + bias_ref[...]


def _gate_and_store(za_ref, u_ref, zb_ref, ws_ref, bs_ref, a_scr, vn, w_mask, ab_ref):
    ab_ref[:, 0:D_ATTN] = (a_scr[...] * _silu(za_ref[...].astype(f32))).astype(bf16)
    vn_bf = vn.astype(bf16)
    for g in range(N_SGU_GROUPS):
        cols = slice(g * SGU_GROUP_DIM, (g + 1) * SGU_GROUP_DIM)
        w_g = jnp.where(w_mask, ws_ref[g], 0.0).astype(bf16)
        s = jnp.dot(w_g, vn_bf[:, cols], preferred_element_type=f32) + bs_ref[:, g:g + 1]
        out_b = u_ref[:, cols].astype(f32) * s * _silu(zb_ref[:, cols].astype(f32))
        ab_ref[:, D_ATTN + g * SGU_GROUP_DIM: D_ATTN + (g + 1) * SGU_GROUP_DIM] = out_b.astype(bf16)


def _mix_prompt_kernel(sink_ref, q_ref, za_ref, u_ref, vb_ref, zb_ref, kvc_ref, kvp_ref,
                       gain_ref, bias_ref, ws_ref, bs_ref, ab_ref, a_scr):
    n = pl.program_id(1)
    kvc, kvp = kvc_ref[...], kvp_ref[...]
    k_all = jnp.concatenate([kvp[:, 0:D_KV], kvc[:, 0:D_KV]], axis=0)
    v_all = jnp.concatenate([kvp[:, D_KV:], kvc[:, D_KV:]], axis=0)
    qi = lax.broadcasted_iota(jnp.int32, (WINDOW, 2 * WINDOW), 0)
    kj = lax.broadcasted_iota(jnp.int32, (WINDOW, 2 * WINDOW), 1)
    no_prev = jnp.where(n > 0, 0, 2 * WINDOW)
    mask = ((kj < WINDOW) & (kj > qi + no_prev)) | ((kj >= WINDOW) & (kj - WINDOW <= qi))
    _attention(q_ref, k_all, v_all, mask, sink_ref, a_scr, WINDOW, 2 * WINDOW)

    vn = _sgu_norm(vb_ref, gain_ref, bias_ref)
    r = lax.broadcasted_iota(jnp.int32, (CHUNK, CHUNK), 0)
    c = lax.broadcasted_iota(jnp.int32, (CHUNK, CHUNK), 1)
    _gate_and_store(za_ref, u_ref, zb_ref, ws_ref, bs_ref, a_scr, vn, c <= r, ab_ref)


def _mix_sample_kernel(sink_ref, q_ref, za_ref, u_ref, vb_ref, zb_ref, kvn_ref, ck_ref, cv_ref,
                       gain_ref, bias_ref, ws_ref, bs_ref, ab_ref, vn_ref, a_scr):
    t = SEQ_GROUP * DEC_SEQ
    n_cache = SEQ_GROUP * WINDOW
    lk = n_cache + WINDOW
    kvn = kvn_ref[...]
    pad = jnp.zeros((WINDOW - t, D_KV), f32)
    k_all = jnp.concatenate([ck_ref[...].reshape(n_cache, D_KV), kvn[:, 0:D_KV], pad], axis=0)
    v_all = jnp.concatenate([cv_ref[...].reshape(n_cache, D_KV), kvn[:, D_KV:], pad], axis=0)
    qi = lax.broadcasted_iota(jnp.int32, (t, lk), 0)
    kj = lax.broadcasted_iota(jnp.int32, (t, lk), 1)
    q_seq, q_tok = _div_pow2(qi, DEC_SEQ), _mod_pow2(qi, DEC_SEQ)
    in_cache = (_div_pow2(kj, WINDOW) == q_seq) & (_mod_pow2(kj, WINDOW) > q_tok)
    kn = jnp.maximum(kj - n_cache, 0)
    in_new = (kn < t) & (_div_pow2(kn, DEC_SEQ) == q_seq) & (_mod_pow2(kn, DEC_SEQ) <= q_tok)
    mask = ((kj < n_cache) & in_cache) | ((kj >= n_cache) & in_new)
    _attention(q_ref, k_all, v_all, mask, sink_ref, a_scr, t, lk)

    vn = _sgu_norm(vb_ref, gain_ref, bias_ref)
    vn_ref[...] = vn
    r = lax.broadcasted_iota(jnp.int32, (t, t), 0)
    c = lax.broadcasted_iota(jnp.int32, (t, t), 1)
    w_mask = (_div_pow2(r, DEC_SEQ) == _div_pow2(c, DEC_SEQ)) & (c <= r)
    _gate_and_store(za_ref, u_ref, zb_ref, ws_ref, bs_ref, a_scr, vn, w_mask, ab_ref)


def _proj_specs(rows, row_of):
    return [pl.BlockSpec((rows, D_ATTN), functools.partial(lambda s, *g: (row_of(*g), s), s))
            for s in range(5)]


def _mix_prompt(proj, kv, sinks, gain, bias, w_s, b_s):
    blocks = SEQ // WINDOW
    row_of = lambda b, n: b * blocks + n
    full = lambda shape: pl.BlockSpec(shape, lambda b, n: (0,) * len(shape))
    return pl.pallas_call(
        _mix_prompt_kernel,
        grid=(BATCH, blocks),
        in_specs=[pl.BlockSpec(memory_space=pltpu.SMEM)] + _proj_specs(WINDOW, row_of) + [
            pl.BlockSpec((WINDOW, 2 * D_KV), lambda b, n: (row_of(b, n), 0)),
            pl.BlockSpec((WINDOW, 2 * D_KV), lambda b, n: (row_of(b, jnp.maximum(n - 1, 0)), 0)),
            full((1, D_SGU)), full((1, D_SGU)),
            full((N_SGU_GROUPS, CHUNK, CHUNK)), full((CHUNK, N_SGU_GROUPS)),
        ],
        out_specs=pl.BlockSpec((WINDOW, D_MODEL), lambda b, n: (row_of(b, n), 0)),
        out_shape=jax.ShapeDtypeStruct((ROWS_P, D_MODEL), bf16),
        scratch_shapes=[pltpu.VMEM((WINDOW, D_ATTN), f32)],
        compiler_params=_params(2),
        name="mix_prompt",
    )(sinks, proj, proj, proj, proj, proj, kv, kv, gain, bias, w_s, b_s)


def _mix_sample(proj, kv, cache_k, cache_v, sinks, gain, bias, w_s, b_s):
    t = SEQ_GROUP * DEC_SEQ
    first = ROWS_P // t
    row_of = lambda g: first + g
    full = lambda shape: pl.BlockSpec(shape, lambda g: (0,) * len(shape))
    cache_spec = pl.BlockSpec((SEQ_GROUP, WINDOW, D_KV), lambda g: (g, 0, 0))
    return pl.pallas_call(
        _mix_sample_kernel,
        grid=(DEC_BATCH // SEQ_GROUP,),
        in_specs=[pl.BlockSpec(memory_space=pltpu.SMEM)] + _proj_specs(t, row_of) + [
            pl.BlockSpec((t, 2 * D_KV), lambda g: (row_of(g), 0)),
            cache_spec, cache_spec,
            full((1, D_SGU)), full((1, D_SGU)),
            full((N_SGU_GROUPS, t, t)), full((t, N_SGU_GROUPS)),
        ],
        out_specs=[pl.BlockSpec((t, D_MODEL), lambda g: (g, 0)),
                   pl.BlockSpec((t, D_SGU), lambda g: (g, 0))],
        out_shape=[jax.ShapeDtypeStruct((ROWS_S, D_MODEL), bf16),
                   jax.ShapeDtypeStruct((ROWS_S, D_SGU), f32)],
        scratch_shapes=[pltpu.VMEM((t, D_ATTN), f32)],
        compiler_params=_params(1),
        name="mix_sample",
    )(sinks, proj, proj, proj, proj, proj, kv, cache_k, cache_v, gain, bias, w_s, b_s)


def _merge_kernel(abp_ref, abs_ref, ga_ref, gb_ref, wa_ref, wb_ref, o_ref, wa_bf, wb_bf):
    i = pl.program_id(1)

    def merged(ab, ga, gb):
        ya = jnp.dot(ab[:, 0:D_ATTN], wa_bf[...], preferred_element_type=f32)
        yb = jnp.dot(ab[:, D_ATTN:], wb_bf[...], preferred_element_type=f32)
        return (jax.nn.sigmoid(ga.astype(f32)) * ya + jax.nn.sigmoid(gb.astype(f32)) * yb).astype(bf16)

    @pl.when(i == 0)
    def _():
        wa_bf[...] = wa_ref[...].astype(bf16)
        wb_bf[...] = wb_ref[...].astype(bf16)
        o_ref[0:ROWS_S, :] = merged(abs_ref[...], ga_ref[0:ROWS_S, :], gb_ref[0:ROWS_S, :])

    @pl.when(i > 0)
    def _():
        o_ref[...] = merged(abp_ref[...], ga_ref[...], gb_ref[...])


def _merge(ab_p, ab_s, proj, w_pa, w_pb, layer):
    n_tiles = D_MODEL // TN
    ga_tile = (PROJ_COLS - 2 * D_MODEL) // TN
    gb_tile = (PROJ_COLS - D_MODEL) // TN
    return pl.pallas_call(
        _merge_kernel,
        grid=(n_tiles, N_ROW_TILES),
        in_specs=[
            pl.BlockSpec((TM, D_MODEL), lambda j, i: (jnp.maximum(i - 1, 0), 0)),
            pl.BlockSpec((ROWS_S, D_MODEL), lambda j, i: (0, 0)),
            pl.BlockSpec((TM, TN), lambda j, i: (_row_tile(i), ga_tile + j)),
            pl.BlockSpec((TM, TN), lambda j, i: (_row_tile(i), gb_tile + j)),
            pl.BlockSpec((None, D_ATTN, TN), lambda j, i: (layer, 0, j)),
            pl.BlockSpec((None, D_SGU, TN), lambda j, i: (layer, 0, j)),
        ],
        out_specs=pl.BlockSpec((TM, TN), lambda j, i: (_row_tile(i), j)),
        out_shape=jax.ShapeDtypeStruct((ROWS, D_MODEL), bf16),
        scratch_shapes=[pltpu.VMEM((D_ATTN, TN), bf16), pltpu.VMEM((D_SGU, TN), bf16)],
        compiler_params=_params(2),
        name="merge",
    )(ab_p, ab_s, proj, proj, w_pa, w_pb)


def _post_norm(x, y, gate, g_ref, b_ref):
    t = ALPHA * x + gate * y.astype(f32)
    mu = jnp.mean(t, axis=1, keepdims=True)
    d = t - mu
    var = jnp.mean(d * d, axis=1, keepdims=True)
    return d * lax.rsqrt(var + LN_EPS) * g_ref[...] + b_ref[...]


def _ln_first_kernel(xp_ref, xs_ref, y_ref, gate_ref, nxt_ref, g_ref, b_ref, x_out, h_out):
    i = pl.program_id(0)
    n_p = ROWS_P // LN_TM

    @pl.when(i < n_p)
    def _():
        gate = _prompt_mod_row(gate_ref, 0, i, SEQ // LN_TM)
        xn = _post_norm(xp_ref[...], y_ref[...], gate, g_ref, b_ref)
        x_out[...] = xn
        shift = _prompt_mod_row(nxt_ref, 0, i, SEQ // LN_TM)
        scale = _prompt_mod_row(nxt_ref, 1, i, SEQ // LN_TM)
        h_out[...] = (xn * (1.0 + scale) + shift).astype(bf16)

    @pl.when(i == n_p)
    def _():
        xn = _post_norm(xs_ref[...], y_ref[0:ROWS_S, :], gate_ref[0, 0:ROWS_S, :], g_ref, b_ref)
        x_out[0:ROWS_S, :] = xn
        h_out[0:ROWS_S, :] = (xn * (1.0 + nxt_ref[1, 0:ROWS_S, :]) + nxt_ref[0, 0:ROWS_S, :]).astype(bf16)


def _ln_last_kernel(x_ref, y_ref, gate_ref, g_ref, b_ref, yp_out, ys_out):
    i = pl.program_id(0)
    n_p = ROWS_P // LN_TM

    @pl.when(i < n_p)
    def _():
        gate = _prompt_mod_row(gate_ref, 0, i, SEQ // LN_TM)
        yp_out[...] = _post_norm(x_ref[...], y_ref[...], gate, g_ref, b_ref)

    @pl.when(i == n_p)
    def _():
        ys_out[...] = _post_norm(x_ref[0:ROWS_S, :], y_ref[0:ROWS_S, :], gate_ref[0, 0:ROWS_S, :], g_ref, b_ref)


def _ln_first(xp, xs, y, mod, ln_g, ln_b, layer):
    n_p = ROWS_P // LN_TM
    row = lambda i: (i, 0)
    vec = pl.BlockSpec((1, D_MODEL), lambda i: (0, 0))
    return pl.pallas_call(
        _ln_first_kernel,
        grid=(n_p + 1,),
        in_specs=[
            pl.BlockSpec((LN_TM, D_MODEL), lambda i: (jnp.minimum(i, n_p - 1), 0)),
            pl.BlockSpec((ROWS_S, D_MODEL), lambda i: (0, 0)),
            pl.BlockSpec((LN_TM, D_MODEL), row),
            pl.BlockSpec((None, 1, MOD_ROWS, D_MODEL), lambda i: (layer, 2, 0, 0)),
            pl.BlockSpec((None, 2, MOD_ROWS, D_MODEL), lambda i: (layer + 1, 0, 0, 0)),
            vec, vec,
        ],
        out_specs=[pl.BlockSpec((LN_TM, D_MODEL), row), pl.BlockSpec((LN_TM, D_MODEL), row)],
        out_shape=[jax.ShapeDtypeStruct((ROWS, D_MODEL), f32), jax.ShapeDtypeStruct((ROWS, D_MODEL), bf16)],
        compiler_params=_params(1),
        name="ln_first",
    )(xp, xs, y, mod, mod, ln_g, ln_b)


def _ln_last(x, y, mod, ln_g, ln_b, layer):
    n_p = ROWS_P // LN_TM
    row = lambda i: (i, 0)
    vec = pl.BlockSpec((1, D_MODEL), lambda i: (0, 0))
    return pl.pallas_call(
        _ln_last_kernel,
        grid=(n_p + 1,),
        in_specs=[
            pl.BlockSpec((LN_TM, D_MODEL), row),
            pl.BlockSpec((LN_TM, D_MODEL), row),
            pl.BlockSpec((None, 1, MOD_ROWS, D_MODEL), lambda i: (layer, 2, 0, 0)),
            vec, vec,
        ],
        out_specs=[pl.BlockSpec((LN_TM, D_MODEL), lambda i: (jnp.minimum(i, n_p - 1), 0)),
                   pl.BlockSpec((ROWS_S, D_MODEL), lambda i: (0, 0))],
        out_shape=[jax.ShapeDtypeStruct((ROWS_P, D_MODEL), f32), jax.ShapeDtypeStruct((ROWS_S, D_MODEL), f32)],
        compiler_params=_params(1),
        name="ln_last",
    )(x, y, mod, ln_g, ln_b)


def kernel(x_prompt, x_sample, cache_k, cache_v, c_prompt, c_sample, w_ada, b_ada, w_in, attn_sinks,
           sgu_ln_gain, sgu_ln_bias, sgu_w_s, sgu_b_s, w_pa, w_pb, w_o, ln_gain, ln_bias):
    assert DEPTH == 2
    xp = x_prompt.reshape(ROWS_P, D_MODEL)
    xs = x_sample.reshape(ROWS_S, D_MODEL)
    c_all = jnp.concatenate([jnp.repeat(c_sample, DEC_SEQ, axis=0), c_prompt,
                             jnp.zeros((MOD_ROWS - ROWS_S - BATCH, D_MODEL), f32)], axis=0)
    mod = _ada(c_all, w_ada, b_ada)

    seq_t = SEQ_GROUP * DEC_SEQ
    win_k, win_v, new_k, new_v, sgu_v = [], [], [], [], []
    h = _modulate(xp, xs, mod)
    x_all = None
    for l in range(DEPTH):
        proj = _matmul(h, w_in, l, PROJ_COLS // TN, lambda j: j + (j >= KV_TILE).astype(jnp.int32), bf16, "proj")
        kv = _matmul(h, w_in, l, 1, lambda j: j + KV_TILE, f32, "kv")
        gain = sgu_ln_gain[l].reshape(1, D_SGU)
        bias = sgu_ln_bias[l].reshape(1, D_SGU)
        ab_p = _mix_prompt(proj, kv, attn_sinks[l], gain, bias, sgu_w_s[l], sgu_b_s[l].T)
        w_small = jnp.tile(sgu_w_s[l][:, :DEC_SEQ, :DEC_SEQ], (1, SEQ_GROUP, SEQ_GROUP))
        b_small = jnp.tile(sgu_b_s[l][:, :DEC_SEQ].T, (SEQ_GROUP, 1))
        ab_s, vn_s = _mix_sample(proj, kv, cache_k[l].reshape(DEC_BATCH, WINDOW, D_KV),
                                 cache_v[l].reshape(DEC_BATCH, WINDOW, D_KV),
                                 attn_sinks[l], gain, bias, w_small, b_small)
        merged = _merge(ab_p, ab_s, proj, w_pa, w_pb, l)
        y = _matmul(merged, w_o, l, D_MODEL // TN, lambda j: j, bf16, "out")
        g, b = ln_gain[l].reshape(1, D_MODEL), ln_bias[l].reshape(1, D_MODEL)
        if l == 0:
            x_all, h = _ln_first(xp, xs, y, mod, g, b, l)
        else:
            y_p, y_s = _ln_last(x_all, y, mod, g, b, l)

        kv_p = kv[:ROWS_P].reshape(BATCH, SEQ, 2, N_KV_HEADS, HEAD_DIM)[:, SEQ - WINDOW:]
        kv_s = kv[ROWS_P:].reshape(DEC_BATCH, DEC_SEQ, 2, N_KV_HEADS, HEAD_DIM)
        win_k.append(kv_p[:, :, 0])
        win_v.append(kv_p[:, :, 1])
        new_k.append(kv_s[:, :, 0])
        new_v.append(kv_s[:, :, 1])
        sgu_v.append(vn_s.reshape(DEC_BATCH, DEC_SEQ, D_SGU))

    return (y_p.reshape(BATCH, SEQ, D_MODEL), y_s.reshape(DEC_BATCH, DEC_SEQ, D_MODEL),
            jnp.stack(win_k), jnp.stack(win_v), jnp.stack(new_k), jnp.stack(new_v), jnp.stack(sgu_v))
```

```python
import functools

import jax
import jax.numpy as jnp
from jax import lax
from jax.experimental import pallas as pl
from jax.experimental.pallas import tpu as pltpu

D_MODEL = 4096
BATCH = 4
SEQ = 2048
DEPTH = 2
DEC_BATCH = 32
DEC_SEQ = 4
HEAD_DIM = 64
D_ATTN = D_MODEL // 2
N_Q_HEADS = D_ATTN // HEAD_DIM
N_KV_HEADS = N_Q_HEADS // 8
GQA_GROUP = N_Q_HEADS // N_KV_HEADS
D_KV = N_KV_HEADS * HEAD_DIM
WINDOW = 128
D_SGU = D_MODEL // 2
N_SGU_GROUPS = 8
SGU_GROUP_DIM = D_SGU // N_SGU_GROUPS
CHUNK = 128
ALPHA = (2 * DEPTH) ** 0.25
LN_EPS = 1e-5
IN_COLS = D_ATTN + 2 * D_KV + D_ATTN + 3 * D_SGU + 2 * D_MODEL

ROWS_P = BATCH * SEQ
ROWS_S = DEC_BATCH * DEC_SEQ
ROWS = ROWS_P + ROWS_S
MOD_ROWS = ROWS_S + 8
PROJ_COLS = IN_COLS - 2 * D_KV

TN = 512
TM = 1024
N_ROW_TILES = pl.cdiv(ROWS, TM)
KV_TILE = D_ATTN // TN
LN_TM = 256
SEQ_GROUP = 8
NEG = -1e30
LOG2E = 1.4426950408889634

VMEM_LIMIT = 56 * 1024 * 1024

bf16 = jnp.bfloat16
f32 = jnp.float32


def _params(n_axes, vmem=VMEM_LIMIT):
    return pltpu.CompilerParams(dimension_semantics=("arbitrary",) * n_axes, vmem_limit_bytes=vmem)


def _sigmoid(x):
    return 0.5 + 0.5 * jnp.tanh(0.5 * x)


def _silu(x):
    half = 0.5 * x
    return half + half * jnp.tanh(half)


def _div_pow2(x, n):
    assert n & (n - 1) == 0
    return x >> (n.bit_length() - 1)


def _mod_pow2(x, n):
    assert n & (n - 1) == 0
    return x & (n - 1)


def _ada_kernel(c_ref, w_ref, b_ref, o_ref):
    a = _silu(c_ref[...]).astype(bf16)
    o_ref[...] = jnp.dot(a, w_ref[...].astype(bf16), preferred_element_type=f32) + b_ref[...]


def _ada(c_all, w_ada, b_ada):
    tiles_per_part = D_MODEL // TN
    return pl.pallas_call(
        _ada_kernel,
        grid=(DEPTH, 3 * tiles_per_part),
        in_specs=[
            pl.BlockSpec((MOD_ROWS, D_MODEL), lambda l, j: (0, 0)),
            pl.BlockSpec((None, D_MODEL, TN), lambda l, j: (l, 0, j)),
            pl.BlockSpec((None, 1, TN), lambda l, j: (l, 0, j)),
        ],
        out_specs=pl.BlockSpec((None, None, MOD_ROWS, TN),
                               lambda l, j: (l, j // tiles_per_part, 0, j % tiles_per_part)),
        out_shape=jax.ShapeDtypeStruct((DEPTH, 3, MOD_ROWS, D_MODEL), f32),
        compiler_params=_params(2),
        name="ada",
    )(c_all, w_ada, b_ada.reshape(DEPTH, 1, 3 * D_MODEL))


def _prompt_mod_row(ref, part, blk, blocks_per_batch):
    return ref[part, pl.ds(ROWS_S + blk // blocks_per_batch, 1), :]


def _modulate_kernel(xp_ref, xs_ref, mod_ref, h_ref):
    i = pl.program_id(0)
    n_p = ROWS_P // LN_TM

    @pl.when(i < n_p)
    def _():
        shift = _prompt_mod_row(mod_ref, 0, i, SEQ // LN_TM)
        scale = _prompt_mod_row(mod_ref, 1, i, SEQ // LN_TM)
        h_ref[...] = (xp_ref[...] * (1.0 + scale) + shift).astype(bf16)

    @pl.when(i == n_p)
    def _():
        shift = mod_ref[0, 0:ROWS_S, :]
        scale = mod_ref[1, 0:ROWS_S, :]
        h_ref[0:ROWS_S, :] = (xs_ref[...] * (1.0 + scale) + shift).astype(bf16)


def _modulate(xp, xs, mod):
    n_p = ROWS_P // LN_TM
    return pl.pallas_call(
        _modulate_kernel,
        grid=(n_p + 1,),
        in_specs=[
            pl.BlockSpec((LN_TM, D_MODEL), lambda i: (jnp.minimum(i, n_p - 1), 0)),
            pl.BlockSpec((ROWS_S, D_MODEL), lambda i: (0, 0)),
            pl.BlockSpec((None, 2, MOD_ROWS, D_MODEL), lambda i: (0, 0, 0, 0)),
        ],
        out_specs=pl.BlockSpec((LN_TM, D_MODEL), lambda i: (i, 0)),
        out_shape=jax.ShapeDtypeStruct((ROWS, D_MODEL), bf16),
        compiler_params=_params(1),
        name="modulate",
    )(xp, xs, mod)


def _row_tile(i, n_row_tiles=N_ROW_TILES):
    return (i + n_row_tiles - 1) % n_row_tiles


def _matmul_kernel(x_ref, w_ref, o_ref, w_bf):
    i = pl.program_id(1)

    @pl.when(i == 0)
    def _():
        w_bf[...] = w_ref[...].astype(bf16)
        o_ref[0:ROWS_S, :] = jnp.dot(x_ref[0:ROWS_S, :], w_bf[...],
                                     preferred_element_type=f32).astype(o_ref.dtype)

    @pl.when(i > 0)
    def _():
        o_ref[...] = jnp.dot(x_ref[...], w_bf[...], preferred_element_type=f32).astype(o_ref.dtype)


def _matmul(x, w, layer, n_tiles, w_tile_of, out_dtype, name, tm=TM, tn=TN):
    k = x.shape[1]
    assert ROWS_P % tm == 0
    n_rt = pl.cdiv(ROWS, tm)
    out_bytes = jnp.dtype(out_dtype).itemsize
    vmem = 2 * tm * k * 2 + 2 * k * tn * 4 + k * tn * 2 + 2 * tm * tn * out_bytes + (5 << 20)
    return pl.pallas_call(
        _matmul_kernel,
        grid=(n_tiles, n_rt),
        in_specs=[
            pl.BlockSpec((tm, k), lambda j, i: (_row_tile(i, n_rt), 0)),
            pl.BlockSpec((None, k, tn), lambda j, i: (layer, 0, w_tile_of(j))),
        ],
        out_specs=pl.BlockSpec((tm, tn), lambda j, i: (_row_tile(i, n_rt), j)),
        out_shape=jax.ShapeDtypeStruct((ROWS, n_tiles * tn), out_dtype),
        scratch_shapes=[pltpu.VMEM((k, tn), bf16)],
        compiler_params=_params(2, vmem),
        name=name,
    )(x, w)


def _attention(q_ref, k_all, v_all, mask, slot, slot_cols, live_key, sink_ref, a_scr, t, lk):
    lane = lax.broadcasted_iota(jnp.int32, (lk, 2 * HEAD_DIM), 1)
    low, high = lane < HEAD_DIM, lane >= HEAD_DIM
    ones2 = jnp.concatenate([low, high], axis=0).astype(f32).astype(bf16)
    nt = (((1,), (1,)), ((), ()))
    pairs = GQA_GROUP // 2
    k_scale = HEAD_DIM ** -0.5 * LOG2E
    for kv_pair in range(N_KV_HEADS // 2):
        cols = slice(kv_pair * 2 * HEAD_DIM, (kv_pair + 1) * 2 * HEAD_DIM)
        kp, vp = k_all[:, cols] * k_scale, v_all[:, cols]
        for par in range(2):
            kv = 2 * kv_pair + par
            keep = (low if par == 0 else high) & live_key
            k_own = jnp.where(keep, kp, 0.0)
            v_own = jnp.where(keep, vp, 0.0)
            k_oth = pltpu.roll(k_own, HEAD_DIM, 1)
            v_oth = pltpu.roll(v_own, HEAD_DIM, 1)
            if par == 0:
                k2 = jnp.concatenate([k_own, k_oth], axis=0).astype(bf16)
                v2 = jnp.concatenate([v_own, v_oth], axis=0).astype(bf16)
            else:
                k2 = jnp.concatenate([k_oth, k_own], axis=0).astype(bf16)
                v2 = jnp.concatenate([v_oth, v_own], axis=0).astype(bf16)
            v2 = jnp.concatenate([v2, ones2], axis=1)
            base = kv * GQA_GROUP * HEAD_DIM
            q4 = jnp.concatenate(
                [q_ref[:, base + p * 128: base + (p + 1) * 128] for p in range(pairs)], axis=0)
            s_all = lax.dot_general(q4, k2, nt, preferred_element_type=f32)
            rows = []
            for p in range(pairs):
                halves = []
                for h in range(2):
                    sink = sink_ref[kv * GQA_GROUP + 2 * p + h] * LOG2E
                    fill = jnp.where(slot, sink, NEG)
                    if slot_cols < lk:
                        fill = jnp.concatenate([fill, jnp.full((t, lk - slot_cols), NEG, f32)], axis=1)
                    s = jnp.where(mask, s_all[p * t:(p + 1) * t, h * lk:(h + 1) * lk], fill)
                    m = jnp.max(s, axis=1, keepdims=True)
                    halves.append(jnp.exp2(s - m).astype(bf16))
                rows.append(jnp.concatenate(halves, axis=1))
            probs = jnp.concatenate(rows, axis=0)
            o = jnp.dot(probs, v2, preferred_element_type=f32)
            o = o[:, 0:128] / o[:, 128:256]
            for p in range(pairs):
                a_scr[:, base + p * 128: base + (p + 1) * 128] = o[p * t:(p + 1) * t, :]


def _sgu_norm(vb_ref, gain_ref, bias_ref):
    vb = vb_ref[...].astype(f32)
    mu = jnp.mean(vb, axis=1, keepdims=True)
    d = vb - mu
    var = jnp.mean(d * d, axis=1, keepdims=True)
    return d * lax.rsqrt(var + LN_EPS) * gain_ref[...] + bias_ref[...]


def _gate_and_store(za_ref, u_ref, zb_ref, ws_ref, bs_ref, a_scr, vn, w_mask, ab_ref):
    ab_ref[:, 0:D_ATTN] = (a_scr[...] * _silu(za_ref[...].astype(f32))).astype(bf16)
    vn_bf = vn.astype(bf16)
    for g in range(N_SGU_GROUPS):
        cols = slice(g * SGU_GROUP_DIM, (g + 1) * SGU_GROUP_DIM)
        w_g = jnp.where(w_mask, ws_ref[g], 0.0).astype(bf16)
        s = jnp.dot(w_g, vn_bf[:, cols], preferred_element_type=f32) + bs_ref[:, g:g + 1]
        out_b = u_ref[:, cols].astype(f32) * s * _silu(zb_ref[:, cols].astype(f32))
        ab_ref[:, D_ATTN + g * SGU_GROUP_DIM: D_ATTN + (g + 1) * SGU_GROUP_DIM] = out_b.astype(bf16)


def _mix_prompt_kernel(sink_ref, q_ref, za_ref, u_ref, vb_ref, zb_ref, kvc_ref, kvp_ref,
                       gain_ref, bias_ref, ws_ref, bs_ref, ab_ref, a_scr):
    n = pl.program_id(1)
    kvc, kvp = kvc_ref[...], kvp_ref[...]
    k_all = jnp.concatenate([kvp[:, 0:D_KV], kvc[:, 0:D_KV]], axis=0)
    v_all = jnp.concatenate([kvp[:, D_KV:], kvc[:, D_KV:]], axis=0)
    qi = lax.broadcasted_iota(jnp.int32, (WINDOW, 2 * WINDOW), 0)
    kj = lax.broadcasted_iota(jnp.int32, (WINDOW, 2 * WINDOW), 1)
    no_prev = jnp.where(n > 0, 0, 2 * WINDOW)
    mask = ((kj < WINDOW) & (kj > qi + no_prev)) | ((kj >= WINDOW) & (kj - WINDOW <= qi))
    slot = lax.broadcasted_iota(jnp.int32, (WINDOW, WINDOW), 1) == 0
    live_key = lax.broadcasted_iota(jnp.int32, (2 * WINDOW, 2 * HEAD_DIM), 0) != 0
    _attention(q_ref, k_all, v_all, mask, slot, WINDOW, live_key, sink_ref, a_scr, WINDOW, 2 * WINDOW)

    vn = _sgu_norm(vb_ref, gain_ref, bias_ref)
    r = lax.broadcasted_iota(jnp.int32, (CHUNK, CHUNK), 0)
    c = lax.broadcasted_iota(jnp.int32, (CHUNK, CHUNK), 1)
    _gate_and_store(za_ref, u_ref, zb_ref, ws_ref, bs_ref, a_scr, vn, c <= r, ab_ref)


def _mix_sample_kernel(sink_ref, q_ref, za_ref, u_ref, vb_ref, zb_ref, kvn_ref, ck_ref, cv_ref,
                       gain_ref, bias_ref, ws_ref, bs_ref, ab_ref, vn_ref, a_scr):
    t = SEQ_GROUP * DEC_SEQ
    n_cache = SEQ_GROUP * WINDOW
    lk = n_cache + WINDOW
    kvn = kvn_ref[...]
    pad = jnp.zeros((WINDOW - t, D_KV), f32)
    k_all = jnp.concatenate([ck_ref[...].reshape(n_cache, D_KV), kvn[:, 0:D_KV], pad], axis=0)
    v_all = jnp.concatenate([cv_ref[...].reshape(n_cache, D_KV), kvn[:, D_KV:], pad], axis=0)
    qi = lax.broadcasted_iota(jnp.int32, (t, lk), 0)
    kj = lax.broadcasted_iota(jnp.int32, (t, lk), 1)
    q_seq, q_tok = _div_pow2(qi, DEC_SEQ), _mod_pow2(qi, DEC_SEQ)
    in_cache = (_div_pow2(kj, WINDOW) == q_seq) & (_mod_pow2(kj, WINDOW) > q_tok)
    kn = jnp.maximum(kj - n_cache, 0)
    in_new = (kn < t) & (_div_pow2(kn, DEC_SEQ) == q_seq) & (_mod_pow2(kn, DEC_SEQ) <= q_tok)
    mask = ((kj < n_cache) & in_cache) | ((kj >= n_cache) & in_new)
    slot_seq = _div_pow2(lax.broadcasted_iota(jnp.int32, (t, n_cache), 0), DEC_SEQ)
    slot = lax.broadcasted_iota(jnp.int32, (t, n_cache), 1) == slot_seq * WINDOW
    key_row = lax.broadcasted_iota(jnp.int32, (lk, 2 * HEAD_DIM), 0)
    live_key = (key_row >= n_cache) | (_mod_pow2(key_row, WINDOW) != 0)
    _attention(q_ref, k_all, v_all, mask, slot, n_cache, live_key, sink_ref, a_scr, t, lk)

    vn = _sgu_norm(vb_ref, gain_ref, bias_ref)
    vn_ref[...] = vn
    r = lax.broadcasted_iota(jnp.int32, (t, t), 0)
    c = lax.broadcasted_iota(jnp.int32, (t, t), 1)
    w_mask = (_div_pow2(r, DEC_SEQ) == _div_pow2(c, DEC_SEQ)) & (c <= r)
    _gate_and_store(za_ref, u_ref, zb_ref, ws_ref, bs_ref, a_scr, vn, w_mask, ab_ref)


def _proj_specs(rows, row_of):
    return [pl.BlockSpec((rows, D_ATTN), functools.partial(lambda s, *g: (row_of(*g), s), s))
            for s in range(5)]


def _mix_prompt(proj, kv, sinks, gain, bias, w_s, b_s):
    blocks = SEQ // WINDOW
    row_of = lambda b, n: b * blocks + n
    full = lambda shape: pl.BlockSpec(shape, lambda b, n: (0,) * len(shape))
    return pl.pallas_call(
        _mix_prompt_kernel,
        grid=(BATCH, blocks),
        in_specs=[pl.BlockSpec(memory_space=pltpu.SMEM)] + _proj_specs(WINDOW, row_of) + [
            pl.BlockSpec((WINDOW, 2 * D_KV), lambda b, n: (row_of(b, n), 0)),
            pl.BlockSpec((WINDOW, 2 * D_KV), lambda b, n: (row_of(b, jnp.maximum(n - 1, 0)), 0)),
            full((1, D_SGU)), full((1, D_SGU)),
            full((N_SGU_GROUPS, CHUNK, CHUNK)), full((CHUNK, N_SGU_GROUPS)),
        ],
        out_specs=pl.BlockSpec((WINDOW, D_MODEL), lambda b, n: (row_of(b, n), 0)),
        out_shape=jax.ShapeDtypeStruct((ROWS_P, D_MODEL), bf16),
        scratch_shapes=[pltpu.VMEM((WINDOW, D_ATTN), f32)],
        compiler_params=_params(2),
        name="mix_prompt",
    )(sinks, proj, proj, proj, proj, proj, kv, kv, gain, bias, w_s, b_s)


def _mix_sample(proj, kv, cache_k, cache_v, sinks, gain, bias, w_s, b_s):
    t = SEQ_GROUP * DEC_SEQ
    first = ROWS_P // t
    row_of = lambda g: first + g
    full = lambda shape: pl.BlockSpec(shape, lambda g: (0,) * len(shape))
    cache_spec = pl.BlockSpec((SEQ_GROUP, WINDOW, D_KV), lambda g: (g, 0, 0))
    return pl.pallas_call(
        _mix_sample_kernel,
        grid=(DEC_BATCH // SEQ_GROUP,),
        in_specs=[pl.BlockSpec(memory_space=pltpu.SMEM)] + _proj_specs(t, row_of) + [
            pl.BlockSpec((t, 2 * D_KV), lambda g: (row_of(g), 0)),
            cache_spec, cache_spec,
            full((1, D_SGU)), full((1, D_SGU)),
            full((N_SGU_GROUPS, t, t)), full((t, N_SGU_GROUPS)),
        ],
        out_specs=[pl.BlockSpec((t, D_MODEL), lambda g: (g, 0)),
                   pl.BlockSpec((t, D_SGU), lambda g: (g, 0))],
        out_shape=[jax.ShapeDtypeStruct((ROWS_S, D_MODEL), bf16),
                   jax.ShapeDtypeStruct((ROWS_S, D_SGU), f32)],
        scratch_shapes=[pltpu.VMEM((t, D_ATTN), f32)],
        compiler_params=_params(1),
        name="mix_sample",
    )(sinks, proj, proj, proj, proj, proj, kv, cache_k, cache_v, gain, bias, w_s, b_s)


def _merge_kernel(abp_ref, abs_ref, ga_ref, gb_ref, wa_ref, wb_ref, o_ref, wa_bf, wb_bf):
    i = pl.program_id(1)

    def merged(ab, ga, gb):
        ya = jnp.dot(ab[:, 0:D_ATTN], wa_bf[...], preferred_element_type=f32)
        yb = jnp.dot(ab[:, D_ATTN:], wb_bf[...], preferred_element_type=f32)
        return (_sigmoid(ga.astype(f32)) * ya + _sigmoid(gb.astype(f32)) * yb).astype(bf16)

    @pl.when(i == 0)
    def _():
        wa_bf[...] = wa_ref[...].astype(bf16)
        wb_bf[...] = wb_ref[...].astype(bf16)
        o_ref[0:ROWS_S, :] = merged(abs_ref[...], ga_ref[0:ROWS_S, :], gb_ref[0:ROWS_S, :])

    @pl.when(i > 0)
    def _():
        o_ref[...] = merged(abp_ref[...], ga_ref[...], gb_ref[...])


def _merge(ab_p, ab_s, proj, w_pa, w_pb, layer):
    n_tiles = D_MODEL // TN
    ga_tile = (PROJ_COLS - 2 * D_MODEL) // TN
    gb_tile = (PROJ_COLS - D_MODEL) // TN
    return pl.pallas_call(
        _merge_kernel,
        grid=(n_tiles, N_ROW_TILES),
        in_specs=[
            pl.BlockSpec((TM, D_MODEL), lambda j, i: (jnp.maximum(i - 1, 0), 0)),
            pl.BlockSpec((ROWS_S, D_MODEL), lambda j, i: (0, 0)),
            pl.BlockSpec((TM, TN), lambda j, i: (_row_tile(i), ga_tile + j)),
            pl.BlockSpec((TM, TN), lambda j, i: (_row_tile(i), gb_tile + j)),
            pl.BlockSpec((None, D_ATTN, TN), lambda j, i: (layer, 0, j)),
            pl.BlockSpec((None, D_SGU, TN), lambda j, i: (layer, 0, j)),
        ],
        out_specs=pl.BlockSpec((TM, TN), lambda j, i: (_row_tile(i), j)),
        out_shape=jax.ShapeDtypeStruct((ROWS, D_MODEL), bf16),
        scratch_shapes=[pltpu.VMEM((D_ATTN, TN), bf16), pltpu.VMEM((D_SGU, TN), bf16)],
        compiler_params=_params(2),
        name="merge",
    )(ab_p, ab_s, proj, proj, w_pa, w_pb)


def _post_norm(x, y, gate, g_ref, b_ref):
    t = ALPHA * x + gate * y.astype(f32)
    mu = jnp.mean(t, axis=1, keepdims=True)
    d = t - mu
    var = jnp.mean(d * d, axis=1, keepdims=True)
    return d * lax.rsqrt(var + LN_EPS) * g_ref[...] + b_ref[...]


def _ln_first_kernel(xp_ref, xs_ref, y_ref, gate_ref, nxt_ref, g_ref, b_ref, x_out, h_out):
    i = pl.program_id(0)
    n_p = ROWS_P // LN_TM

    @pl.when(i < n_p)
    def _():
        gate = _prompt_mod_row(gate_ref, 0, i, SEQ // LN_TM)
        xn = _post_norm(xp_ref[...], y_ref[...], gate, g_ref, b_ref)
        x_out[...] = xn
        shift = _prompt_mod_row(nxt_ref, 0, i, SEQ // LN_TM)
        scale = _prompt_mod_row(nxt_ref, 1, i, SEQ // LN_TM)
        h_out[...] = (xn * (1.0 + scale) + shift).astype(bf16)

    @pl.when(i == n_p)
    def _():
        xn = _post_norm(xs_ref[...], y_ref[0:ROWS_S, :], gate_ref[0, 0:ROWS_S, :], g_ref, b_ref)
        x_out[0:ROWS_S, :] = xn
        h_out[0:ROWS_S, :] = (xn * (1.0 + nxt_ref[1, 0:ROWS_S, :]) + nxt_ref[0, 0:ROWS_S, :]).astype(bf16)


def _ln_last_kernel(x_ref, y_ref, gate_ref, g_ref, b_ref, yp_out, ys_out):
    i = pl.program_id(0)
    n_p = ROWS_P // LN_TM

    @pl.when(i < n_p)
    def _():
        gate = _prompt_mod_row(gate_ref, 0, i, SEQ // LN_TM)
        yp_out[...] = _post_norm(x_ref[...], y_ref[...], gate, g_ref, b_ref)

    @pl.when(i == n_p)
    def _():
        ys_out[...] = _post_norm(x_ref[0:ROWS_S, :], y_ref[0:ROWS_S, :], gate_ref[0, 0:ROWS_S, :], g_ref, b_ref)


def _ln_first(xp, xs, y, mod, ln_g, ln_b, layer):
    n_p = ROWS_P // LN_TM
    row = lambda i: (i, 0)
    vec = pl.BlockSpec((1, D_MODEL), lambda i: (0, 0))
    return pl.pallas_call(
        _ln_first_kernel,
        grid=(n_p + 1,),
        in_specs=[
            pl.BlockSpec((LN_TM, D_MODEL), lambda i: (jnp.minimum(i, n_p - 1), 0)),
            pl.BlockSpec((ROWS_S, D_MODEL), lambda i: (0, 0)),
            pl.BlockSpec((LN_TM, D_MODEL), row),
            pl.BlockSpec((None, 1, MOD_ROWS, D_MODEL), lambda i: (layer, 2, 0, 0)),
            pl.BlockSpec((None, 2, MOD_ROWS, D_MODEL), lambda i: (layer + 1, 0, 0, 0)),
            vec, vec,
        ],
        out_specs=[pl.BlockSpec((LN_TM, D_MODEL), row), pl.BlockSpec((LN_TM, D_MODEL), row)],
        out_shape=[jax.ShapeDtypeStruct((ROWS, D_MODEL), f32), jax.ShapeDtypeStruct((ROWS, D_MODEL), bf16)],
        compiler_params=_params(1),
        name="ln_first",
    )(xp, xs, y, mod, mod, ln_g, ln_b)


def _ln_last(x, y, mod, ln_g, ln_b, layer):
    n_p = ROWS_P // LN_TM
    row = lambda i: (i, 0)
    vec = pl.BlockSpec((1, D_MODEL), lambda i: (0, 0))
    return pl.pallas_call(
        _ln_last_kernel,
        grid=(n_p + 1,),
        in_specs=[
            pl.BlockSpec((LN_TM, D_MODEL), row),
            pl.BlockSpec((LN_TM, D_MODEL), row),
            pl.BlockSpec((None, 1, MOD_ROWS, D_MODEL), lambda i: (layer, 2, 0, 0)),
            vec, vec,
        ],
        out_specs=[pl.BlockSpec((LN_TM, D_MODEL), lambda i: (jnp.minimum(i, n_p - 1), 0)),
                   pl.BlockSpec((ROWS_S, D_MODEL), lambda i: (0, 0))],
        out_shape=[jax.ShapeDtypeStruct((ROWS_P, D_MODEL), f32), jax.ShapeDtypeStruct((ROWS_S, D_MODEL), f32)],
        compiler_params=_params(1),
        name="ln_last",
    )(x, y, mod, ln_g, ln_b)


def kernel(x_prompt, x_sample, cache_k, cache_v, c_prompt, c_sample, w_ada, b_ada, w_in, attn_sinks,
           sgu_ln_gain, sgu_ln_bias, sgu_w_s, sgu_b_s, w_pa, w_pb, w_o, ln_gain, ln_bias):
    assert DEPTH == 2
    xp = x_prompt.reshape(ROWS_P, D_MODEL)
    xs = x_sample.reshape(ROWS_S, D_MODEL)
    c_all = jnp.concatenate([jnp.repeat(c_sample, DEC_SEQ, axis=0), c_prompt,
                             jnp.zeros((MOD_ROWS - ROWS_S - BATCH, D_MODEL), f32)], axis=0)
    mod = _ada(c_all, w_ada, b_ada)

    seq_t = SEQ_GROUP * DEC_SEQ
    win_k, win_v, new_k, new_v, sgu_v = [], [], [], [], []
    h = _modulate(xp, xs, mod)
    x_all = None
    for l in range(DEPTH):
        proj = _matmul(h, w_in, l, PROJ_COLS // TN, lambda j: j + (j >= KV_TILE).astype(jnp.int32), bf16, "proj",
                       tm=2048 if l == 0 else TM)
        kv = _matmul(h, w_in, l, 1, lambda j: j + KV_TILE, f32, "kv")
        gain = sgu_ln_gain[l].reshape(1, D_SGU)
        bias = sgu_ln_bias[l].reshape(1, D_SGU)
        ab_p = _mix_prompt(proj, kv, attn_sinks[l], gain, bias, sgu_w_s[l], sgu_b_s[l].T)
        w_small = jnp.tile(sgu_w_s[l][:, :DEC_SEQ, :DEC_SEQ], (1, SEQ_GROUP, SEQ_GROUP))
        b_small = jnp.tile(sgu_b_s[l][:, :DEC_SEQ].T, (SEQ_GROUP, 1))
        ab_s, vn_s = _mix_sample(proj, kv, cache_k[l].reshape(DEC_BATCH, WINDOW, D_KV),
                                 cache_v[l].reshape(DEC_BATCH, WINDOW, D_KV),
                                 attn_sinks[l], gain, bias, w_small, b_small)
        merged = _merge(ab_p, ab_s, proj, w_pa, w_pb, l)
        if l == 0:
            y = _matmul(merged, w_o, l, D_MODEL // TN, lambda j: j, bf16, "out", tm=2048)
        else:
            y = _matmul(merged, w_o, l, D_MODEL // 1024, lambda j: j, bf16, "out", tm=512, tn=1024)
        g, b = ln_gain[l].reshape(1, D_MODEL), ln_bias[l].reshape(1, D_MODEL)
        if l == 0:
            x_all, h = _ln_first(xp, xs, y, mod, g, b, l)
        else:
            y_p, y_s = _ln_last(x_all, y, mod, g, b, l)

        kv_p = kv[:ROWS_P].reshape(BATCH, SEQ, 2, N_KV_HEADS, HEAD_DIM)[:, SEQ - WINDOW:]
        kv_s = kv[ROWS_P:].reshape(DEC_BATCH, DEC_SEQ, 2, N_KV_HEADS, HEAD_DIM)
        win_k.append(kv_p[:, :, 0])
        win_v.append(kv_p[:, :, 1])
        new_k.append(kv_s[:, :, 0])
        new_v.append(kv_s[:, :, 1])
        sgu_v.append(vn_s.reshape(DEC_BATCH, DEC_SEQ, D_SGU))

    return (y_p.reshape(BATCH, SEQ, D_MODEL), y_s.reshape(DEC_BATCH, DEC_SEQ, D_MODEL),
            jnp.stack(win_k), jnp.stack(win_v), jnp.stack(new_k), jnp.stack(new_v), jnp.stack(sgu_v))
```

```python
import functools

import jax
import jax.numpy as jnp
from jax import lax
from jax.experimental import pallas as pl
from jax.experimental.pallas import tpu as pltpu

D_MODEL = 4096
BATCH = 4
SEQ = 2048
DEPTH = 2
DEC_BATCH = 32
DEC_SEQ = 4
HEAD_DIM = 64
D_ATTN = D_MODEL // 2
N_Q_HEADS = D_ATTN // HEAD_DIM
N_KV_HEADS = N_Q_HEADS // 8
GQA_GROUP = N_Q_HEADS // N_KV_HEADS
D_KV = N_KV_HEADS * HEAD_DIM
WINDOW = 128
D_SGU = D_MODEL // 2
N_SGU_GROUPS = 8
SGU_GROUP_DIM = D_SGU // N_SGU_GROUPS
CHUNK = 128
ALPHA = (2 * DEPTH) ** 0.25
LN_EPS = 1e-5
IN_COLS = D_ATTN + 2 * D_KV + D_ATTN + 3 * D_SGU + 2 * D_MODEL

ROWS_P = BATCH * SEQ
ROWS_S = DEC_BATCH * DEC_SEQ
MOD_ROWS = ROWS_S + 8
PROJ_COLS = IN_COLS - 2 * D_KV

TN = 512
TM = 1024
KV_TILE = D_ATTN // TN
LN_TM = 256
SEQ_GROUP = 8
NEG = -1e30
LOG2E = 1.4426950408889634

VMEM_LIMIT = 56 * 1024 * 1024

bf16 = jnp.bfloat16
f32 = jnp.float32


def _params(n_axes, vmem=VMEM_LIMIT):
    return pltpu.CompilerParams(dimension_semantics=("arbitrary",) * n_axes, vmem_limit_bytes=vmem)


def _sigmoid(x):
    return 0.5 + 0.5 * jnp.tanh(0.5 * x)


def _silu(x):
    half = 0.5 * x
    return half + half * jnp.tanh(half)


def _div_pow2(x, n):
    assert n & (n - 1) == 0
    return x >> (n.bit_length() - 1)


def _mod_pow2(x, n):
    assert n & (n - 1) == 0
    return x & (n - 1)


def _ada_kernel(c_ref, w_ref, b_ref, o_ref):
    a = _silu(c_ref[...]).astype(bf16)
    o_ref[...] = jnp.dot(a, w_ref[...].astype(bf16), preferred_element_type=f32) + b_ref[...]


def _ada(c_all, w_ada, b_ada):
    tiles_per_part = D_MODEL // TN
    return pl.pallas_call(
        _ada_kernel,
        grid=(DEPTH, 3 * tiles_per_part),
        in_specs=[
            pl.BlockSpec((MOD_ROWS, D_MODEL), lambda l, j: (0, 0)),
            pl.BlockSpec((None, D_MODEL, TN), lambda l, j: (l, 0, j)),
            pl.BlockSpec((None, 1, TN), lambda l, j: (l, 0, j)),
        ],
        out_specs=pl.BlockSpec((None, None, MOD_ROWS, TN),
                               lambda l, j: (l, j // tiles_per_part, 0, j % tiles_per_part)),
        out_shape=jax.ShapeDtypeStruct((DEPTH, 3, MOD_ROWS, D_MODEL), f32),
        compiler_params=_params(2),
        name="ada",
    )(c_all, w_ada, b_ada.reshape(DEPTH, 1, 3 * D_MODEL))


def _prompt_mod_row(ref, part, blk, blocks_per_batch):
    return ref[part, pl.ds(ROWS_S + blk // blocks_per_batch, 1), :]


LN_BLOCKS = ROWS_P // LN_TM


def _ln_row(i):
    return (jnp.minimum(i, LN_BLOCKS - 1), 0)


_LN_ROW_SPEC = pl.BlockSpec((LN_TM, D_MODEL), _ln_row)
_LN_SAMPLE_SPEC = pl.BlockSpec((ROWS_S, D_MODEL), lambda i: (0, 0))
_LN_VEC_SPEC = pl.BlockSpec((1, D_MODEL), lambda i: (0, 0))


def _modulate_kernel(xp_ref, xs_ref, mod_ref, hp_ref, hs_ref):
    i = pl.program_id(0)

    @pl.when(i < LN_BLOCKS)
    def _():
        shift = _prompt_mod_row(mod_ref, 0, i, SEQ // LN_TM)
        scale = _prompt_mod_row(mod_ref, 1, i, SEQ // LN_TM)
        hp_ref[...] = (xp_ref[...] * (1.0 + scale) + shift).astype(bf16)

    @pl.when(i == LN_BLOCKS)
    def _():
        hs_ref[...] = (xs_ref[...] * (1.0 + mod_ref[1, 0:ROWS_S, :]) + mod_ref[0, 0:ROWS_S, :]).astype(bf16)


def _modulate(xp, xs, mod):
    return pl.pallas_call(
        _modulate_kernel,
        grid=(LN_BLOCKS + 1,),
        in_specs=[_LN_ROW_SPEC, _LN_SAMPLE_SPEC,
                  pl.BlockSpec((None, 2, MOD_ROWS, D_MODEL), lambda i: (0, 0, 0, 0))],
        out_specs=[_LN_ROW_SPEC, _LN_SAMPLE_SPEC],
        out_shape=[jax.ShapeDtypeStruct((ROWS_P, D_MODEL), bf16), jax.ShapeDtypeStruct((ROWS_S, D_MODEL), bf16)],
        compiler_params=_params(1),
        name="modulate",
    )(xp, xs, mod)


def _matmul_kernel(*refs, n_w, tw):
    xp_ref, xs_ref = refs[0:2]
    w_refs = refs[2:2 + n_w]
    op_ref, os_ref, w_bf = refs[2 + n_w:]

    @pl.when(pl.program_id(1) == 0)
    def _():
        for a, w_ref in enumerate(w_refs):
            w_bf[:, a * tw:(a + 1) * tw] = w_ref[...].astype(bf16)
        os_ref[...] = jnp.dot(xs_ref[...], w_bf[...], preferred_element_type=f32).astype(os_ref.dtype)

    op_ref[...] = jnp.dot(xp_ref[...], w_bf[...], preferred_element_type=f32).astype(op_ref.dtype)


def _matmul(xp, xs, w, layer, n_tiles, w_tiles_of, out_dtype, name, tm=TM, tw=TN):
    k = xp.shape[1]
    n_w = len(w_tiles_of)
    tn = n_w * tw
    out_bytes = jnp.dtype(out_dtype).itemsize
    vmem = (2 * (tm + ROWS_S) * k * 2 + 2 * k * tn * 4 + k * tn * 2 + 2 * (tm + ROWS_S) * tn * out_bytes + (6 << 20))
    w_specs = [pl.BlockSpec((None, k, tw), functools.partial(lambda f, j, i: (layer, 0, f(j)), f))
               for f in w_tiles_of]
    return pl.pallas_call(
        functools.partial(_matmul_kernel, n_w=n_w, tw=tw),
        grid=(n_tiles, ROWS_P // tm),
        in_specs=[pl.BlockSpec((tm, k), lambda j, i: (i, 0)),
                  pl.BlockSpec((ROWS_S, k), lambda j, i: (0, 0))] + w_specs,
        out_specs=[pl.BlockSpec((tm, tn), lambda j, i: (i, j)),
                   pl.BlockSpec((ROWS_S, tn), lambda j, i: (0, j))],
        out_shape=[jax.ShapeDtypeStruct((ROWS_P, n_tiles * tn), out_dtype),
                   jax.ShapeDtypeStruct((ROWS_S, n_tiles * tn), out_dtype)],
        scratch_shapes=[pltpu.VMEM((k, tn), bf16)],
        compiler_params=_params(2, vmem),
        name=name,
    )(xp, xs, *([w] * n_w))


def _skip_kv_tile(c):
    return c + (c >= KV_TILE).astype(jnp.int32)


def _attention(q_ref, k_all, v_all, mask, slot, slot_cols, live_key, sink_ref, a_scr, t, lk):
    lane = lax.broadcasted_iota(jnp.int32, (lk, 2 * HEAD_DIM), 1)
    low, high = lane < HEAD_DIM, lane >= HEAD_DIM
    ones2 = jnp.concatenate([low, high], axis=0).astype(f32).astype(bf16)
    nt = (((1,), (1,)), ((), ()))
    pairs = GQA_GROUP // 2
    k_scale = HEAD_DIM ** -0.5 * LOG2E
    for kv_pair in range(N_KV_HEADS // 2):
        cols = slice(kv_pair * 2 * HEAD_DIM, (kv_pair + 1) * 2 * HEAD_DIM)
        kp, vp = k_all[:, cols] * k_scale, v_all[:, cols]
        for par in range(2):
            kv = 2 * kv_pair + par
            keep = (low if par == 0 else high) & live_key
            k_own = jnp.where(keep, kp, 0.0)
            v_own = jnp.where(keep, vp, 0.0)
            k_oth = pltpu.roll(k_own, HEAD_DIM, 1)
            v_oth = pltpu.roll(v_own, HEAD_DIM, 1)
            if par == 0:
                k2 = jnp.concatenate([k_own, k_oth], axis=0).astype(bf16)
                v2 = jnp.concatenate([v_own, v_oth], axis=0).astype(bf16)
            else:
                k2 = jnp.concatenate([k_oth, k_own], axis=0).astype(bf16)
                v2 = jnp.concatenate([v_oth, v_own], axis=0).astype(bf16)
            v2 = jnp.concatenate([v2, ones2], axis=1)
            base = kv * GQA_GROUP * HEAD_DIM
            q4 = jnp.concatenate(
                [q_ref[:, base + p * 128: base + (p + 1) * 128] for p in range(pairs)], axis=0)
            s_all = lax.dot_general(q4, k2, nt, preferred_element_type=f32)
            rows = []
            for p in range(pairs):
                halves = []
                for h in range(2):
                    sink = sink_ref[kv * GQA_GROUP + 2 * p + h] * LOG2E
                    fill = jnp.where(slot, sink, NEG)
                    if slot_cols < lk:
                        fill = jnp.concatenate([fill, jnp.full((t, lk - slot_cols), NEG, f32)], axis=1)
                    s = jnp.where(mask, s_all[p * t:(p + 1) * t, h * lk:(h + 1) * lk], fill)
                    m = jnp.max(s, axis=1, keepdims=True)
                    halves.append(jnp.exp2(s - m).astype(bf16))
                rows.append(jnp.concatenate(halves, axis=1))
            probs = jnp.concatenate(rows, axis=0)
            o = jnp.dot(probs, v2, preferred_element_type=f32)
            o = o[:, 0:128] / o[:, 128:256]
            for p in range(pairs):
                a_scr[:, base + p * 128: base + (p + 1) * 128] = o[p * t:(p + 1) * t, :]


def _sgu_norm(vb_ref, gain_ref, bias_ref):
    vb = vb_ref[...].astype(f32)
    mu = jnp.mean(vb, axis=1, keepdims=True)
    d = vb - mu
    var = jnp.mean(d * d, axis=1, keepdims=True)
    return d * lax.rsqrt(var + LN_EPS) * gain_ref[...] + bias_ref[...]


def _gate_and_store(za_ref, u_ref, zb_ref, ws_ref, bs_ref, a_scr, vn, w_mask, ab_ref):
    ab_ref[:, 0:D_ATTN] = (a_scr[...] * _silu(za_ref[...].astype(f32))).astype(bf16)
    vn_bf = vn.astype(bf16)
    for g in range(N_SGU_GROUPS):
        cols = slice(g * SGU_GROUP_DIM, (g + 1) * SGU_GROUP_DIM)
        w_g = jnp.where(w_mask, ws_ref[g], 0.0).astype(bf16)
        s = jnp.dot(w_g, vn_bf[:, cols], preferred_element_type=f32) + bs_ref[:, g:g + 1]
        out_b = u_ref[:, cols].astype(f32) * s * _silu(zb_ref[:, cols].astype(f32))
        ab_ref[:, D_ATTN + g * SGU_GROUP_DIM: D_ATTN + (g + 1) * SGU_GROUP_DIM] = out_b.astype(bf16)


def _mix_prompt_kernel(sink_ref, q_ref, za_ref, u_ref, vb_ref, zb_ref, kvc_ref, kvp_ref,
                       gain_ref, bias_ref, ws_ref, bs_ref, ab_ref, a_scr):
    n = pl.program_id(1)
    kvc, kvp = kvc_ref[...], kvp_ref[...]
    k_all = jnp.concatenate([kvp[:, 0:D_KV], kvc[:, 0:D_KV]], axis=0)
    v_all = jnp.concatenate([kvp[:, D_KV:], kvc[:, D_KV:]], axis=0)
    qi = lax.broadcasted_iota(jnp.int32, (WINDOW, 2 * WINDOW), 0)
    kj = lax.broadcasted_iota(jnp.int32, (WINDOW, 2 * WINDOW), 1)
    no_prev = jnp.where(n > 0, 0, 2 * WINDOW)
    mask = ((kj < WINDOW) & (kj > qi + no_prev)) | ((kj >= WINDOW) & (kj - WINDOW <= qi))
    slot = lax.broadcasted_iota(jnp.int32, (WINDOW, WINDOW), 1) == 0
    live_key = lax.broadcasted_iota(jnp.int32, (2 * WINDOW, 2 * HEAD_DIM), 0) != 0
    _attention(q_ref, k_all, v_all, mask, slot, WINDOW, live_key, sink_ref, a_scr, WINDOW, 2 * WINDOW)

    vn = _sgu_norm(vb_ref, gain_ref, bias_ref)
    r = lax.broadcasted_iota(jnp.int32, (CHUNK, CHUNK), 0)
    c = lax.broadcasted_iota(jnp.int32, (CHUNK, CHUNK), 1)
    _gate_and_store(za_ref, u_ref, zb_ref, ws_ref, bs_ref, a_scr, vn, c <= r, ab_ref)


def _mix_sample_kernel(sink_ref, q_ref, za_ref, u_ref, vb_ref, zb_ref, kvn_ref, ck_ref, cv_ref,
                       gain_ref, bias_ref, ws_ref, bs_ref, ab_ref, vn_ref, a_scr):
    t = SEQ_GROUP * DEC_SEQ
    n_cache = SEQ_GROUP * WINDOW
    lk = n_cache + WINDOW
    kvn = kvn_ref[...]
    pad = jnp.zeros((WINDOW - t, D_KV), f32)
    k_all = jnp.concatenate([ck_ref[...].reshape(n_cache, D_KV), kvn[:, 0:D_KV], pad], axis=0)
    v_all = jnp.concatenate([cv_ref[...].reshape(n_cache, D_KV), kvn[:, D_KV:], pad], axis=0)
    qi = lax.broadcasted_iota(jnp.int32, (t, lk), 0)
    kj = lax.broadcasted_iota(jnp.int32, (t, lk), 1)
    q_seq, q_tok = _div_pow2(qi, DEC_SEQ), _mod_pow2(qi, DEC_SEQ)
    in_cache = (_div_pow2(kj, WINDOW) == q_seq) & (_mod_pow2(kj, WINDOW) > q_tok)
    kn = jnp.maximum(kj - n_cache, 0)
    in_new = (kn < t) & (_div_pow2(kn, DEC_SEQ) == q_seq) & (_mod_pow2(kn, DEC_SEQ) <= q_tok)
    mask = ((kj < n_cache) & in_cache) | ((kj >= n_cache) & in_new)
    slot_seq = _div_pow2(lax.broadcasted_iota(jnp.int32, (t, n_cache), 0), DEC_SEQ)
    slot = lax.broadcasted_iota(jnp.int32, (t, n_cache), 1) == slot_seq * WINDOW
    key_row = lax.broadcasted_iota(jnp.int32, (lk, 2 * HEAD_DIM), 0)
    live_key = (key_row >= n_cache) | (_mod_pow2(key_row, WINDOW) != 0)
    _attention(q_ref, k_all, v_all, mask, slot, n_cache, live_key, sink_ref, a_scr, t, lk)

    vn = _sgu_norm(vb_ref, gain_ref, bias_ref)
    vn_ref[...] = vn
    r = lax.broadcasted_iota(jnp.int32, (t, t), 0)
    c = lax.broadcasted_iota(jnp.int32, (t, t), 1)
    w_mask = (_div_pow2(r, DEC_SEQ) == _div_pow2(c, DEC_SEQ)) & (c <= r)
    _gate_and_store(za_ref, u_ref, zb_ref, ws_ref, bs_ref, a_scr, vn, w_mask, ab_ref)


def _proj_specs(rows, row_of):
    return [pl.BlockSpec((rows, D_ATTN), functools.partial(lambda s, *g: (row_of(*g), s), s))
            for s in range(5)]


def _mix_prompt(proj, kv, sinks, gain, bias, w_s, b_s):
    blocks = SEQ // WINDOW
    row_of = lambda b, n: b * blocks + n
    full = lambda shape: pl.BlockSpec(shape, lambda b, n: (0,) * len(shape))
    return pl.pallas_call(
        _mix_prompt_kernel,
        grid=(BATCH, blocks),
        in_specs=[pl.BlockSpec(memory_space=pltpu.SMEM)] + _proj_specs(WINDOW, row_of) + [
            pl.BlockSpec((WINDOW, 2 * D_KV), lambda b, n: (row_of(b, n), 0)),
            pl.BlockSpec((WINDOW, 2 * D_KV), lambda b, n: (row_of(b, jnp.maximum(n - 1, 0)), 0)),
            full((1, D_SGU)), full((1, D_SGU)),
            full((N_SGU_GROUPS, CHUNK, CHUNK)), full((CHUNK, N_SGU_GROUPS)),
        ],
        out_specs=pl.BlockSpec((WINDOW, D_MODEL), lambda b, n: (row_of(b, n), 0)),
        out_shape=jax.ShapeDtypeStruct((ROWS_P, D_MODEL), bf16),
        scratch_shapes=[pltpu.VMEM((WINDOW, D_ATTN), f32)],
        compiler_params=_params(2),
        name="mix_prompt",
    )(sinks, proj, proj, proj, proj, proj, kv, kv, gain, bias, w_s, b_s)


def _mix_sample(proj, kv, cache_k, cache_v, layer, sinks, gain, bias, w_s, b_s):
    t = SEQ_GROUP * DEC_SEQ
    groups = DEC_BATCH // SEQ_GROUP
    full = lambda shape: pl.BlockSpec(shape, lambda g: (0,) * len(shape))
    cache_spec = pl.BlockSpec((SEQ_GROUP, WINDOW, D_KV), lambda g: (layer * groups + g, 0, 0))
    return pl.pallas_call(
        _mix_sample_kernel,
        grid=(groups,),
        in_specs=[pl.BlockSpec(memory_space=pltpu.SMEM)] + _proj_specs(t, lambda g: g) + [
            pl.BlockSpec((t, 2 * D_KV), lambda g: (g, 0)),
            cache_spec, cache_spec,
            full((1, D_SGU)), full((1, D_SGU)),
            full((N_SGU_GROUPS, t, t)), full((t, N_SGU_GROUPS)),
        ],
        out_specs=[pl.BlockSpec((t, D_MODEL), lambda g: (g, 0)),
                   pl.BlockSpec((t, D_SGU), lambda g: (g, 0))],
        out_shape=[jax.ShapeDtypeStruct((ROWS_S, D_MODEL), bf16),
                   jax.ShapeDtypeStruct((ROWS_S, D_SGU), f32)],
        scratch_shapes=[pltpu.VMEM((t, D_ATTN), f32)],
        compiler_params=_params(1),
        name="mix_sample",
    )(sinks, proj, proj, proj, proj, proj, kv, cache_k, cache_v, gain, bias, w_s, b_s)


def _merge_kernel(abp_ref, abs_ref, gap_ref, gbp_ref, gas_ref, gbs_ref, wa_ref, wb_ref,
                  op_ref, os_ref, wa_bf, wb_bf):
    def merged(ab_ref, ga_ref, gb_ref):
        ya = jnp.dot(ab_ref[:, 0:D_ATTN], wa_bf[...], preferred_element_type=f32)
        yb = jnp.dot(ab_ref[:, D_ATTN:], wb_bf[...], preferred_element_type=f32)
        return (_sigmoid(ga_ref[...].astype(f32)) * ya + _sigmoid(gb_ref[...].astype(f32)) * yb).astype(bf16)

    @pl.when(pl.program_id(1) == 0)
    def _():
        wa_bf[...] = wa_ref[...].astype(bf16)
        wb_bf[...] = wb_ref[...].astype(bf16)
        os_ref[...] = merged(abs_ref, gas_ref, gbs_ref)

    op_ref[...] = merged(abp_ref, gap_ref, gbp_ref)


def _merge(ab_p, ab_s, proj_p, proj_s, w_pa, w_pb, layer):
    n_tiles = D_MODEL // TN
    ga_tile = (PROJ_COLS - 2 * D_MODEL) // TN
    gb_tile = (PROJ_COLS - D_MODEL) // TN
    return pl.pallas_call(
        _merge_kernel,
        grid=(n_tiles, ROWS_P // TM),
        in_specs=[
            pl.BlockSpec((TM, D_MODEL), lambda j, i: (i, 0)),
            pl.BlockSpec((ROWS_S, D_MODEL), lambda j, i: (0, 0)),
            pl.BlockSpec((TM, TN), lambda j, i: (i, ga_tile + j)),
            pl.BlockSpec((TM, TN), lambda j, i: (i, gb_tile + j)),
            pl.BlockSpec((ROWS_S, TN), lambda j, i: (0, ga_tile + j)),
            pl.BlockSpec((ROWS_S, TN), lambda j, i: (0, gb_tile + j)),
            pl.BlockSpec((None, D_ATTN, TN), lambda j, i: (layer, 0, j)),
            pl.BlockSpec((None, D_SGU, TN), lambda j, i: (layer, 0, j)),
        ],
        out_specs=[pl.BlockSpec((TM, TN), lambda j, i: (i, j)),
                   pl.BlockSpec((ROWS_S, TN), lambda j, i: (0, j))],
        out_shape=[jax.ShapeDtypeStruct((ROWS_P, D_MODEL), bf16), jax.ShapeDtypeStruct((ROWS_S, D_MODEL), bf16)],
        scratch_shapes=[pltpu.VMEM((D_ATTN, TN), bf16), pltpu.VMEM((D_SGU, TN), bf16)],
        compiler_params=_params(2),
        name="merge",
    )(ab_p, ab_s, proj_p, proj_p, proj_s, proj_s, w_pa, w_pb)


def _post_norm(x, y, gate, g_ref, b_ref):
    t = ALPHA * x + gate * y.astype(f32)
    mu = jnp.mean(t, axis=1, keepdims=True)
    d = t - mu
    var = jnp.mean(d * d, axis=1, keepdims=True)
    return d * lax.rsqrt(var + LN_EPS) * g_ref[...] + b_ref[...]


def _ln_first_kernel(xp_ref, xs_ref, yp_ref, ys_ref, gate_ref, nxt_ref, g_ref, b_ref,
                     xp_out, xs_out, hp_out, hs_out):
    i = pl.program_id(0)

    @pl.when(i < LN_BLOCKS)
    def _():
        gate = _prompt_mod_row(gate_ref, 0, i, SEQ // LN_TM)
        xn = _post_norm(xp_ref[...], yp_ref[...], gate, g_ref, b_ref)
        xp_out[...] = xn
        shift = _prompt_mod_row(nxt_ref, 0, i, SEQ // LN_TM)
        scale = _prompt_mod_row(nxt_ref, 1, i, SEQ // LN_TM)
        hp_out[...] = (xn * (1.0 + scale) + shift).astype(bf16)

    @pl.when(i == LN_BLOCKS)
    def _():
        xn = _post_norm(xs_ref[...], ys_ref[...], gate_ref[0, 0:ROWS_S, :], g_ref, b_ref)
        xs_out[...] = xn
        hs_out[...] = (xn * (1.0 + nxt_ref[1, 0:ROWS_S, :]) + nxt_ref[0, 0:ROWS_S, :]).astype(bf16)


def _ln_last_kernel(xp_ref, xs_ref, yp_ref, ys_ref, gate_ref, g_ref, b_ref, xp_out, xs_out):
    i = pl.program_id(0)

    @pl.when(i < LN_BLOCKS)
    def _():
        gate = _prompt_mod_row(gate_ref, 0, i, SEQ // LN_TM)
        xp_out[...] = _post_norm(xp_ref[...], yp_ref[...], gate, g_ref, b_ref)

    @pl.when(i == LN_BLOCKS)
    def _():
        xs_out[...] = _post_norm(xs_ref[...], ys_ref[...], gate_ref[0, 0:ROWS_S, :], g_ref, b_ref)


def _ln(xp, xs, yp, ys, mod, ln_g, ln_b, layer):
    last = layer == DEPTH - 1
    gate_spec = pl.BlockSpec((None, 1, MOD_ROWS, D_MODEL), lambda i: (layer, 2, 0, 0))
    in_specs = [_LN_ROW_SPEC, _LN_SAMPLE_SPEC, _LN_ROW_SPEC, _LN_SAMPLE_SPEC, gate_spec]
    args = [xp, xs, yp, ys, mod]
    out_specs = [_LN_ROW_SPEC, _LN_SAMPLE_SPEC]
    out_shape = [jax.ShapeDtypeStruct((ROWS_P, D_MODEL), f32), jax.ShapeDtypeStruct((ROWS_S, D_MODEL), f32)]
    if not last:
        in_specs.append(pl.BlockSpec((None, 2, MOD_ROWS, D_MODEL), lambda i: (layer + 1, 0, 0, 0)))
        args.append(mod)
        out_specs += [_LN_ROW_SPEC, _LN_SAMPLE_SPEC]
        out_shape += [jax.ShapeDtypeStruct((ROWS_P, D_MODEL), bf16), jax.ShapeDtypeStruct((ROWS_S, D_MODEL), bf16)]
    return pl.pallas_call(
        _ln_last_kernel if last else _ln_first_kernel,
        grid=(LN_BLOCKS + 1,),
        in_specs=in_specs + [_LN_VEC_SPEC, _LN_VEC_SPEC],
        out_specs=out_specs,
        out_shape=out_shape,
        compiler_params=_params(1),
        name="ln_last" if last else "ln_first",
    )(*args, ln_g, ln_b)


def kernel(x_prompt, x_sample, cache_k, cache_v, c_prompt, c_sample, w_ada, b_ada, w_in, attn_sinks,
           sgu_ln_gain, sgu_ln_bias, sgu_w_s, sgu_b_s, w_pa, w_pb, w_o, ln_gain, ln_bias):
    xp = x_prompt.reshape(ROWS_P, D_MODEL)
    xs = x_sample.reshape(ROWS_S, D_MODEL)
    c_all = jnp.concatenate([jnp.repeat(c_sample, DEC_SEQ, axis=0), c_prompt,
                             jnp.zeros((MOD_ROWS - ROWS_S - BATCH, D_MODEL), f32)], axis=0)
    mod = _ada(c_all, w_ada, b_ada)
    ck = cache_k.reshape(DEPTH * DEC_BATCH, WINDOW, D_KV)
    cv = cache_v.reshape(DEPTH * DEC_BATCH, WINDOW, D_KV)

    win_k, win_v, new_k, new_v, sgu_v = [], [], [], [], []
    hp, hs = _modulate(xp, xs, mod)
    for l in range(DEPTH):
        if l == 0:
            proj_p, proj_s = _matmul(hp, hs, w_in, l, PROJ_COLS // TN, [_skip_kv_tile], bf16, "proj")
        else:
            proj_p, proj_s = _matmul(hp, hs, w_in, l, PROJ_COLS // (2 * TN),
                                     [lambda j: _skip_kv_tile(2 * j), lambda j: _skip_kv_tile(2 * j + 1)],
                                     bf16, "proj", tm=512)
        kv_p, kv_s = _matmul(hp, hs, w_in, l, 1, [lambda j: j + KV_TILE], f32, "kv")
        gain = sgu_ln_gain[l].reshape(1, D_SGU)
        bias = sgu_ln_bias[l].reshape(1, D_SGU)
        ab_p = _mix_prompt(proj_p, kv_p, attn_sinks[l], gain, bias, sgu_w_s[l], sgu_b_s[l].T)
        w_small = jnp.tile(sgu_w_s[l][:, :DEC_SEQ, :DEC_SEQ], (1, SEQ_GROUP, SEQ_GROUP))
        b_small = jnp.tile(sgu_b_s[l][:, :DEC_SEQ].T, (SEQ_GROUP, 1))
        ab_s, vn_s = _mix_sample(proj_s, kv_s, ck, cv, l, attn_sinks[l], gain, bias, w_small, b_small)
        m_p, m_s = _merge(ab_p, ab_s, proj_p, proj_s, w_pa, w_pb, l)
        if l == 0:
            y_p, y_s = _matmul(m_p, m_s, w_o, l, D_MODEL // TN, [lambda j: j], bf16, "out")
        else:
            y_p, y_s = _matmul(m_p, m_s, w_o, l, D_MODEL // 1024, [lambda j: j], bf16, "out", tm=512, tw=1024)
        g, b = ln_gain[l].reshape(1, D_MODEL), ln_bias[l].reshape(1, D_MODEL)
        if l < DEPTH - 1:
            xp, xs, hp, hs = _ln(xp, xs, y_p, y_s, mod, g, b, l)
        else:
            xp, xs = _ln(xp, xs, y_p, y_s, mod, g, b, l)

        kv_win = kv_p.reshape(BATCH, SEQ, 2, N_KV_HEADS, HEAD_DIM)[:, SEQ - WINDOW:]
        kv_new = kv_s.reshape(DEC_BATCH, DEC_SEQ, 2, N_KV_HEADS, HEAD_DIM)
        win_k.append(kv_win[:, :, 0])
        win_v.append(kv_win[:, :, 1])
        new_k.append(kv_new[:, :, 0])
        new_v.append(kv_new[:, :, 1])
        sgu_v.append(vn_s.reshape(DEC_BATCH, DEC_SEQ, D_SGU))

    return (xp.reshape(BATCH, SEQ, D_MODEL), xs.reshape(DEC_BATCH, DEC_SEQ, D_MODEL),
            jnp.stack(win_k), jnp.stack(win_v), jnp.stack(new_k), jnp.stack(new_v), jnp.stack(sgu_v))
```

```python
import functools

import jax
import jax.numpy as jnp
from jax import lax
from jax.experimental import pallas as pl
from jax.experimental.pallas import tpu as pltpu

D_MODEL = 4096
BATCH = 4
SEQ = 2048
DEPTH = 2
DEC_BATCH = 32
DEC_SEQ = 4
HEAD_DIM = 64
D_ATTN = D_MODEL // 2
N_Q_HEADS = D_ATTN // HEAD_DIM
N_KV_HEADS = N_Q_HEADS // 8
GQA_GROUP = N_Q_HEADS // N_KV_HEADS
D_KV = N_KV_HEADS * HEAD_DIM
WINDOW = 128
D_SGU = D_MODEL // 2
N_SGU_GROUPS = 8
SGU_GROUP_DIM = D_SGU // N_SGU_GROUPS
CHUNK = 128
ALPHA = (2 * DEPTH) ** 0.25
LN_EPS = 1e-5
IN_COLS = D_ATTN + 2 * D_KV + D_ATTN + 3 * D_SGU + 2 * D_MODEL

ROWS_P = BATCH * SEQ
ROWS_S = DEC_BATCH * DEC_SEQ
MOD_ROWS = ROWS_S + 8
PROJ_COLS = IN_COLS - 2 * D_KV

TN = 512
TM = 1024
KV_TILE = D_ATTN // TN
LN_TM = 256
SEQ_GROUP = 8
NEG = -1e30
LOG2E = 1.4426950408889634

VMEM_LIMIT = 56 * 1024 * 1024

bf16 = jnp.bfloat16
f32 = jnp.float32


def _params(n_axes, vmem=VMEM_LIMIT):
    return pltpu.CompilerParams(dimension_semantics=("arbitrary",) * n_axes, vmem_limit_bytes=vmem)


def _sigmoid(x):
    return 0.5 + 0.5 * jnp.tanh(0.5 * x)


def _silu(x):
    half = 0.5 * x
    return half + half * jnp.tanh(half)


def _div_pow2(x, n):
    assert n & (n - 1) == 0
    return x >> (n.bit_length() - 1)


def _mod_pow2(x, n):
    assert n & (n - 1) == 0
    return x & (n - 1)


def _ada_kernel(c_ref, w_ref, b_ref, o_ref):
    a = _silu(c_ref[...]).astype(bf16)
    o_ref[...] = jnp.dot(a, w_ref[...].astype(bf16), preferred_element_type=f32) + b_ref[...]


def _ada(c_all, w_ada, b_ada):
    tiles_per_part = D_MODEL // TN
    return pl.pallas_call(
        _ada_kernel,
        grid=(DEPTH, 3 * tiles_per_part),
        in_specs=[
            pl.BlockSpec((MOD_ROWS, D_MODEL), lambda l, j: (0, 0)),
            pl.BlockSpec((None, D_MODEL, TN), lambda l, j: (l, 0, j)),
            pl.BlockSpec((None, 1, TN), lambda l, j: (l, 0, j)),
        ],
        out_specs=pl.BlockSpec((None, None, MOD_ROWS, TN),
                               lambda l, j: (l, j // tiles_per_part, 0, j % tiles_per_part)),
        out_shape=jax.ShapeDtypeStruct((DEPTH, 3, MOD_ROWS, D_MODEL), f32),
        compiler_params=_params(2),
        name="ada",
    )(c_all, w_ada, b_ada.reshape(DEPTH, 1, 3 * D_MODEL))


def _prompt_mod_row(ref, part, blk, blocks_per_batch):
    return ref[part, pl.ds(ROWS_S + blk // blocks_per_batch, 1), :]


LN_BLOCKS = ROWS_P // LN_TM


def _ln_row(i):
    return (jnp.minimum(i, LN_BLOCKS - 1), 0)


_LN_ROW_SPEC = pl.BlockSpec((LN_TM, D_MODEL), _ln_row)
_LN_SAMPLE_SPEC = pl.BlockSpec((ROWS_S, D_MODEL), lambda i: (0, 0))
_LN_VEC_SPEC = pl.BlockSpec((1, D_MODEL), lambda i: (0, 0))


def _modulate_kv_kernel(xp_ref, xs_ref, mod_ref, w_ref, hp_ref, hs_ref, kvp_ref, kvs_ref, w_bf):
    i = pl.program_id(0)

    @pl.when(i == 0)
    def _():
        w_bf[...] = w_ref[...].astype(bf16)

    @pl.when(i < LN_BLOCKS)
    def _():
        shift = _prompt_mod_row(mod_ref, 0, i, SEQ // LN_TM)
        scale = _prompt_mod_row(mod_ref, 1, i, SEQ // LN_TM)
        h = (xp_ref[...] * (1.0 + scale) + shift).astype(bf16)
        hp_ref[...] = h
        kvp_ref[...] = jnp.dot(h, w_bf[...], preferred_element_type=f32)

    @pl.when(i == LN_BLOCKS)
    def _():
        h = (xs_ref[...] * (1.0 + mod_ref[1, 0:ROWS_S, :]) + mod_ref[0, 0:ROWS_S, :]).astype(bf16)
        hs_ref[...] = h
        kvs_ref[...] = jnp.dot(h, w_bf[...], preferred_element_type=f32)


def _modulate_kv(xp, xs, mod, w_in):
    kv_row_spec = pl.BlockSpec((LN_TM, 2 * D_KV), _ln_row)
    kv_sample_spec = pl.BlockSpec((ROWS_S, 2 * D_KV), lambda i: (0, 0))
    return pl.pallas_call(
        _modulate_kv_kernel,
        grid=(LN_BLOCKS + 1,),
        in_specs=[_LN_ROW_SPEC, _LN_SAMPLE_SPEC,
                  pl.BlockSpec((None, 2, MOD_ROWS, D_MODEL), lambda i: (0, 0, 0, 0)),
                  pl.BlockSpec((None, D_MODEL, TN), lambda i: (0, 0, KV_TILE), pipeline_mode=pl.Buffered(1))],
        out_specs=[_LN_ROW_SPEC, _LN_SAMPLE_SPEC, kv_row_spec, kv_sample_spec],
        out_shape=[jax.ShapeDtypeStruct((ROWS_P, D_MODEL), bf16), jax.ShapeDtypeStruct((ROWS_S, D_MODEL), bf16),
                   jax.ShapeDtypeStruct((ROWS_P, 2 * D_KV), f32), jax.ShapeDtypeStruct((ROWS_S, 2 * D_KV), f32)],
        scratch_shapes=[pltpu.VMEM((D_MODEL, TN), bf16)],
        compiler_params=_params(1),
        name="modulate_kv",
    )(xp, xs, mod, w_in)


def _matmul_kernel(*refs, n_w, tw):
    xp_ref, xs_ref = refs[0:2]
    w_refs = refs[2:2 + n_w]
    op_ref, os_ref, w_bf = refs[2 + n_w:]

    @pl.when(pl.program_id(1) == 0)
    def _():
        for a, w_ref in enumerate(w_refs):
            w_bf[:, a * tw:(a + 1) * tw] = w_ref[...].astype(bf16)
        os_ref[...] = jnp.dot(xs_ref[...], w_bf[...], preferred_element_type=f32).astype(os_ref.dtype)

    op_ref[...] = jnp.dot(xp_ref[...], w_bf[...], preferred_element_type=f32).astype(op_ref.dtype)


def _matmul(xp, xs, w, layer, n_tiles, w_tiles_of, out_dtype, name, tm=TM, tw=TN):
    k = xp.shape[1]
    n_w = len(w_tiles_of)
    tn = n_w * tw
    out_bytes = jnp.dtype(out_dtype).itemsize
    vmem = (2 * (tm + ROWS_S) * k * 2 + 2 * k * tn * 4 + k * tn * 2 + 2 * (tm + ROWS_S) * tn * out_bytes + (6 << 20))
    w_specs = [pl.BlockSpec((None, k, tw), functools.partial(lambda f, j, i: (layer, 0, f(j)), f))
               for f in w_tiles_of]
    return pl.pallas_call(
        functools.partial(_matmul_kernel, n_w=n_w, tw=tw),
        grid=(n_tiles, ROWS_P // tm),
        in_specs=[pl.BlockSpec((tm, k), lambda j, i: (i, 0)),
                  pl.BlockSpec((ROWS_S, k), lambda j, i: (0, 0))] + w_specs,
        out_specs=[pl.BlockSpec((tm, tn), lambda j, i: (i, j)),
                   pl.BlockSpec((ROWS_S, tn), lambda j, i: (0, j))],
        out_shape=[jax.ShapeDtypeStruct((ROWS_P, n_tiles * tn), out_dtype),
                   jax.ShapeDtypeStruct((ROWS_S, n_tiles * tn), out_dtype)],
        scratch_shapes=[pltpu.VMEM((k, tn), bf16)],
        compiler_params=_params(2, vmem),
        name=name,
    )(xp, xs, *([w] * n_w))


def _skip_kv_tile(c):
    return c + (c >= KV_TILE).astype(jnp.int32)


GA_COL = 5 * D_ATTN
GB_COL = GA_COL + D_MODEL


def _attention(q_ref, k_all, v_all, mask, slot, slot_cols, live_key, sink_ref, za_ref, ab_ref, t, lk):
    lane = lax.broadcasted_iota(jnp.int32, (lk, 2 * HEAD_DIM), 1)
    low, high = lane < HEAD_DIM, lane >= HEAD_DIM
    ones2 = jnp.concatenate([low, high], axis=0).astype(f32).astype(bf16)
    nt = (((1,), (1,)), ((), ()))
    pairs = GQA_GROUP // 2
    k_scale = HEAD_DIM ** -0.5 * LOG2E
    for kv_pair in range(N_KV_HEADS // 2):
        cols = slice(kv_pair * 2 * HEAD_DIM, (kv_pair + 1) * 2 * HEAD_DIM)
        kp, vp = k_all[:, cols] * k_scale, v_all[:, cols]
        for par in range(2):
            kv = 2 * kv_pair + par
            keep = (low if par == 0 else high) & live_key
            k_own = jnp.where(keep, kp, 0.0)
            v_own = jnp.where(keep, vp, 0.0)
            k_oth = pltpu.roll(k_own, HEAD_DIM, 1)
            v_oth = pltpu.roll(v_own, HEAD_DIM, 1)
            if par == 0:
                k2 = jnp.concatenate([k_own, k_oth], axis=0).astype(bf16)
                v2 = jnp.concatenate([v_own, v_oth], axis=0).astype(bf16)
            else:
                k2 = jnp.concatenate([k_oth, k_own], axis=0).astype(bf16)
                v2 = jnp.concatenate([v_oth, v_own], axis=0).astype(bf16)
            v2 = jnp.concatenate([v2, ones2], axis=1)
            base = kv * GQA_GROUP * HEAD_DIM
            q4 = jnp.concatenate(
                [q_ref[:, base + p * 128: base + (p + 1) * 128] for p in range(pairs)], axis=0)
            s_all = lax.dot_general(q4, k2, nt, preferred_element_type=f32)
            rows = []
            for p in range(pairs):
                halves = []
                for h in range(2):
                    sink = sink_ref[kv * GQA_GROUP + 2 * p + h] * LOG2E
                    fill = jnp.where(slot, sink, NEG)
                    if slot_cols < lk:
                        fill = jnp.concatenate([fill, jnp.full((t, lk - slot_cols), NEG, f32)], axis=1)
                    s = jnp.where(mask, s_all[p * t:(p + 1) * t, h * lk:(h + 1) * lk], fill)
                    m = jnp.max(s, axis=1, keepdims=True)
                    halves.append(jnp.exp2(s - m).astype(bf16))
                rows.append(jnp.concatenate(halves, axis=1))
            probs = jnp.concatenate(rows, axis=0)
            o = jnp.dot(probs, v2, preferred_element_type=f32)
            o = o[:, 0:128] / o[:, 128:256]
            for p in range(pairs):
                cols = slice(base + p * 128, base + (p + 1) * 128)
                ab_ref[:, cols] = (o[p * t:(p + 1) * t, :] * _silu(za_ref[:, cols].astype(f32))).astype(bf16)


def _sgu_norm(vb_ref, gain_ref, bias_ref):
    vb = vb_ref[...].astype(f32)
    mu = jnp.mean(vb, axis=1, keepdims=True)
    d = vb - mu
    var = jnp.mean(d * d, axis=1, keepdims=True)
    return d * lax.rsqrt(var + LN_EPS) * gain_ref[...] + bias_ref[...]


def _sgu_and_store(u_ref, zb_ref, ws_ref, bs_ref, vn, w_mask, ab_ref):
    vn_bf = vn.astype(bf16)
    for g in range(N_SGU_GROUPS):
        cols = slice(g * SGU_GROUP_DIM, (g + 1) * SGU_GROUP_DIM)
        w_g = jnp.where(w_mask, ws_ref[g], 0.0).astype(bf16)
        s = jnp.dot(w_g, vn_bf[:, cols], preferred_element_type=f32) + bs_ref[:, g:g + 1]
        out_b = u_ref[:, cols].astype(f32) * s * _silu(zb_ref[:, cols].astype(f32))
        ab_ref[:, D_ATTN + g * SGU_GROUP_DIM: D_ATTN + (g + 1) * SGU_GROUP_DIM] = out_b.astype(bf16)


def _mix_prompt_kernel(sink_ref, q_ref, za_ref, u_ref, vb_ref, zb_ref, kvc_ref, kvp_ref,
                       gain_ref, bias_ref, ws_ref, bs_ref, ab_ref):
    n = pl.program_id(1)
    kvc, kvp = kvc_ref[...], kvp_ref[...]
    k_all = jnp.concatenate([kvp[:, 0:D_KV], kvc[:, 0:D_KV]], axis=0)
    v_all = jnp.concatenate([kvp[:, D_KV:], kvc[:, D_KV:]], axis=0)
    qi = lax.broadcasted_iota(jnp.int32, (WINDOW, 2 * WINDOW), 0)
    kj = lax.broadcasted_iota(jnp.int32, (WINDOW, 2 * WINDOW), 1)
    no_prev = jnp.where(n > 0, 0, 2 * WINDOW)
    mask = ((kj < WINDOW) & (kj > qi + no_prev)) | ((kj >= WINDOW) & (kj - WINDOW <= qi))
    slot = lax.broadcasted_iota(jnp.int32, (WINDOW, WINDOW), 1) == 0
    live_key = lax.broadcasted_iota(jnp.int32, (2 * WINDOW, 2 * HEAD_DIM), 0) != 0
    _attention(q_ref, k_all, v_all, mask, slot, WINDOW, live_key, sink_ref, za_ref, ab_ref, WINDOW, 2 * WINDOW)

    vn = _sgu_norm(vb_ref, gain_ref, bias_ref)
    r = lax.broadcasted_iota(jnp.int32, (CHUNK, CHUNK), 0)
    c = lax.broadcasted_iota(jnp.int32, (CHUNK, CHUNK), 1)
    _sgu_and_store(u_ref, zb_ref, ws_ref, bs_ref, vn, c <= r, ab_ref)


def _mix_sample_kernel(sink_ref, q_ref, za_ref, u_ref, vb_ref, zb_ref, kvn_ref, ck_ref, cv_ref,
                       gain_ref, bias_ref, ws_ref, bs_ref, ab_ref, vn_ref):
    t = SEQ_GROUP * DEC_SEQ
    n_cache = SEQ_GROUP * WINDOW
    lk = n_cache + WINDOW
    kvn = kvn_ref[...]
    pad = jnp.zeros((WINDOW - t, D_KV), f32)
    k_all = jnp.concatenate([ck_ref[...].reshape(n_cache, D_KV), kvn[:, 0:D_KV], pad], axis=0)
    v_all = jnp.concatenate([cv_ref[...].reshape(n_cache, D_KV), kvn[:, D_KV:], pad], axis=0)
    qi = lax.broadcasted_iota(jnp.int32, (t, lk), 0)
    kj = lax.broadcasted_iota(jnp.int32, (t, lk), 1)
    q_seq, q_tok = _div_pow2(qi, DEC_SEQ), _mod_pow2(qi, DEC_SEQ)
    in_cache = (_div_pow2(kj, WINDOW) == q_seq) & (_mod_pow2(kj, WINDOW) > q_tok)
    kn = jnp.maximum(kj - n_cache, 0)
    in_new = (kn < t) & (_div_pow2(kn, DEC_SEQ) == q_seq) & (_mod_pow2(kn, DEC_SEQ) <= q_tok)
    mask = ((kj < n_cache) & in_cache) | ((kj >= n_cache) & in_new)
    slot_seq = _div_pow2(lax.broadcasted_iota(jnp.int32, (t, n_cache), 0), DEC_SEQ)
    slot = lax.broadcasted_iota(jnp.int32, (t, n_cache), 1) == slot_seq * WINDOW
    key_row = lax.broadcasted_iota(jnp.int32, (lk, 2 * HEAD_DIM), 0)
    live_key = (key_row >= n_cache) | (_mod_pow2(key_row, WINDOW) != 0)
    _attention(q_ref, k_all, v_all, mask, slot, n_cache, live_key, sink_ref, za_ref, ab_ref, t, lk)

    vn = _sgu_norm(vb_ref, gain_ref, bias_ref)
    vn_ref[...] = vn
    r = lax.broadcasted_iota(jnp.int32, (t, t), 0)
    c = lax.broadcasted_iota(jnp.int32, (t, t), 1)
    w_mask = (_div_pow2(r, DEC_SEQ) == _div_pow2(c, DEC_SEQ)) & (c <= r)
    _sgu_and_store(u_ref, zb_ref, ws_ref, bs_ref, vn, w_mask, ab_ref)


def _proj_specs(rows, row_of):
    return [pl.BlockSpec((rows, D_ATTN), functools.partial(lambda s, *g: (row_of(*g), s), s))
            for s in range(5)]


def _mix_prompt(proj, kv, sinks, gain, bias, w_s, b_s):
    blocks = SEQ // WINDOW
    row_of = lambda b, n: b * blocks + n
    full = lambda shape: pl.BlockSpec(shape, lambda b, n: (0,) * len(shape))
    return pl.pallas_call(
        _mix_prompt_kernel,
        grid=(BATCH, blocks),
        in_specs=[pl.BlockSpec(memory_space=pltpu.SMEM)] + _proj_specs(WINDOW, row_of) + [
            pl.BlockSpec((WINDOW, 2 * D_KV), lambda b, n: (row_of(b, n), 0)),
            pl.BlockSpec((WINDOW, 2 * D_KV), lambda b, n: (row_of(b, jnp.maximum(n - 1, 0)), 0)),
            full((1, D_SGU)), full((1, D_SGU)),
            full((N_SGU_GROUPS, CHUNK, CHUNK)), full((CHUNK, N_SGU_GROUPS)),
        ],
        out_specs=pl.BlockSpec((WINDOW, D_MODEL), lambda b, n: (row_of(b, n), 0)),
        out_shape=jax.ShapeDtypeStruct((ROWS_P, D_MODEL), bf16),
        compiler_params=_params(2),
        name="mix_prompt",
    )(sinks, proj, proj, proj, proj, proj, kv, kv, gain, bias, w_s, b_s)


def _mix_sample(proj, kv, cache_k, cache_v, layer, sinks, gain, bias, w_s, b_s):
    t = SEQ_GROUP * DEC_SEQ
    groups = DEC_BATCH // SEQ_GROUP
    full = lambda shape: pl.BlockSpec(shape, lambda g: (0,) * len(shape))
    cache_spec = pl.BlockSpec((SEQ_GROUP, WINDOW, D_KV), lambda g: (layer * groups + g, 0, 0))
    return pl.pallas_call(
        _mix_sample_kernel,
        grid=(groups,),
        in_specs=[pl.BlockSpec(memory_space=pltpu.SMEM)] + _proj_specs(t, lambda g: g) + [
            pl.BlockSpec((t, 2 * D_KV), lambda g: (g, 0)),
            cache_spec, cache_spec,
            full((1, D_SGU)), full((1, D_SGU)),
            full((N_SGU_GROUPS, t, t)), full((t, N_SGU_GROUPS)),
        ],
        out_specs=[pl.BlockSpec((t, D_MODEL), lambda g: (g, 0)),
                   pl.BlockSpec((t, D_SGU), lambda g: (g, 0))],
        out_shape=[jax.ShapeDtypeStruct((ROWS_S, D_MODEL), bf16),
                   jax.ShapeDtypeStruct((ROWS_S, D_SGU), f32)],
        compiler_params=_params(1),
        name="mix_sample",
    )(sinks, proj, proj, proj, proj, proj, kv, cache_k, cache_v, gain, bias, w_s, b_s)


def _merge_kernel(abp_ref, abs_ref, gap_ref, gbp_ref, gas_ref, gbs_ref, wa_ref, wb_ref,
                  op_ref, os_ref, wa_bf, wb_bf):
    def merged(ab_ref, ga_ref, gb_ref):
        ya = jnp.dot(ab_ref[:, 0:D_ATTN], wa_bf[...], preferred_element_type=f32)
        yb = jnp.dot(ab_ref[:, D_ATTN:], wb_bf[...], preferred_element_type=f32)
        return (_sigmoid(ga_ref[...].astype(f32)) * ya + _sigmoid(gb_ref[...].astype(f32)) * yb).astype(bf16)

    @pl.when(pl.program_id(1) == 0)
    def _():
        wa_bf[...] = wa_ref[...].astype(bf16)
        wb_bf[...] = wb_ref[...].astype(bf16)
        os_ref[...] = merged(abs_ref, gas_ref, gbs_ref)

    op_ref[...] = merged(abp_ref, gap_ref, gbp_ref)


def _merge(ab_p, ab_s, proj_p, proj_s, w_pa, w_pb, layer):
    n_tiles = D_MODEL // TN
    ga_tile = GA_COL // TN
    gb_tile = GB_COL // TN
    return pl.pallas_call(
        _merge_kernel,
        grid=(n_tiles, ROWS_P // TM),
        in_specs=[
            pl.BlockSpec((TM, D_MODEL), lambda j, i: (i, 0)),
            pl.BlockSpec((ROWS_S, D_MODEL), lambda j, i: (0, 0)),
            pl.BlockSpec((TM, TN), lambda j, i: (i, ga_tile + j)),
            pl.BlockSpec((TM, TN), lambda j, i: (i, gb_tile + j)),
            pl.BlockSpec((ROWS_S, TN), lambda j, i: (0, ga_tile + j)),
            pl.BlockSpec((ROWS_S, TN), lambda j, i: (0, gb_tile + j)),
            pl.BlockSpec((None, D_ATTN, TN), lambda j, i: (layer, 0, j)),
            pl.BlockSpec((None, D_SGU, TN), lambda j, i: (layer, 0, j)),
        ],
        out_specs=[pl.BlockSpec((TM, TN), lambda j, i: (i, j)),
                   pl.BlockSpec((ROWS_S, TN), lambda j, i: (0, j))],
        out_shape=[jax.ShapeDtypeStruct((ROWS_P, D_MODEL), bf16), jax.ShapeDtypeStruct((ROWS_S, D_MODEL), bf16)],
        scratch_shapes=[pltpu.VMEM((D_ATTN, TN), bf16), pltpu.VMEM((D_SGU, TN), bf16)],
        compiler_params=_params(2),
        name="merge",
    )(ab_p, ab_s, proj_p, proj_p, proj_s, proj_s, w_pa, w_pb)


def _post_norm(x, y, gate, g_ref, b_ref):
    t = ALPHA * x + gate * y.astype(f32)
    mu = jnp.mean(t, axis=1, keepdims=True)
    d = t - mu
    var = jnp.mean(d * d, axis=1, keepdims=True)
    return d * lax.rsqrt(var + LN_EPS) * g_ref[...] + b_ref[...]


def _ln_first_kernel(xp_ref, xs_ref, yp_ref, ys_ref, gate_ref, nxt_ref, g_ref, b_ref,
                     xp_out, xs_out, hp_out, hs_out):
    i = pl.program_id(0)

    @pl.when(i < LN_BLOCKS)
    def _():
        gate = _prompt_mod_row(gate_ref, 0, i, SEQ // LN_TM)
        xn = _post_norm(xp_ref[...], yp_ref[...], gate, g_ref, b_ref)
        xp_out[...] = xn
        shift = _prompt_mod_row(nxt_ref, 0, i, SEQ // LN_TM)
        scale = _prompt_mod_row(nxt_ref, 1, i, SEQ // LN_TM)
        hp_out[...] = (xn * (1.0 + scale) + shift).astype(bf16)

    @pl.when(i == LN_BLOCKS)
    def _():
        xn = _post_norm(xs_ref[...], ys_ref[...], gate_ref[0, 0:ROWS_S, :], g_ref, b_ref)
        xs_out[...] = xn
        hs_out[...] = (xn * (1.0 + nxt_ref[1, 0:ROWS_S, :]) + nxt_ref[0, 0:ROWS_S, :]).astype(bf16)


def _ln_last_kernel(xp_ref, xs_ref, yp_ref, ys_ref, gate_ref, g_ref, b_ref, xp_out, xs_out):
    i = pl.program_id(0)

    @pl.when(i < LN_BLOCKS)
    def _():
        gate = _prompt_mod_row(gate_ref, 0, i, SEQ // LN_TM)
        xp_out[...] = _post_norm(xp_ref[...], yp_ref[...], gate, g_ref, b_ref)

    @pl.when(i == LN_BLOCKS)
    def _():
        xs_out[...] = _post_norm(xs_ref[...], ys_ref[...], gate_ref[0, 0:ROWS_S, :], g_ref, b_ref)


def _ln(xp, xs, yp, ys, mod, ln_g, ln_b, layer):
    last = layer == DEPTH - 1
    gate_spec = pl.BlockSpec((None, 1, MOD_ROWS, D_MODEL), lambda i: (layer, 2, 0, 0))
    in_specs = [_LN_ROW_SPEC, _LN_SAMPLE_SPEC, _LN_ROW_SPEC, _LN_SAMPLE_SPEC, gate_spec]
    args = [xp, xs, yp, ys, mod]
    out_specs = [_LN_ROW_SPEC, _LN_SAMPLE_SPEC]
    out_shape = [jax.ShapeDtypeStruct((ROWS_P, D_MODEL), f32), jax.ShapeDtypeStruct((ROWS_S, D_MODEL), f32)]
    if not last:
        in_specs.append(pl.BlockSpec((None, 2, MOD_ROWS, D_MODEL), lambda i: (layer + 1, 0, 0, 0)))
        args.append(mod)
        out_specs += [_LN_ROW_SPEC, _LN_SAMPLE_SPEC]
        out_shape += [jax.ShapeDtypeStruct((ROWS_P, D_MODEL), bf16), jax.ShapeDtypeStruct((ROWS_S, D_MODEL), bf16)]
    return pl.pallas_call(
        _ln_last_kernel if last else _ln_first_kernel,
        grid=(LN_BLOCKS + 1,),
        in_specs=in_specs + [_LN_VEC_SPEC, _LN_VEC_SPEC],
        out_specs=out_specs,
        out_shape=out_shape,
        compiler_params=_params(1),
        name="ln_last" if last else "ln_first",
    )(*args, ln_g, ln_b)


def kernel(x_prompt, x_sample, cache_k, cache_v, c_prompt, c_sample, w_ada, b_ada, w_in, attn_sinks,
           sgu_ln_gain, sgu_ln_bias, sgu_w_s, sgu_b_s, w_pa, w_pb, w_o, ln_gain, ln_bias):
    xp = x_prompt.reshape(ROWS_P, D_MODEL)
    xs = x_sample.reshape(ROWS_S, D_MODEL)
    c_all = jnp.concatenate([jnp.repeat(c_sample, DEC_SEQ, axis=0), c_prompt,
                             jnp.zeros((MOD_ROWS - ROWS_S - BATCH, D_MODEL), f32)], axis=0)
    mod = _ada(c_all, w_ada, b_ada)
    ck = cache_k.reshape(DEPTH * DEC_BATCH, WINDOW, D_KV)
    cv = cache_v.reshape(DEPTH * DEC_BATCH, WINDOW, D_KV)

    win_k, win_v, new_k, new_v, sgu_v = [], [], [], [], []
    hp, hs, kv_p, kv_s = _modulate_kv(xp, xs, mod, w_in)
    for l in range(DEPTH):
        proj_p, proj_s = _matmul(hp, hs, w_in, l, PROJ_COLS // TN, [_skip_kv_tile], bf16, "proj")
        if l > 0:
            kv_p, kv_s = _matmul(hp, hs, w_in, l, 1, [lambda j: j + KV_TILE], f32, "kv")
        gain = sgu_ln_gain[l].reshape(1, D_SGU)
        bias = sgu_ln_bias[l].reshape(1, D_SGU)
        ab_p = _mix_prompt(proj_p, kv_p, attn_sinks[l], gain, bias, sgu_w_s[l], sgu_b_s[l].T)
        w_small = jnp.tile(sgu_w_s[l][:, :DEC_SEQ, :DEC_SEQ], (1, SEQ_GROUP, SEQ_GROUP))
        b_small = jnp.tile(sgu_b_s[l][:, :DEC_SEQ].T, (SEQ_GROUP, 1))
        ab_s, vn_s = _mix_sample(proj_s, kv_s, ck, cv, l, attn_sinks[l], gain, bias, w_small, b_small)
        m_p, m_s = _merge(ab_p, ab_s, proj_p, proj_s, w_pa, w_pb, l)
        y_p, y_s = _matmul(m_p, m_s, w_o, l, D_MODEL // TN, [lambda j: j], bf16, "out")
        g, b = ln_gain[l].reshape(1, D_MODEL), ln_bias[l].reshape(1, D_MODEL)
        if l < DEPTH - 1:
            xp, xs, hp, hs = _ln(xp, xs, y_p, y_s, mod, g, b, l)
        else:
            xp, xs = _ln(xp, xs, y_p, y_s, mod, g, b, l)

        kv_win = kv_p.reshape(BATCH, SEQ, 2 * D_KV)[:, SEQ - WINDOW:]
        kv_win = kv_win.reshape(BATCH, WINDOW, 2, N_KV_HEADS, HEAD_DIM)
        kv_new = kv_s.reshape(DEC_BATCH, DEC_SEQ, 2, N_KV_HEADS, HEAD_DIM)
        win_k.append(kv_win[:, :, 0])
        win_v.append(kv_win[:, :, 1])
        new_k.append(kv_new[:, :, 0])
        new_v.append(kv_new[:, :, 1])
        sgu_v.append(vn_s.reshape(DEC_BATCH, DEC_SEQ, D_SGU))

    return (xp.reshape(BATCH, SEQ, D_MODEL), xs.reshape(DEC_BATCH, DEC_SEQ, D_MODEL),
            jnp.stack(win_k), jnp.stack(win_v), jnp.stack(new_k), jnp.stack(new_v), jnp.stack(sgu_v))
```

```python
import functools

import jax
import jax.numpy as jnp
from jax import lax
from jax.experimental import pallas as pl
from jax.experimental.pallas import tpu as pltpu

D_MODEL = 4096
BATCH = 4
SEQ = 2048
DEPTH = 2
DEC_BATCH = 32
DEC_SEQ = 4
HEAD_DIM = 64
D_ATTN = D_MODEL // 2
N_Q_HEADS = D_ATTN // HEAD_DIM
N_KV_HEADS = N_Q_HEADS // 8
GQA_GROUP = N_Q_HEADS // N_KV_HEADS
D_KV = N_KV_HEADS * HEAD_DIM
WINDOW = 128
D_SGU = D_MODEL // 2
N_SGU_GROUPS = 8
SGU_GROUP_DIM = D_SGU // N_SGU_GROUPS
CHUNK = 128
ALPHA = (2 * DEPTH) ** 0.25
LN_EPS = 1e-5
IN_COLS = D_ATTN + 2 * D_KV + D_ATTN + 3 * D_SGU + 2 * D_MODEL

ROWS_P = BATCH * SEQ
ROWS_S = DEC_BATCH * DEC_SEQ
MOD_ROWS = ROWS_S + 8
PROJ_COLS = IN_COLS - 2 * D_KV

TN = 512
TM = 1024
KV_TILE = D_ATTN // TN
LN_TM = 256
SEQ_GROUP = 8
NEG = -1e30
LOG2E = 1.4426950408889634

VMEM_LIMIT = 56 * 1024 * 1024

bf16 = jnp.bfloat16
f32 = jnp.float32


def _params(n_axes, vmem=VMEM_LIMIT):
    return pltpu.CompilerParams(dimension_semantics=("arbitrary",) * n_axes, vmem_limit_bytes=vmem)


def _sigmoid(x):
    return 0.5 + 0.5 * jnp.tanh(0.5 * x)


def _silu(x):
    half = 0.5 * x
    return half + half * jnp.tanh(half)


def _div_pow2(x, n):
    assert n & (n - 1) == 0
    return x >> (n.bit_length() - 1)


def _mod_pow2(x, n):
    assert n & (n - 1) == 0
    return x & (n - 1)


def _ada_kernel(c_ref, w_ref, b_ref, o_ref):
    a = _silu(c_ref[...]).astype(bf16)
    o_ref[...] = jnp.dot(a, w_ref[...].astype(bf16), preferred_element_type=f32) + b_ref[...]


def _ada(c_all, w_ada, b_ada):
    tiles_per_part = D_MODEL // TN
    return pl.pallas_call(
        _ada_kernel,
        grid=(DEPTH, 3 * tiles_per_part),
        in_specs=[
            pl.BlockSpec((MOD_ROWS, D_MODEL), lambda l, j: (0, 0)),
            pl.BlockSpec((None, D_MODEL, TN), lambda l, j: (l, 0, j)),
            pl.BlockSpec((None, 1, TN), lambda l, j: (l, 0, j)),
        ],
        out_specs=pl.BlockSpec((None, None, MOD_ROWS, TN),
                               lambda l, j: (l, j // tiles_per_part, 0, j % tiles_per_part)),
        out_shape=jax.ShapeDtypeStruct((DEPTH, 3, MOD_ROWS, D_MODEL), f32),
        compiler_params=_params(2),
        name="ada",
    )(c_all, w_ada, b_ada.reshape(DEPTH, 1, 3 * D_MODEL))


def _prompt_mod_row(ref, part, blk, blocks_per_batch):
    return ref[part, pl.ds(ROWS_S + blk // blocks_per_batch, 1), :]


LN_BLOCKS = ROWS_P // LN_TM


def _ln_row(i):
    return (jnp.minimum(i, LN_BLOCKS - 1), 0)


_LN_ROW_SPEC = pl.BlockSpec((LN_TM, D_MODEL), _ln_row)
_LN_SAMPLE_SPEC = pl.BlockSpec((ROWS_S, D_MODEL), lambda i: (0, 0))
_LN_VEC_SPEC = pl.BlockSpec((1, D_MODEL), lambda i: (0, 0))


def _modulate_kv_kernel(xp_ref, xs_ref, mod_ref, w_ref, hp_ref, hs_ref, kvp_ref, kvs_ref, w_bf):
    i = pl.program_id(0)

    @pl.when(i == 0)
    def _():
        w_bf[...] = w_ref[...].astype(bf16)

    @pl.when(i < LN_BLOCKS)
    def _():
        shift = _prompt_mod_row(mod_ref, 0, i, SEQ // LN_TM)
        scale = _prompt_mod_row(mod_ref, 1, i, SEQ // LN_TM)
        h = (xp_ref[...] * (1.0 + scale) + shift).astype(bf16)
        hp_ref[...] = h
        kvp_ref[...] = jnp.dot(h, w_bf[...], preferred_element_type=f32)

    @pl.when(i == LN_BLOCKS)
    def _():
        h = (xs_ref[...] * (1.0 + mod_ref[1, 0:ROWS_S, :]) + mod_ref[0, 0:ROWS_S, :]).astype(bf16)
        hs_ref[...] = h
        kvs_ref[...] = jnp.dot(h, w_bf[...], preferred_element_type=f32)


def _modulate_kv(xp, xs, mod, w_in):
    kv_row_spec = pl.BlockSpec((LN_TM, 2 * D_KV), _ln_row)
    kv_sample_spec = pl.BlockSpec((ROWS_S, 2 * D_KV), lambda i: (0, 0))
    return pl.pallas_call(
        _modulate_kv_kernel,
        grid=(LN_BLOCKS + 1,),
        in_specs=[_LN_ROW_SPEC, _LN_SAMPLE_SPEC,
                  pl.BlockSpec((None, 2, MOD_ROWS, D_MODEL), lambda i: (0, 0, 0, 0)),
                  pl.BlockSpec((None, D_MODEL, TN), lambda i: (0, 0, KV_TILE), pipeline_mode=pl.Buffered(1))],
        out_specs=[_LN_ROW_SPEC, _LN_SAMPLE_SPEC, kv_row_spec, kv_sample_spec],
        out_shape=[jax.ShapeDtypeStruct((ROWS_P, D_MODEL), bf16), jax.ShapeDtypeStruct((ROWS_S, D_MODEL), bf16),
                   jax.ShapeDtypeStruct((ROWS_P, 2 * D_KV), f32), jax.ShapeDtypeStruct((ROWS_S, 2 * D_KV), f32)],
        scratch_shapes=[pltpu.VMEM((D_MODEL, TN), bf16)],
        compiler_params=_params(1),
        name="modulate_kv",
    )(xp, xs, mod, w_in)


def _matmul_kernel(*refs, n_w, tw):
    xp_ref, xs_ref = refs[0:2]
    w_refs = refs[2:2 + n_w]
    op_ref, os_ref, w_bf = refs[2 + n_w:]

    @pl.when(pl.program_id(1) == 0)
    def _():
        for a, w_ref in enumerate(w_refs):
            w_bf[:, a * tw:(a + 1) * tw] = w_ref[...].astype(bf16)
        os_ref[...] = jnp.dot(xs_ref[...], w_bf[...], preferred_element_type=f32).astype(os_ref.dtype)

    op_ref[...] = jnp.dot(xp_ref[...], w_bf[...], preferred_element_type=f32).astype(op_ref.dtype)


def _matmul(xp, xs, w, layer, n_tiles, w_tiles_of, out_dtype, name, tm=TM, tw=TN):
    k = xp.shape[1]
    n_w = len(w_tiles_of)
    tn = n_w * tw
    out_bytes = jnp.dtype(out_dtype).itemsize
    vmem = (2 * (tm + ROWS_S) * k * 2 + 2 * k * tn * 4 + k * tn * 2 + 2 * (tm + ROWS_S) * tn * out_bytes + (6 << 20))
    w_specs = [pl.BlockSpec((None, k, tw), functools.partial(lambda f, j, i: (layer, 0, f(j)), f))
               for f in w_tiles_of]
    return pl.pallas_call(
        functools.partial(_matmul_kernel, n_w=n_w, tw=tw),
        grid=(n_tiles, ROWS_P // tm),
        in_specs=[pl.BlockSpec((tm, k), lambda j, i: (i, 0)),
                  pl.BlockSpec((ROWS_S, k), lambda j, i: (0, 0))] + w_specs,
        out_specs=[pl.BlockSpec((tm, tn), lambda j, i: (i, j)),
                   pl.BlockSpec((ROWS_S, tn), lambda j, i: (0, j))],
        out_shape=[jax.ShapeDtypeStruct((ROWS_P, n_tiles * tn), out_dtype),
                   jax.ShapeDtypeStruct((ROWS_S, n_tiles * tn), out_dtype)],
        scratch_shapes=[pltpu.VMEM((k, tn), bf16)],
        compiler_params=_params(2, vmem),
        name=name,
    )(xp, xs, *([w] * n_w))


def _skip_kv_tile(c):
    return c + (c >= KV_TILE).astype(jnp.int32)


GA_COL = 5 * D_ATTN
GB_COL = GA_COL + D_MODEL


def _attention_stages(q_ref, k_all, v_all, mask, slot, slot_cols, live_key, sink_ref, za_ref, ab_ref, t, lk):
    lane = lax.broadcasted_iota(jnp.int32, (lk, 2 * HEAD_DIM), 1)
    low, high = lane < HEAD_DIM, lane >= HEAD_DIM
    ones2 = jnp.concatenate([low, high], axis=0).astype(f32).astype(bf16)
    nt = (((1,), (1,)), ((), ()))
    pairs = GQA_GROUP // 2
    k_scale = HEAD_DIM ** -0.5 * LOG2E

    def head(kv):
        cols = slice((kv // 2) * 2 * HEAD_DIM, (kv // 2 + 1) * 2 * HEAD_DIM)
        kp, vp = k_all[:, cols] * k_scale, v_all[:, cols]
        keep = (low if kv % 2 == 0 else high) & live_key
        k_own = jnp.where(keep, kp, 0.0)
        v_own = jnp.where(keep, vp, 0.0)
        k_oth = pltpu.roll(k_own, HEAD_DIM, 1)
        v_oth = pltpu.roll(v_own, HEAD_DIM, 1)
        if kv % 2 == 0:
            k2 = jnp.concatenate([k_own, k_oth], axis=0).astype(bf16)
            v2 = jnp.concatenate([v_own, v_oth], axis=0).astype(bf16)
        else:
            k2 = jnp.concatenate([k_oth, k_own], axis=0).astype(bf16)
            v2 = jnp.concatenate([v_oth, v_own], axis=0).astype(bf16)
        v2 = jnp.concatenate([v2, ones2], axis=1)
        base = kv * GQA_GROUP * HEAD_DIM
        q4 = jnp.concatenate(
            [q_ref[:, base + p * 128: base + (p + 1) * 128] for p in range(pairs)], axis=0)
        s_all = lax.dot_general(q4, k2, nt, preferred_element_type=f32)
        rows = []
        for p in range(pairs):
            halves = []
            for h in range(2):
                sink = sink_ref[kv * GQA_GROUP + 2 * p + h] * LOG2E
                fill = jnp.where(slot, sink, NEG)
                if slot_cols < lk:
                    fill = jnp.concatenate([fill, jnp.full((t, lk - slot_cols), NEG, f32)], axis=1)
                s = jnp.where(mask, s_all[p * t:(p + 1) * t, h * lk:(h + 1) * lk], fill)
                m = jnp.max(s, axis=1, keepdims=True)
                halves.append(jnp.exp2(s - m).astype(bf16))
            rows.append(jnp.concatenate(halves, axis=1))
        probs = jnp.concatenate(rows, axis=0)
        o = jnp.dot(probs, v2, preferred_element_type=f32)
        o = o[:, 0:128] / o[:, 128:256]
        for p in range(pairs):
            cols = slice(base + p * 128, base + (p + 1) * 128)
            ab_ref[:, cols] = (o[p * t:(p + 1) * t, :] * _silu(za_ref[:, cols].astype(f32))).astype(bf16)

    return [functools.partial(head, kv) for kv in range(N_KV_HEADS)]


def _sgu_norm(vb_ref, gain_ref, bias_ref):
    vb = vb_ref[...].astype(f32)
    mu = jnp.mean(vb, axis=1, keepdims=True)
    d = vb - mu
    var = jnp.mean(d * d, axis=1, keepdims=True)
    return d * lax.rsqrt(var + LN_EPS) * gain_ref[...] + bias_ref[...]


def _sgu_stages(u_ref, zb_ref, ws_ref, bs_ref, vn, w_mask, ab_ref):
    vn_bf = vn.astype(bf16)

    def group(g):
        cols = slice(g * SGU_GROUP_DIM, (g + 1) * SGU_GROUP_DIM)
        w_g = jnp.where(w_mask, ws_ref[g], 0.0).astype(bf16)
        s = jnp.dot(w_g, vn_bf[:, cols], preferred_element_type=f32) + bs_ref[:, g:g + 1]
        out_b = u_ref[:, cols].astype(f32) * s * _silu(zb_ref[:, cols].astype(f32))
        ab_ref[:, D_ATTN + g * SGU_GROUP_DIM: D_ATTN + (g + 1) * SGU_GROUP_DIM] = out_b.astype(bf16)

    return [functools.partial(group, g) for g in range(N_SGU_GROUPS)]


def _interleave(attention, sgu, others=()):
    per_head = len(sgu) // len(attention)
    for kv, head in enumerate(attention):
        if kv < len(others):
            others[kv]()
        head()
        for stage in sgu[kv * per_head:(kv + 1) * per_head]:
            stage()


def _mix_prompt_block(n, sink_ref, q_ref, za_ref, u_ref, vb_ref, zb_ref, kvc_ref, kvp_ref,
                      gain_ref, bias_ref, ws_ref, bs_ref, ab_ref, others=()):
    vn = _sgu_norm(vb_ref, gain_ref, bias_ref)
    kvc, kvp = kvc_ref[...], kvp_ref[...]
    k_all = jnp.concatenate([kvp[:, 0:D_KV], kvc[:, 0:D_KV]], axis=0)
    v_all = jnp.concatenate([kvp[:, D_KV:], kvc[:, D_KV:]], axis=0)
    qi = lax.broadcasted_iota(jnp.int32, (WINDOW, 2 * WINDOW), 0)
    kj = lax.broadcasted_iota(jnp.int32, (WINDOW, 2 * WINDOW), 1)
    no_prev = jnp.where(n > 0, 0, 2 * WINDOW)
    mask = ((kj < WINDOW) & (kj > qi + no_prev)) | ((kj >= WINDOW) & (kj - WINDOW <= qi))
    slot = lax.broadcasted_iota(jnp.int32, (WINDOW, WINDOW), 1) == 0
    live_key = lax.broadcasted_iota(jnp.int32, (2 * WINDOW, 2 * HEAD_DIM), 0) != 0
    heads = _attention_stages(q_ref, k_all, v_all, mask, slot, WINDOW, live_key, sink_ref, za_ref, ab_ref,
                              WINDOW, 2 * WINDOW)
    r = lax.broadcasted_iota(jnp.int32, (CHUNK, CHUNK), 0)
    c = lax.broadcasted_iota(jnp.int32, (CHUNK, CHUNK), 1)
    _interleave(heads, _sgu_stages(u_ref, zb_ref, ws_ref, bs_ref, vn, c <= r, ab_ref), others)


ROW_TILES = ROWS_P // TM
TILE_BLOCKS = TM // WINDOW
assert TILE_BLOCKS == D_MODEL // TN


def _mix_merge_kernel(sink_ref, q_ref, za_ref, u_ref, vb_ref, zb_ref, kvc_ref, kvp_ref,
                      gain_ref, bias_ref, ws_ref, bs_ref,
                      abs_ref, gap_ref, gbp_ref, gas_ref, gbs_ref, wa_ref, wb_ref,
                      op_ref, os_ref, ab_even, ab_odd, wa_bf, wb_bf):
    r, c = pl.program_id(0), pl.program_id(1)
    blk = jnp.minimum(r, ROW_TILES - 1) * TILE_BLOCKS + c

    def mix(ab_tile, others=()):
        mix_out = ab_tile.at[pl.ds(pl.multiple_of(c * WINDOW, WINDOW), WINDOW), :]
        _mix_prompt_block(_mod_pow2(blk, SEQ // WINDOW), sink_ref, q_ref, za_ref, u_ref, vb_ref, zb_ref,
                          kvc_ref, kvp_ref, gain_ref, bias_ref, ws_ref, bs_ref, mix_out, others)

    def merged(ab_ref, ga_ref, gb_ref):
        ya = jnp.dot(ab_ref[:, 0:D_ATTN], wa_bf[...], preferred_element_type=f32)
        yb = jnp.dot(ab_ref[:, D_ATTN:], wb_bf[...], preferred_element_type=f32)
        return (_sigmoid(ga_ref[...].astype(f32)) * ya + _sigmoid(gb_ref[...].astype(f32)) * yb).astype(bf16)

    def mix_and_project(mix_tile, done_tile):
        wa_bf[...] = wa_ref[...].astype(bf16)
        wb_bf[...] = wb_ref[...].astype(bf16)
        half = TN // 2
        parts = {}

        def piece(lhs_cols, w_bf, key, cols):
            parts[key] = jnp.dot(done_tile[:, lhs_cols], w_bf[:, cols], preferred_element_type=f32)

        pieces = [functools.partial(piece, lhs_cols, w_bf, (name, hc), slice(hc * half, (hc + 1) * half))
                  for name, lhs_cols, w_bf in (("a", slice(0, D_ATTN), wa_bf), ("b", slice(D_ATTN, D_MODEL), wb_bf))
                  for hc in range(2)]
        mix(mix_tile, pieces)
        for hc in range(2):
            cols = slice(hc * half, (hc + 1) * half)
            op_ref[:, cols] = (_sigmoid(gap_ref[:, cols].astype(f32)) * parts["a", hc]
                               + _sigmoid(gbp_ref[:, cols].astype(f32)) * parts["b", hc]).astype(bf16)

    @pl.when(r == 0)
    def _():
        mix(ab_even)

    @pl.when((r > 0) & (r % 2 == 1))
    def _():
        mix_and_project(ab_odd, ab_even)

    @pl.when((r > 0) & (r % 2 == 0))
    def _():
        mix_and_project(ab_even, ab_odd)

    @pl.when(r == ROW_TILES)
    def _():
        os_ref[...] = merged(abs_ref, gas_ref, gbs_ref)


def _mix_merge(proj_p, proj_s, kv_p, ab_s, sinks, gain, bias, w_s, b_s, w_pa, w_pb, layer):
    seq_blocks = SEQ // WINDOW
    ga_tile, gb_tile = GA_COL // TN, GB_COL // TN
    blk_of = lambda r, c: jnp.minimum(r, ROW_TILES - 1) * TILE_BLOCKS + c
    prev_of = lambda r, c: blk_of(r, c) - (_mod_pow2(blk_of(r, c), seq_blocks) > 0).astype(jnp.int32)
    tile_of = lambda r: jnp.maximum(r - 1, 0)
    col_of = lambda r, c: jnp.where(r == 0, 0, c)
    scol_of = lambda r, c: jnp.where(r == ROW_TILES, c, 0)
    full = lambda shape: pl.BlockSpec(shape, lambda r, c: (0,) * len(shape))
    return pl.pallas_call(
        _mix_merge_kernel,
        grid=(ROW_TILES + 1, TILE_BLOCKS),
        in_specs=[pl.BlockSpec(memory_space=pltpu.SMEM)] + _proj_specs(WINDOW, blk_of) + [
            pl.BlockSpec((WINDOW, 2 * D_KV), lambda r, c: (blk_of(r, c), 0)),
            pl.BlockSpec((WINDOW, 2 * D_KV), lambda r, c: (prev_of(r, c), 0)),
            full((1, D_SGU)), full((1, D_SGU)),
            full((N_SGU_GROUPS, CHUNK, CHUNK)), full((CHUNK, N_SGU_GROUPS)),
            full((ROWS_S, D_MODEL)),
            pl.BlockSpec((TM, TN), lambda r, c: (tile_of(r), ga_tile + col_of(r, c))),
            pl.BlockSpec((TM, TN), lambda r, c: (tile_of(r), gb_tile + col_of(r, c))),
            pl.BlockSpec((ROWS_S, TN), lambda r, c: (0, ga_tile + scol_of(r, c))),
            pl.BlockSpec((ROWS_S, TN), lambda r, c: (0, gb_tile + scol_of(r, c))),
            pl.BlockSpec((None, D_ATTN, TN), lambda r, c: (layer, 0, col_of(r, c))),
            pl.BlockSpec((None, D_SGU, TN), lambda r, c: (layer, 0, col_of(r, c))),
        ],
        out_specs=[pl.BlockSpec((TM, TN), lambda r, c: (tile_of(r), col_of(r, c))),
                   pl.BlockSpec((ROWS_S, TN), lambda r, c: (0, scol_of(r, c)))],
        out_shape=[jax.ShapeDtypeStruct((ROWS_P, D_MODEL), bf16), jax.ShapeDtypeStruct((ROWS_S, D_MODEL), bf16)],
        scratch_shapes=[pltpu.VMEM((TM, D_MODEL), bf16), pltpu.VMEM((TM, D_MODEL), bf16),
                        pltpu.VMEM((D_ATTN, TN), bf16), pltpu.VMEM((D_SGU, TN), bf16)],
        compiler_params=_params(2),
        name="mix_merge",
    )(sinks, proj_p, proj_p, proj_p, proj_p, proj_p, kv_p, kv_p, gain, bias, w_s, b_s,
      ab_s, proj_p, proj_p, proj_s, proj_s, w_pa, w_pb)


def _mix_sample_kernel(sink_ref, q_ref, za_ref, u_ref, vb_ref, zb_ref, kvn_ref, ck_ref, cv_ref,
                       gain_ref, bias_ref, ws_ref, bs_ref, ab_ref, vn_ref):
    t = SEQ_GROUP * DEC_SEQ
    n_cache = SEQ_GROUP * WINDOW
    lk = n_cache + WINDOW
    kvn = kvn_ref[...]
    pad = jnp.zeros((WINDOW - t, D_KV), f32)
    k_all = jnp.concatenate([ck_ref[...].reshape(n_cache, D_KV), kvn[:, 0:D_KV], pad], axis=0)
    v_all = jnp.concatenate([cv_ref[...].reshape(n_cache, D_KV), kvn[:, D_KV:], pad], axis=0)
    qi = lax.broadcasted_iota(jnp.int32, (t, lk), 0)
    kj = lax.broadcasted_iota(jnp.int32, (t, lk), 1)
    q_seq, q_tok = _div_pow2(qi, DEC_SEQ), _mod_pow2(qi, DEC_SEQ)
    in_cache = (_div_pow2(kj, WINDOW) == q_seq) & (_mod_pow2(kj, WINDOW) > q_tok)
    kn = jnp.maximum(kj - n_cache, 0)
    in_new = (kn < t) & (_div_pow2(kn, DEC_SEQ) == q_seq) & (_mod_pow2(kn, DEC_SEQ) <= q_tok)
    mask = ((kj < n_cache) & in_cache) | ((kj >= n_cache) & in_new)
    slot_seq = _div_pow2(lax.broadcasted_iota(jnp.int32, (t, n_cache), 0), DEC_SEQ)
    slot = lax.broadcasted_iota(jnp.int32, (t, n_cache), 1) == slot_seq * WINDOW
    key_row = lax.broadcasted_iota(jnp.int32, (lk, 2 * HEAD_DIM), 0)
    live_key = (key_row >= n_cache) | (_mod_pow2(key_row, WINDOW) != 0)
    heads = _attention_stages(q_ref, k_all, v_all, mask, slot, n_cache, live_key, sink_ref, za_ref, ab_ref, t, lk)

    vn = _sgu_norm(vb_ref, gain_ref, bias_ref)
    vn_ref[...] = vn
    r = lax.broadcasted_iota(jnp.int32, (t, t), 0)
    c = lax.broadcasted_iota(jnp.int32, (t, t), 1)
    w_mask = (_div_pow2(r, DEC_SEQ) == _div_pow2(c, DEC_SEQ)) & (c <= r)
    _interleave(heads, _sgu_stages(u_ref, zb_ref, ws_ref, bs_ref, vn, w_mask, ab_ref))


def _proj_specs(rows, row_of):
    return [pl.BlockSpec((rows, D_ATTN), functools.partial(lambda s, *g: (row_of(*g), s), s))
            for s in range(5)]


def _mix_prompt(proj, kv, sinks, gain, bias, w_s, b_s):
    blocks = SEQ // WINDOW
    row_of = lambda b, n: b * blocks + n
    full = lambda shape: pl.BlockSpec(shape, lambda b, n: (0,) * len(shape))
    return pl.pallas_call(
        _mix_prompt_kernel,
        grid=(BATCH, blocks),
        in_specs=[pl.BlockSpec(memory_space=pltpu.SMEM)] + _proj_specs(WINDOW, row_of) + [
            pl.BlockSpec((WINDOW, 2 * D_KV), lambda b, n: (row_of(b, n), 0)),
            pl.BlockSpec((WINDOW, 2 * D_KV), lambda b, n: (row_of(b, jnp.maximum(n - 1, 0)), 0)),
            full((1, D_SGU)), full((1, D_SGU)),
            full((N_SGU_GROUPS, CHUNK, CHUNK)), full((CHUNK, N_SGU_GROUPS)),
        ],
        out_specs=pl.BlockSpec((WINDOW, D_MODEL), lambda b, n: (row_of(b, n), 0)),
        out_shape=jax.ShapeDtypeStruct((ROWS_P, D_MODEL), bf16),
        compiler_params=_params(2),
        name="mix_prompt",
    )(sinks, proj, proj, proj, proj, proj, kv, kv, gain, bias, w_s, b_s)


def _mix_sample(proj, kv, cache_k, cache_v, layer, sinks, gain, bias, w_s, b_s):
    t = SEQ_GROUP * DEC_SEQ
    groups = DEC_BATCH // SEQ_GROUP
    full = lambda shape: pl.BlockSpec(shape, lambda g: (0,) * len(shape))
    cache_spec = pl.BlockSpec((SEQ_GROUP, WINDOW, D_KV), lambda g: (layer * groups + g, 0, 0))
    return pl.pallas_call(
        _mix_sample_kernel,
        grid=(groups,),
        in_specs=[pl.BlockSpec(memory_space=pltpu.SMEM)] + _proj_specs(t, lambda g: g) + [
            pl.BlockSpec((t, 2 * D_KV), lambda g: (g, 0)),
            cache_spec, cache_spec,
            full((1, D_SGU)), full((1, D_SGU)),
            full((N_SGU_GROUPS, t, t)), full((t, N_SGU_GROUPS)),
        ],
        out_specs=[pl.BlockSpec((t, D_MODEL), lambda g: (g, 0)),
                   pl.BlockSpec((t, D_SGU), lambda g: (g, 0))],
        out_shape=[jax.ShapeDtypeStruct((ROWS_S, D_MODEL), bf16),
                   jax.ShapeDtypeStruct((ROWS_S, D_SGU), f32)],
        compiler_params=_params(1),
        name="mix_sample",
    )(sinks, proj, proj, proj, proj, proj, kv, cache_k, cache_v, gain, bias, w_s, b_s)


def _merge_kernel(abp_ref, abs_ref, gap_ref, gbp_ref, gas_ref, gbs_ref, wa_ref, wb_ref,
                  op_ref, os_ref, wa_bf, wb_bf):
    def merged(ab_ref, ga_ref, gb_ref):
        ya = jnp.dot(ab_ref[:, 0:D_ATTN], wa_bf[...], preferred_element_type=f32)
        yb = jnp.dot(ab_ref[:, D_ATTN:], wb_bf[...], preferred_element_type=f32)
        return (_sigmoid(ga_ref[...].astype(f32)) * ya + _sigmoid(gb_ref[...].astype(f32)) * yb).astype(bf16)

    @pl.when(pl.program_id(1) == 0)
    def _():
        wa_bf[...] = wa_ref[...].astype(bf16)
        wb_bf[...] = wb_ref[...].astype(bf16)
        os_ref[...] = merged(abs_ref, gas_ref, gbs_ref)

    op_ref[...] = merged(abp_ref, gap_ref, gbp_ref)


def _merge(ab_p, ab_s, proj_p, proj_s, w_pa, w_pb, layer):
    n_tiles = D_MODEL // TN
    ga_tile = GA_COL // TN
    gb_tile = GB_COL // TN
    return pl.pallas_call(
        _merge_kernel,
        grid=(n_tiles, ROWS_P // TM),
        in_specs=[
            pl.BlockSpec((TM, D_MODEL), lambda j, i: (i, 0)),
            pl.BlockSpec((ROWS_S, D_MODEL), lambda j, i: (0, 0)),
            pl.BlockSpec((TM, TN), lambda j, i: (i, ga_tile + j)),
            pl.BlockSpec((TM, TN), lambda j, i: (i, gb_tile + j)),
            pl.BlockSpec((ROWS_S, TN), lambda j, i: (0, ga_tile + j)),
            pl.BlockSpec((ROWS_S, TN), lambda j, i: (0, gb_tile + j)),
            pl.BlockSpec((None, D_ATTN, TN), lambda j, i: (layer, 0, j)),
            pl.BlockSpec((None, D_SGU, TN), lambda j, i: (layer, 0, j)),
        ],
        out_specs=[pl.BlockSpec((TM, TN), lambda j, i: (i, j)),
                   pl.BlockSpec((ROWS_S, TN), lambda j, i: (0, j))],
        out_shape=[jax.ShapeDtypeStruct((ROWS_P, D_MODEL), bf16), jax.ShapeDtypeStruct((ROWS_S, D_MODEL), bf16)],
        scratch_shapes=[pltpu.VMEM((D_ATTN, TN), bf16), pltpu.VMEM((D_SGU, TN), bf16)],
        compiler_params=_params(2),
        name="merge",
    )(ab_p, ab_s, proj_p, proj_p, proj_s, proj_s, w_pa, w_pb)


def _post_norm(x, y, gate, g_ref, b_ref):
    t = ALPHA * x + gate * y.astype(f32)
    mu = jnp.mean(t, axis=1, keepdims=True)
    d = t - mu
    var = jnp.mean(d * d, axis=1, keepdims=True)
    return d * lax.rsqrt(var + LN_EPS) * g_ref[...] + b_ref[...]


def _ln_first_kernel(xp_ref, xs_ref, yp_ref, ys_ref, gate_ref, nxt_ref, g_ref, b_ref,
                     xp_out, xs_out, hp_out, hs_out):
    i = pl.program_id(0)

    @pl.when(i < LN_BLOCKS)
    def _():
        gate = _prompt_mod_row(gate_ref, 0, i, SEQ // LN_TM)
        xn = _post_norm(xp_ref[...], yp_ref[...], gate, g_ref, b_ref)
        xp_out[...] = xn
        shift = _prompt_mod_row(nxt_ref, 0, i, SEQ // LN_TM)
        scale = _prompt_mod_row(nxt_ref, 1, i, SEQ // LN_TM)
        hp_out[...] = (xn * (1.0 + scale) + shift).astype(bf16)

    @pl.when(i == LN_BLOCKS)
    def _():
        xn = _post_norm(xs_ref[...], ys_ref[...], gate_ref[0, 0:ROWS_S, :], g_ref, b_ref)
        xs_out[...] = xn
        hs_out[...] = (xn * (1.0 + nxt_ref[1, 0:ROWS_S, :]) + nxt_ref[0, 0:ROWS_S, :]).astype(bf16)


def _ln_last_kernel(xp_ref, xs_ref, yp_ref, ys_ref, gate_ref, g_ref, b_ref, xp_out, xs_out):
    i = pl.program_id(0)

    @pl.when(i < LN_BLOCKS)
    def _():
        gate = _prompt_mod_row(gate_ref, 0, i, SEQ // LN_TM)
        xp_out[...] = _post_norm(xp_ref[...], yp_ref[...], gate, g_ref, b_ref)

    @pl.when(i == LN_BLOCKS)
    def _():
        xs_out[...] = _post_norm(xs_ref[...], ys_ref[...], gate_ref[0, 0:ROWS_S, :], g_ref, b_ref)


def _ln(xp, xs, yp, ys, mod, ln_g, ln_b, layer):
    last = layer == DEPTH - 1
    gate_spec = pl.BlockSpec((None, 1, MOD_ROWS, D_MODEL), lambda i: (layer, 2, 0, 0))
    in_specs = [_LN_ROW_SPEC, _LN_SAMPLE_SPEC, _LN_ROW_SPEC, _LN_SAMPLE_SPEC, gate_spec]
    args = [xp, xs, yp, ys, mod]
    out_specs = [_LN_ROW_SPEC, _LN_SAMPLE_SPEC]
    out_shape = [jax.ShapeDtypeStruct((ROWS_P, D_MODEL), f32), jax.ShapeDtypeStruct((ROWS_S, D_MODEL), f32)]
    if not last:
        in_specs.append(pl.BlockSpec((None, 2, MOD_ROWS, D_MODEL), lambda i: (layer + 1, 0, 0, 0)))
        args.append(mod)
        out_specs += [_LN_ROW_SPEC, _LN_SAMPLE_SPEC]
        out_shape += [jax.ShapeDtypeStruct((ROWS_P, D_MODEL), bf16), jax.ShapeDtypeStruct((ROWS_S, D_MODEL), bf16)]
    return pl.pallas_call(
        _ln_last_kernel if last else _ln_first_kernel,
        grid=(LN_BLOCKS + 1,),
        in_specs=in_specs + [_LN_VEC_SPEC, _LN_VEC_SPEC],
        out_specs=out_specs,
        out_shape=out_shape,
        compiler_params=_params(1),
        name="ln_last" if last else "ln_first",
    )(*args, ln_g, ln_b)


def kernel(x_prompt, x_sample, cache_k, cache_v, c_prompt, c_sample, w_ada, b_ada, w_in, attn_sinks,
           sgu_ln_gain, sgu_ln_bias, sgu_w_s, sgu_b_s, w_pa, w_pb, w_o, ln_gain, ln_bias):
    xp = x_prompt.reshape(ROWS_P, D_MODEL)
    xs = x_sample.reshape(ROWS_S, D_MODEL)
    c_all = jnp.concatenate([jnp.repeat(c_sample, DEC_SEQ, axis=0), c_prompt,
                             jnp.zeros((MOD_ROWS - ROWS_S - BATCH, D_MODEL), f32)], axis=0)
    mod = _ada(c_all, w_ada, b_ada)
    ck = cache_k.reshape(DEPTH * DEC_BATCH, WINDOW, D_KV)
    cv = cache_v.reshape(DEPTH * DEC_BATCH, WINDOW, D_KV)

    win_k, win_v, new_k, new_v, sgu_v = [], [], [], [], []
    hp, hs, kv_p, kv_s = _modulate_kv(xp, xs, mod, w_in)
    for l in range(DEPTH):
        proj_p, proj_s = _matmul(hp, hs, w_in, l, PROJ_COLS // TN, [_skip_kv_tile], bf16, "proj")
        if l > 0:
            kv_p, kv_s = _matmul(hp, hs, w_in, l, 1, [lambda j: j + KV_TILE], f32, "kv")
        gain = sgu_ln_gain[l].reshape(1, D_SGU)
        bias = sgu_ln_bias[l].reshape(1, D_SGU)
        w_small = jnp.tile(sgu_w_s[l][:, :DEC_SEQ, :DEC_SEQ], (1, SEQ_GROUP, SEQ_GROUP))
        b_small = jnp.tile(sgu_b_s[l][:, :DEC_SEQ].T, (SEQ_GROUP, 1))
        ab_s, vn_s = _mix_sample(proj_s, kv_s, ck, cv, l, attn_sinks[l], gain, bias, w_small, b_small)
        m_p, m_s = _mix_merge(proj_p, proj_s, kv_p, ab_s, attn_sinks[l], gain, bias, sgu_w_s[l], sgu_b_s[l].T,
                              w_pa, w_pb, l)
        y_p, y_s = _matmul(m_p, m_s, w_o, l, D_MODEL // TN, [lambda j: j], bf16, "out")
        g, b = ln_gain[l].reshape(1, D_MODEL), ln_bias[l].reshape(1, D_MODEL)
        if l < DEPTH - 1:
            xp, xs, hp, hs = _ln(xp, xs, y_p, y_s, mod, g, b, l)
        else:
            xp, xs = _ln(xp, xs, y_p, y_s, mod, g, b, l)

        kv_win = kv_p.reshape(BATCH, SEQ, 2 * D_KV)[:, SEQ - WINDOW:]
        kv_win = kv_win.reshape(BATCH, WINDOW, 2, N_KV_HEADS, HEAD_DIM)
        kv_new = kv_s.reshape(DEC_BATCH, DEC_SEQ, 2, N_KV_HEADS, HEAD_DIM)
        win_k.append(kv_win[:, :, 0])
        win_v.append(kv_win[:, :, 1])
        new_k.append(kv_new[:, :, 0])
        new_v.append(kv_new[:, :, 1])
        sgu_v.append(vn_s.reshape(DEC_BATCH, DEC_SEQ, D_SGU))

    return (xp.reshape(BATCH, SEQ, D_MODEL), xs.reshape(DEC_BATCH, DEC_SEQ, D_MODEL),
            jnp.stack(win_k), jnp.stack(win_v), jnp.stack(new_k), jnp.stack(new_v), jnp.stack(sgu_v))
```

```python
import functools

import jax
import jax.numpy as jnp
from jax import lax
from jax.experimental import pallas as pl
from jax.experimental.pallas import tpu as pltpu

D_MODEL = 4096
BATCH = 4
SEQ = 2048
DEPTH = 2
DEC_BATCH = 32
DEC_SEQ = 4
HEAD_DIM = 64
D_ATTN = D_MODEL // 2
N_Q_HEADS = D_ATTN // HEAD_DIM
N_KV_HEADS = N_Q_HEADS // 8
GQA_GROUP = N_Q_HEADS // N_KV_HEADS
D_KV = N_KV_HEADS * HEAD_DIM
WINDOW = 128
D_SGU = D_MODEL // 2
N_SGU_GROUPS = 8
SGU_GROUP_DIM = D_SGU // N_SGU_GROUPS
CHUNK = 128
ALPHA = (2 * DEPTH) ** 0.25
LN_EPS = 1e-5
IN_COLS = D_ATTN + 2 * D_KV + D_ATTN + 3 * D_SGU + 2 * D_MODEL

ROWS_P = BATCH * SEQ
ROWS_S = DEC_BATCH * DEC_SEQ
MOD_ROWS = ROWS_S + 8
PROJ_COLS = IN_COLS - 2 * D_KV

TN = 512
TM = 1024
KV_TILE = D_ATTN // TN
LN_TM = 256
SEQ_GROUP = 8
NEG = -1e30
LOG2E = 1.4426950408889634

VMEM_LIMIT = 56 * 1024 * 1024

bf16 = jnp.bfloat16
f32 = jnp.float32


def _params(n_axes, vmem=VMEM_LIMIT):
    return pltpu.CompilerParams(dimension_semantics=("arbitrary",) * n_axes, vmem_limit_bytes=vmem)


def _sigmoid(x):
    return 0.5 + 0.5 * jnp.tanh(0.5 * x)


def _silu(x):
    half = 0.5 * x
    return half + half * jnp.tanh(half)


def _div_pow2(x, n):
    assert n & (n - 1) == 0
    return x >> (n.bit_length() - 1)


def _mod_pow2(x, n):
    assert n & (n - 1) == 0
    return x & (n - 1)


def _ada_kernel(c_ref, w_ref, b_ref, o_ref):
    a = _silu(c_ref[...]).astype(bf16)
    o_ref[...] = jnp.dot(a, w_ref[...].astype(bf16), preferred_element_type=f32) + b_ref[...]


def _ada(c_all, w_ada, b_ada):
    tiles_per_part = D_MODEL // TN
    return pl.pallas_call(
        _ada_kernel,
        grid=(DEPTH, 3 * tiles_per_part),
        in_specs=[
            pl.BlockSpec((MOD_ROWS, D_MODEL), lambda l, j: (0, 0)),
            pl.BlockSpec((None, D_MODEL, TN), lambda l, j: (l, 0, j)),
            pl.BlockSpec((None, 1, TN), lambda l, j: (l, 0, j)),
        ],
        out_specs=pl.BlockSpec((None, None, MOD_ROWS, TN),
                               lambda l, j: (l, j // tiles_per_part, 0, j % tiles_per_part)),
        out_shape=jax.ShapeDtypeStruct((DEPTH, 3, MOD_ROWS, D_MODEL), f32),
        compiler_params=_params(2),
        name="ada",
    )(c_all, w_ada, b_ada.reshape(DEPTH, 1, 3 * D_MODEL))


def _prompt_mod_row(ref, part, blk, blocks_per_batch):
    return ref[part, pl.ds(ROWS_S + blk // blocks_per_batch, 1), :]


LN_BLOCKS = ROWS_P // LN_TM


def _ln_row(i):
    return (jnp.minimum(i, LN_BLOCKS - 1), 0)


_LN_ROW_SPEC = pl.BlockSpec((LN_TM, D_MODEL), _ln_row)
_LN_SAMPLE_SPEC = pl.BlockSpec((ROWS_S, D_MODEL), lambda i: (0, 0))


def _matmul_kernel(*refs, n_w, tw):
    xp_ref, xs_ref = refs[0:2]
    w_refs = refs[2:2 + n_w]
    op_ref, os_ref, w_bf = refs[2 + n_w:]

    @pl.when(pl.program_id(1) == 0)
    def _():
        for a, w_ref in enumerate(w_refs):
            w_bf[:, a * tw:(a + 1) * tw] = w_ref[...].astype(bf16)
        os_ref[...] = jnp.dot(xs_ref[...], w_bf[...], preferred_element_type=f32).astype(os_ref.dtype)

    op_ref[...] = jnp.dot(xp_ref[...], w_bf[...], preferred_element_type=f32).astype(op_ref.dtype)


def _matmul(xp, xs, w, layer, n_tiles, w_tiles_of, out_dtype, name, tm=TM, tw=TN):
    k = xp.shape[1]
    n_w = len(w_tiles_of)
    tn = n_w * tw
    out_bytes = jnp.dtype(out_dtype).itemsize
    vmem = (2 * (tm + ROWS_S) * k * 2 + 2 * k * tn * 4 + k * tn * 2 + 2 * (tm + ROWS_S) * tn * out_bytes + (6 << 20))
    w_specs = [pl.BlockSpec((None, k, tw), functools.partial(lambda f, j, i: (layer, 0, f(j)), f))
               for f in w_tiles_of]
    return pl.pallas_call(
        functools.partial(_matmul_kernel, n_w=n_w, tw=tw),
        grid=(n_tiles, ROWS_P // tm),
        in_specs=[pl.BlockSpec((tm, k), lambda j, i: (i, 0)),
                  pl.BlockSpec((ROWS_S, k), lambda j, i: (0, 0))] + w_specs,
        out_specs=[pl.BlockSpec((tm, tn), lambda j, i: (i, j)),
                   pl.BlockSpec((ROWS_S, tn), lambda j, i: (0, j))],
        out_shape=[jax.ShapeDtypeStruct((ROWS_P, n_tiles * tn), out_dtype),
                   jax.ShapeDtypeStruct((ROWS_S, n_tiles * tn), out_dtype)],
        scratch_shapes=[pltpu.VMEM((k, tn), bf16)],
        compiler_params=_params(2, vmem),
        name=name,
    )(xp, xs, *([w] * n_w))


def _skip_kv_tile(c):
    return c + (c >= KV_TILE).astype(jnp.int32)


GA_COL = 5 * D_ATTN
GB_COL = GA_COL + D_MODEL


def _attention_stages(q_ref, k_all, v_all, mask, slot, slot_cols, live_key, sink_ref, za_ref, ab_ref, t, lk):
    lane = lax.broadcasted_iota(jnp.int32, (lk, 2 * HEAD_DIM), 1)
    low, high = lane < HEAD_DIM, lane >= HEAD_DIM
    ones2 = jnp.concatenate([low, high], axis=0).astype(f32).astype(bf16)
    nt = (((1,), (1,)), ((), ()))
    pairs = GQA_GROUP // 2
    k_scale = HEAD_DIM ** -0.5 * LOG2E

    def head(kv):
        cols = slice((kv // 2) * 2 * HEAD_DIM, (kv // 2 + 1) * 2 * HEAD_DIM)
        kp, vp = k_all[:, cols] * k_scale, v_all[:, cols]
        keep = (low if kv % 2 == 0 else high) & live_key
        k_own = jnp.where(keep, kp, 0.0)
        v_own = jnp.where(keep, vp, 0.0)
        k_oth = pltpu.roll(k_own, HEAD_DIM, 1)
        v_oth = pltpu.roll(v_own, HEAD_DIM, 1)
        if kv % 2 == 0:
            k2 = jnp.concatenate([k_own, k_oth], axis=0).astype(bf16)
            v2 = jnp.concatenate([v_own, v_oth], axis=0).astype(bf16)
        else:
            k2 = jnp.concatenate([k_oth, k_own], axis=0).astype(bf16)
            v2 = jnp.concatenate([v_oth, v_own], axis=0).astype(bf16)
        v2 = jnp.concatenate([v2, ones2], axis=1)
        base = kv * GQA_GROUP * HEAD_DIM
        q4 = jnp.concatenate(
            [q_ref[:, base + p * 128: base + (p + 1) * 128] for p in range(pairs)], axis=0)
        s_all = lax.dot_general(q4, k2, nt, preferred_element_type=f32)
        rows = []
        for p in range(pairs):
            halves = []
            for h in range(2):
                sink = sink_ref[kv * GQA_GROUP + 2 * p + h] * LOG2E
                fill = jnp.where(slot, sink, NEG)
                if slot_cols < lk:
                    fill = jnp.concatenate([fill, jnp.full((t, lk - slot_cols), NEG, f32)], axis=1)
                s = jnp.where(mask, s_all[p * t:(p + 1) * t, h * lk:(h + 1) * lk], fill)
                m = jnp.max(s, axis=1, keepdims=True)
                halves.append(jnp.exp2(s - m).astype(bf16))
            rows.append(jnp.concatenate(halves, axis=1))
        probs = jnp.concatenate(rows, axis=0)
        o = jnp.dot(probs, v2, preferred_element_type=f32)
        o = o[:, 0:128] / o[:, 128:256]
        for p in range(pairs):
            cols = slice(base + p * 128, base + (p + 1) * 128)
            ab_ref[:, cols] = (o[p * t:(p + 1) * t, :] * _silu(za_ref[:, cols].astype(f32))).astype(bf16)

    return [functools.partial(head, kv) for kv in range(N_KV_HEADS)]


def _sgu_norm(vb_ref, gain_ref, bias_ref):
    vb = vb_ref[...].astype(f32)
    mu = jnp.mean(vb, axis=1, keepdims=True)
    d = vb - mu
    var = jnp.mean(d * d, axis=1, keepdims=True)
    return d * lax.rsqrt(var + LN_EPS) * gain_ref[...] + bias_ref[...]


def _sgu_stages(u_ref, zb_ref, ws_ref, bs_ref, vn, w_mask, ab_ref):
    vn_bf = vn.astype(bf16)

    def group(g):
        cols = slice(g * SGU_GROUP_DIM, (g + 1) * SGU_GROUP_DIM)
        w_g = jnp.where(w_mask, ws_ref[g], 0.0).astype(bf16)
        s = jnp.dot(w_g, vn_bf[:, cols], preferred_element_type=f32) + bs_ref[:, g:g + 1]
        out_b = u_ref[:, cols].astype(f32) * s * _silu(zb_ref[:, cols].astype(f32))
        ab_ref[:, D_ATTN + g * SGU_GROUP_DIM: D_ATTN + (g + 1) * SGU_GROUP_DIM] = out_b.astype(bf16)

    return [functools.partial(group, g) for g in range(N_SGU_GROUPS)]


def _interleave(attention, sgu, others=()):
    per_head = len(sgu) // len(attention)
    for kv, head in enumerate(attention):
        if kv < len(others):
            others[kv]()
        head()
        for stage in sgu[kv * per_head:(kv + 1) * per_head]:
            stage()


def _mix_prompt_block(n, sink_ref, q_ref, za_ref, u_ref, vb_ref, zb_ref, kvc_ref, kvp_ref,
                      gain_ref, bias_ref, ws_ref, bs_ref, ab_ref, others=()):
    vn = _sgu_norm(vb_ref, gain_ref, bias_ref)
    kvc, kvp = kvc_ref[...], kvp_ref[...]
    k_all = jnp.concatenate([kvp[:, 0:D_KV], kvc[:, 0:D_KV]], axis=0)
    v_all = jnp.concatenate([kvp[:, D_KV:], kvc[:, D_KV:]], axis=0)
    qi = lax.broadcasted_iota(jnp.int32, (WINDOW, 2 * WINDOW), 0)
    kj = lax.broadcasted_iota(jnp.int32, (WINDOW, 2 * WINDOW), 1)
    no_prev = jnp.where(n > 0, 0, 2 * WINDOW)
    mask = ((kj < WINDOW) & (kj > qi + no_prev)) | ((kj >= WINDOW) & (kj - WINDOW <= qi))
    slot = lax.broadcasted_iota(jnp.int32, (WINDOW, WINDOW), 1) == 0
    live_key = lax.broadcasted_iota(jnp.int32, (2 * WINDOW, 2 * HEAD_DIM), 0) != 0
    heads = _attention_stages(q_ref, k_all, v_all, mask, slot, WINDOW, live_key, sink_ref, za_ref, ab_ref,
                              WINDOW, 2 * WINDOW)
    r = lax.broadcasted_iota(jnp.int32, (CHUNK, CHUNK), 0)
    c = lax.broadcasted_iota(jnp.int32, (CHUNK, CHUNK), 1)
    _interleave(heads, _sgu_stages(u_ref, zb_ref, ws_ref, bs_ref, vn, c <= r, ab_ref), others)


ROW_TILES = ROWS_P // TM
TILE_BLOCKS = TM // WINDOW
assert TILE_BLOCKS == D_MODEL // TN


def _mix_merge_kernel(sink_ref, q_ref, za_ref, u_ref, vb_ref, zb_ref, kvc_ref, kvp_ref,
                      gain_ref, bias_ref, ws_ref, bs_ref,
                      abs_ref, gap_ref, gbp_ref, gas_ref, gbs_ref, wa_ref, wb_ref,
                      op_ref, os_ref, ab_even, ab_odd, wa_bf, wb_bf):
    r, c = pl.program_id(0), pl.program_id(1)
    blk = jnp.minimum(r, ROW_TILES - 1) * TILE_BLOCKS + c

    def mix(ab_tile, others=()):
        mix_out = ab_tile.at[pl.ds(pl.multiple_of(c * WINDOW, WINDOW), WINDOW), :]
        _mix_prompt_block(_mod_pow2(blk, SEQ // WINDOW), sink_ref, q_ref, za_ref, u_ref, vb_ref, zb_ref,
                          kvc_ref, kvp_ref, gain_ref, bias_ref, ws_ref, bs_ref, mix_out, others)

    def merged(ab_ref, ga_ref, gb_ref):
        ya = jnp.dot(ab_ref[:, 0:D_ATTN], wa_bf[...], preferred_element_type=f32)
        yb = jnp.dot(ab_ref[:, D_ATTN:], wb_bf[...], preferred_element_type=f32)
        return (_sigmoid(ga_ref[...].astype(f32)) * ya + _sigmoid(gb_ref[...].astype(f32)) * yb).astype(bf16)

    def mix_and_project(mix_tile, done_tile):
        wa_bf[...] = wa_ref[...].astype(bf16)
        wb_bf[...] = wb_ref[...].astype(bf16)
        half = TN // 2
        parts = {}

        def piece(lhs_cols, w_bf, key, cols):
            parts[key] = jnp.dot(done_tile[:, lhs_cols], w_bf[:, cols], preferred_element_type=f32)

        pieces = [functools.partial(piece, lhs_cols, w_bf, (name, hc), slice(hc * half, (hc + 1) * half))
                  for name, lhs_cols, w_bf in (("a", slice(0, D_ATTN), wa_bf), ("b", slice(D_ATTN, D_MODEL), wb_bf))
                  for hc in range(2)]
        mix(mix_tile, pieces)
        for hc in range(2):
            cols = slice(hc * half, (hc + 1) * half)
            op_ref[:, cols] = (_sigmoid(gap_ref[:, cols].astype(f32)) * parts["a", hc]
                               + _sigmoid(gbp_ref[:, cols].astype(f32)) * parts["b", hc]).astype(bf16)

    @pl.when(r == 0)
    def _():
        mix(ab_even)

    @pl.when((r > 0) & (r % 2 == 1))
    def _():
        mix_and_project(ab_odd, ab_even)

    @pl.when((r > 0) & (r % 2 == 0))
    def _():
        mix_and_project(ab_even, ab_odd)

    @pl.when(r == ROW_TILES)
    def _():
        os_ref[...] = merged(abs_ref, gas_ref, gbs_ref)


def _mix_merge(proj_p, proj_s, kv_p, ab_s, sinks, gain, bias, w_s, b_s, w_pa, w_pb, layer):
    seq_blocks = SEQ // WINDOW
    ga_tile, gb_tile = GA_COL // TN, GB_COL // TN
    blk_of = lambda r, c: jnp.minimum(r, ROW_TILES - 1) * TILE_BLOCKS + c
    prev_of = lambda r, c: blk_of(r, c) - (_mod_pow2(blk_of(r, c), seq_blocks) > 0).astype(jnp.int32)
    tile_of = lambda r: jnp.maximum(r - 1, 0)
    col_of = lambda r, c: jnp.where(r == 0, 0, c)
    scol_of = lambda r, c: jnp.where(r == ROW_TILES, c, 0)
    full = lambda shape: pl.BlockSpec(shape, lambda r, c: (0,) * len(shape))
    return pl.pallas_call(
        _mix_merge_kernel,
        grid=(ROW_TILES + 1, TILE_BLOCKS),
        in_specs=[pl.BlockSpec(memory_space=pltpu.SMEM)] + _proj_specs(WINDOW, blk_of) + [
            pl.BlockSpec((WINDOW, 2 * D_KV), lambda r, c: (blk_of(r, c), 0)),
            pl.BlockSpec((WINDOW, 2 * D_KV), lambda r, c: (prev_of(r, c), 0)),
            full((1, D_SGU)), full((1, D_SGU)),
            full((N_SGU_GROUPS, CHUNK, CHUNK)), full((CHUNK, N_SGU_GROUPS)),
            full((ROWS_S, D_MODEL)),
            pl.BlockSpec((TM, TN), lambda r, c: (tile_of(r), ga_tile + col_of(r, c))),
            pl.BlockSpec((TM, TN), lambda r, c: (tile_of(r), gb_tile + col_of(r, c))),
            pl.BlockSpec((ROWS_S, TN), lambda r, c: (0, ga_tile + scol_of(r, c))),
            pl.BlockSpec((ROWS_S, TN), lambda r, c: (0, gb_tile + scol_of(r, c))),
            pl.BlockSpec((None, D_ATTN, TN), lambda r, c: (layer, 0, col_of(r, c))),
            pl.BlockSpec((None, D_SGU, TN), lambda r, c: (layer, 0, col_of(r, c))),
        ],
        out_specs=[pl.BlockSpec((TM, TN), lambda r, c: (tile_of(r), col_of(r, c))),
                   pl.BlockSpec((ROWS_S, TN), lambda r, c: (0, scol_of(r, c)))],
        out_shape=[jax.ShapeDtypeStruct((ROWS_P, D_MODEL), bf16), jax.ShapeDtypeStruct((ROWS_S, D_MODEL), bf16)],
        scratch_shapes=[pltpu.VMEM((TM, D_MODEL), bf16), pltpu.VMEM((TM, D_MODEL), bf16),
                        pltpu.VMEM((D_ATTN, TN), bf16), pltpu.VMEM((D_SGU, TN), bf16)],
        compiler_params=_params(2),
        name="mix_merge",
    )(sinks, proj_p, proj_p, proj_p, proj_p, proj_p, kv_p, kv_p, gain, bias, w_s, b_s,
      ab_s, proj_p, proj_p, proj_s, proj_s, w_pa, w_pb)


def _mix_sample_kernel(sink_ref, q_ref, za_ref, u_ref, vb_ref, zb_ref, kvn_ref, ck_ref, cv_ref,
                       gain_ref, bias_ref, ws_ref, bs_ref, ab_ref, vn_ref):
    t = SEQ_GROUP * DEC_SEQ
    n_cache = SEQ_GROUP * WINDOW
    lk = n_cache + WINDOW
    kvn = kvn_ref[...]
    pad = jnp.zeros((WINDOW - t, D_KV), f32)
    k_all = jnp.concatenate([ck_ref[...].reshape(n_cache, D_KV), kvn[:, 0:D_KV], pad], axis=0)
    v_all = jnp.concatenate([cv_ref[...].reshape(n_cache, D_KV), kvn[:, D_KV:], pad], axis=0)
    qi = lax.broadcasted_iota(jnp.int32, (t, lk), 0)
    kj = lax.broadcasted_iota(jnp.int32, (t, lk), 1)
    q_seq, q_tok = _div_pow2(qi, DEC_SEQ), _mod_pow2(qi, DEC_SEQ)
    in_cache = (_div_pow2(kj, WINDOW) == q_seq) & (_mod_pow2(kj, WINDOW) > q_tok)
    kn = jnp.maximum(kj - n_cache, 0)
    in_new = (kn < t) & (_div_pow2(kn, DEC_SEQ) == q_seq) & (_mod_pow2(kn, DEC_SEQ) <= q_tok)
    mask = ((kj < n_cache) & in_cache) | ((kj >= n_cache) & in_new)
    slot_seq = _div_pow2(lax.broadcasted_iota(jnp.int32, (t, n_cache), 0), DEC_SEQ)
    slot = lax.broadcasted_iota(jnp.int32, (t, n_cache), 1) == slot_seq * WINDOW
    key_row = lax.broadcasted_iota(jnp.int32, (lk, 2 * HEAD_DIM), 0)
    live_key = (key_row >= n_cache) | (_mod_pow2(key_row, WINDOW) != 0)
    heads = _attention_stages(q_ref, k_all, v_all, mask, slot, n_cache, live_key, sink_ref, za_ref, ab_ref, t, lk)

    vn = _sgu_norm(vb_ref, gain_ref, bias_ref)
    vn_ref[...] = vn
    r = lax.broadcasted_iota(jnp.int32, (t, t), 0)
    c = lax.broadcasted_iota(jnp.int32, (t, t), 1)
    w_mask = (_div_pow2(r, DEC_SEQ) == _div_pow2(c, DEC_SEQ)) & (c <= r)
    _interleave(heads, _sgu_stages(u_ref, zb_ref, ws_ref, bs_ref, vn, w_mask, ab_ref))


def _proj_specs(rows, row_of):
    return [pl.BlockSpec((rows, D_ATTN), functools.partial(lambda s, *g: (row_of(*g), s), s))
            for s in range(5)]


def _mix_sample(proj, kv, cache_k, cache_v, layer, sinks, gain, bias, w_s, b_s):
    t = SEQ_GROUP * DEC_SEQ
    groups = DEC_BATCH // SEQ_GROUP
    full = lambda shape: pl.BlockSpec(shape, lambda g: (0,) * len(shape))
    cache_spec = pl.BlockSpec((SEQ_GROUP, WINDOW, D_KV), lambda g: (layer * groups + g, 0, 0))
    return pl.pallas_call(
        _mix_sample_kernel,
        grid=(groups,),
        in_specs=[pl.BlockSpec(memory_space=pltpu.SMEM)] + _proj_specs(t, lambda g: g) + [
            pl.BlockSpec((t, 2 * D_KV), lambda g: (g, 0)),
            cache_spec, cache_spec,
            full((1, D_SGU)), full((1, D_SGU)),
            full((N_SGU_GROUPS, t, t)), full((t, N_SGU_GROUPS)),
        ],
        out_specs=[pl.BlockSpec((t, D_MODEL), lambda g: (g, 0)),
                   pl.BlockSpec((t, D_SGU), lambda g: (g, 0))],
        out_shape=[jax.ShapeDtypeStruct((ROWS_S, D_MODEL), bf16),
                   jax.ShapeDtypeStruct((ROWS_S, D_SGU), f32)],
        compiler_params=_params(1),
        name="mix_sample",
    )(sinks, proj, proj, proj, proj, proj, kv, cache_k, cache_v, gain, bias, w_s, b_s)


def _post_norm(x, y, gate, g_ref, b_ref):
    t = x + (gate * (1.0 / ALPHA)) * y.astype(f32)
    mu = jnp.mean(t, axis=1, keepdims=True)
    d = t - mu
    var = jnp.mean(d * d, axis=1, keepdims=True)
    return d * lax.rsqrt(var + LN_EPS / ALPHA ** 2) * g_ref[...] + b_ref[...]


def _ln_kernel(*refs, n):
    xp_ref, xs_ref = refs[0:2]
    y_refs = refs[2:2 + 2 * n]
    gate_refs = refs[2 + 2 * n:2 + 3 * n]
    gb_refs = refs[2 + 3 * n:2 + 5 * n]
    rest = refs[2 + 5 * n:]
    i = pl.program_id(0)

    def stream(x, sample):
        for l in range(n):
            if sample:
                gate = gate_refs[l][0, 0:ROWS_S, :]
            else:
                gate = _prompt_mod_row(gate_refs[l], 0, i, SEQ // LN_TM)
            x = _post_norm(x, y_refs[2 * l + sample][...], gate, gb_refs[2 * l], gb_refs[2 * l + 1])
        return x

    if n == DEPTH:
        xp_out, xs_out = rest

        @pl.when(i < LN_BLOCKS)
        def _():
            xp_out[...] = stream(xp_ref[...], 0)

        @pl.when(i == LN_BLOCKS)
        def _():
            xs_out[...] = stream(xs_ref[...], 1)
    else:
        nxt_ref, w_ref, hp_out, hs_out, kvp_out, kvs_out, w_bf = rest

        @pl.when(i == 0)
        def _():
            w_bf[...] = w_ref[...].astype(bf16)

        @pl.when(i < LN_BLOCKS)
        def _():
            shift = _prompt_mod_row(nxt_ref, 0, i, SEQ // LN_TM)
            scale = _prompt_mod_row(nxt_ref, 1, i, SEQ // LN_TM)
            h = (stream(xp_ref[...], 0) * (1.0 + scale) + shift).astype(bf16)
            hp_out[...] = h
            kvp_out[...] = jnp.dot(h, w_bf[...], preferred_element_type=f32)

        @pl.when(i == LN_BLOCKS)
        def _():
            h = (stream(xs_ref[...], 1) * (1.0 + nxt_ref[1, 0:ROWS_S, :]) + nxt_ref[0, 0:ROWS_S, :]).astype(bf16)
            hs_out[...] = h
            kvs_out[...] = jnp.dot(h, w_bf[...], preferred_element_type=f32)


def _stream(xp, xs, ys, mod, ln_gain, ln_bias, w_in):
    n = len(ys)
    in_specs = [_LN_ROW_SPEC, _LN_SAMPLE_SPEC] + [_LN_ROW_SPEC, _LN_SAMPLE_SPEC] * n
    in_specs += [pl.BlockSpec((None, 1, MOD_ROWS, D_MODEL), functools.partial(lambda l, i: (l, 2, 0, 0), l))
                 for l in range(n)]
    in_specs += [pl.BlockSpec((None, 1, D_MODEL), functools.partial(lambda l, i: (l, 0, 0), l))
                 for l in range(n) for _ in range(2)]
    args = [xp, xs] + [y for pair in ys for y in pair] + [mod] * n
    for l in range(n):
        args += [ln_gain.reshape(DEPTH, 1, D_MODEL), ln_bias.reshape(DEPTH, 1, D_MODEL)]
    if n == DEPTH:
        out_specs = [_LN_ROW_SPEC, _LN_SAMPLE_SPEC]
        out_shape = [jax.ShapeDtypeStruct((ROWS_P, D_MODEL), f32), jax.ShapeDtypeStruct((ROWS_S, D_MODEL), f32)]
        scratch = []
    else:
        in_specs += [pl.BlockSpec((None, 2, MOD_ROWS, D_MODEL), lambda i: (n, 0, 0, 0)),
                     pl.BlockSpec((None, D_MODEL, TN), lambda i: (n, 0, KV_TILE), pipeline_mode=pl.Buffered(1))]
        args += [mod, w_in]
        out_specs = [_LN_ROW_SPEC, _LN_SAMPLE_SPEC, pl.BlockSpec((LN_TM, 2 * D_KV), _ln_row),
                     pl.BlockSpec((ROWS_S, 2 * D_KV), lambda i: (0, 0))]
        out_shape = [jax.ShapeDtypeStruct((ROWS_P, D_MODEL), bf16), jax.ShapeDtypeStruct((ROWS_S, D_MODEL), bf16),
                     jax.ShapeDtypeStruct((ROWS_P, 2 * D_KV), f32), jax.ShapeDtypeStruct((ROWS_S, 2 * D_KV), f32)]
        scratch = [pltpu.VMEM((D_MODEL, TN), bf16)]
    return pl.pallas_call(
        functools.partial(_ln_kernel, n=n),
        grid=(LN_BLOCKS + 1,),
        in_specs=in_specs,
        out_specs=out_specs,
        out_shape=out_shape,
        scratch_shapes=scratch,
        compiler_params=_params(1),
        name=f"stream{n}",
    )(*args)


def kernel(x_prompt, x_sample, cache_k, cache_v, c_prompt, c_sample, w_ada, b_ada, w_in, attn_sinks,
           sgu_ln_gain, sgu_ln_bias, sgu_w_s, sgu_b_s, w_pa, w_pb, w_o, ln_gain, ln_bias):
    xp = x_prompt.reshape(ROWS_P, D_MODEL)
    xs = x_sample.reshape(ROWS_S, D_MODEL)
    c_all = jnp.concatenate([jnp.repeat(c_sample, DEC_SEQ, axis=0), c_prompt,
                             jnp.zeros((MOD_ROWS - ROWS_S - BATCH, D_MODEL), f32)], axis=0)
    mod = _ada(c_all, w_ada, b_ada)
    ck = cache_k.reshape(DEPTH * DEC_BATCH, WINDOW, D_KV)
    cv = cache_v.reshape(DEPTH * DEC_BATCH, WINDOW, D_KV)

    win_k, win_v, new_k, new_v, sgu_v, ys = [], [], [], [], [], []
    for l in range(DEPTH):
        hp, hs, kv_p, kv_s = _stream(xp, xs, ys, mod, ln_gain, ln_bias, w_in)
        proj_p, proj_s = _matmul(hp, hs, w_in, l, PROJ_COLS // TN, [_skip_kv_tile], bf16, "proj")
        gain = sgu_ln_gain[l].reshape(1, D_SGU)
        bias = sgu_ln_bias[l].reshape(1, D_SGU)
        w_small = jnp.tile(sgu_w_s[l][:, :DEC_SEQ, :DEC_SEQ], (1, SEQ_GROUP, SEQ_GROUP))
        b_small = jnp.tile(sgu_b_s[l][:, :DEC_SEQ].T, (SEQ_GROUP, 1))
        ab_s, vn_s = _mix_sample(proj_s, kv_s, ck, cv, l, attn_sinks[l], gain, bias, w_small, b_small)
        m_p, m_s = _mix_merge(proj_p, proj_s, kv_p, ab_s, attn_sinks[l], gain, bias, sgu_w_s[l], sgu_b_s[l].T,
                              w_pa, w_pb, l)
        ys.append(_matmul(m_p, m_s, w_o, l, D_MODEL // TN, [lambda j: j], bf16, "out"))

        kv_win = kv_p.reshape(BATCH, SEQ, 2 * D_KV)[:, SEQ - WINDOW:]
        kv_win = kv_win.reshape(BATCH, WINDOW, 2, N_KV_HEADS, HEAD_DIM)
        kv_new = kv_s.reshape(DEC_BATCH, DEC_SEQ, 2, N_KV_HEADS, HEAD_DIM)
        win_k.append(kv_win[:, :, 0])
        win_v.append(kv_win[:, :, 1])
        new_k.append(kv_new[:, :, 0])
        new_v.append(kv_new[:, :, 1])
        sgu_v.append(vn_s.reshape(DEC_BATCH, DEC_SEQ, D_SGU))

    xp, xs = _stream(xp, xs, ys, mod, ln_gain, ln_bias, w_in)
    return (xp.reshape(BATCH, SEQ, D_MODEL), xs.reshape(DEC_BATCH, DEC_SEQ, D_MODEL),
            jnp.stack(win_k), jnp.stack(win_v), jnp.stack(new_k), jnp.stack(new_v), jnp.stack(sgu_v))
```

```python
import functools

import jax
import jax.numpy as jnp
from jax import lax
from jax.experimental import pallas as pl
from jax.experimental.pallas import tpu as pltpu

D_MODEL = 4096
BATCH = 4
SEQ = 2048
DEPTH = 2
DEC_BATCH = 32
DEC_SEQ = 4
HEAD_DIM = 64
D_ATTN = D_MODEL // 2
N_Q_HEADS = D_ATTN // HEAD_DIM
N_KV_HEADS = N_Q_HEADS // 8
GQA_GROUP = N_Q_HEADS // N_KV_HEADS
D_KV = N_KV_HEADS * HEAD_DIM
WINDOW = 128
D_SGU = D_MODEL // 2
N_SGU_GROUPS = 8
SGU_GROUP_DIM = D_SGU // N_SGU_GROUPS
CHUNK = 128
ALPHA = (2 * DEPTH) ** 0.25
LN_EPS = 1e-5
IN_COLS = D_ATTN + 2 * D_KV + D_ATTN + 3 * D_SGU + 2 * D_MODEL

ROWS_P = BATCH * SEQ
ROWS_S = DEC_BATCH * DEC_SEQ
MOD_ROWS = ROWS_S + 8
PROJ_COLS = IN_COLS - 2 * D_KV

TN = 512
TM = 1024
MM_TM = 2048
KV_TILE = D_ATTN // TN
LN_TM = 256
SEQ_GROUP = 8
NEG = -1e30
LOG2E = 1.4426950408889634

VMEM_LIMIT = 56 * 1024 * 1024

bf16 = jnp.bfloat16
f32 = jnp.float32


def _params(n_axes, vmem=VMEM_LIMIT):
    return pltpu.CompilerParams(dimension_semantics=("arbitrary",) * n_axes, vmem_limit_bytes=vmem)


def _sigmoid(x):
    return 0.5 + 0.5 * jnp.tanh(0.5 * x)


def _silu(x):
    half = 0.5 * x
    return half + half * jnp.tanh(half)


def _div_pow2(x, n):
    assert n & (n - 1) == 0
    return x >> (n.bit_length() - 1)


def _mod_pow2(x, n):
    assert n & (n - 1) == 0
    return x & (n - 1)


def _ada_kernel(c_ref, w_ref, b_ref, o_ref):
    a = _silu(c_ref[...]).astype(bf16)
    o_ref[...] = jnp.dot(a, w_ref[...].astype(bf16), preferred_element_type=f32) + b_ref[...]


def _ada(c_all, w_ada, b_ada):
    tiles_per_part = D_MODEL // TN
    return pl.pallas_call(
        _ada_kernel,
        grid=(DEPTH, 3 * tiles_per_part),
        in_specs=[
            pl.BlockSpec((MOD_ROWS, D_MODEL), lambda l, j: (0, 0)),
            pl.BlockSpec((None, D_MODEL, TN), lambda l, j: (l, 0, j)),
            pl.BlockSpec((None, 1, TN), lambda l, j: (l, 0, j)),
        ],
        out_specs=pl.BlockSpec((None, None, MOD_ROWS, TN),
                               lambda l, j: (l, j // tiles_per_part, 0, j % tiles_per_part)),
        out_shape=jax.ShapeDtypeStruct((DEPTH, 3, MOD_ROWS, D_MODEL), f32),
        compiler_params=_params(2),
        name="ada",
    )(c_all, w_ada, b_ada.reshape(DEPTH, 1, 3 * D_MODEL))


def _prompt_mod_row(ref, part, blk, blocks_per_batch):
    return ref[part, pl.ds(ROWS_S + blk // blocks_per_batch, 1), :]


LN_BLOCKS = ROWS_P // LN_TM


def _ln_row(i):
    return (jnp.minimum(i, LN_BLOCKS - 1), 0)


_LN_ROW_SPEC = pl.BlockSpec((LN_TM, D_MODEL), _ln_row)
_LN_SAMPLE_SPEC = pl.BlockSpec((ROWS_S, D_MODEL), lambda i: (0, 0))


def _matmul_kernel(xp_ref, xs_ref, w_ref, op_ref, os_ref, w_bf):
    @pl.when(pl.program_id(1) == 0)
    def _():
        w_bf[...] = w_ref[...].astype(bf16)
        os_ref[...] = jnp.dot(xs_ref[...], w_bf[...], preferred_element_type=f32).astype(os_ref.dtype)

    op_ref[...] = jnp.dot(xp_ref[...], w_bf[...], preferred_element_type=f32).astype(op_ref.dtype)


def _matmul(xp, xs, w, layer, n_tiles, w_tile_of, out_dtype, name):
    k = xp.shape[1]
    out_bytes = jnp.dtype(out_dtype).itemsize
    vmem = (2 * (MM_TM + ROWS_S) * k * 2 + 2 * k * TN * 4 + k * TN * 2 + 2 * (MM_TM + ROWS_S) * TN * out_bytes
            + (4 << 20))
    return pl.pallas_call(
        _matmul_kernel,
        grid=(n_tiles, ROWS_P // MM_TM),
        in_specs=[pl.BlockSpec((MM_TM, k), lambda j, i: (i, 0)),
                  pl.BlockSpec((ROWS_S, k), lambda j, i: (0, 0)),
                  pl.BlockSpec((None, k, TN), lambda j, i: (layer, 0, w_tile_of(j)))],
        out_specs=[pl.BlockSpec((MM_TM, TN), lambda j, i: (i, j)),
                   pl.BlockSpec((ROWS_S, TN), lambda j, i: (0, j))],
        out_shape=[jax.ShapeDtypeStruct((ROWS_P, n_tiles * TN), out_dtype),
                   jax.ShapeDtypeStruct((ROWS_S, n_tiles * TN), out_dtype)],
        scratch_shapes=[pltpu.VMEM((k, TN), bf16)],
        compiler_params=_params(2, vmem),
        name=name,
    )(xp, xs, w)


def _skip_kv_tile(c):
    return c + (c >= KV_TILE).astype(jnp.int32)


GA_COL = 5 * D_ATTN
GB_COL = GA_COL + D_MODEL


def _attention_stages(q_ref, k_all, v_all, mask, slot, slot_cols, live_key, sink_ref, za_ref, ab_ref, t, lk):
    lane = lax.broadcasted_iota(jnp.int32, (lk, 2 * HEAD_DIM), 1)
    low, high = lane < HEAD_DIM, lane >= HEAD_DIM
    ones2 = jnp.concatenate([low, high], axis=0).astype(f32).astype(bf16)
    nt = (((1,), (1,)), ((), ()))
    pairs = GQA_GROUP // 2
    k_scale = HEAD_DIM ** -0.5 * LOG2E

    def head(kv):
        cols = slice((kv // 2) * 2 * HEAD_DIM, (kv // 2 + 1) * 2 * HEAD_DIM)
        kp, vp = k_all[:, cols] * k_scale, v_all[:, cols]
        keep = (low if kv % 2 == 0 else high) & live_key
        k_own = jnp.where(keep, kp, 0.0)
        v_own = jnp.where(keep, vp, 0.0)
        k_oth = pltpu.roll(k_own, HEAD_DIM, 1)
        v_oth = pltpu.roll(v_own, HEAD_DIM, 1)
        if kv % 2 == 0:
            k2 = jnp.concatenate([k_own, k_oth], axis=0).astype(bf16)
            v2 = jnp.concatenate([v_own, v_oth], axis=0).astype(bf16)
        else:
            k2 = jnp.concatenate([k_oth, k_own], axis=0).astype(bf16)
            v2 = jnp.concatenate([v_oth, v_own], axis=0).astype(bf16)
        v2 = jnp.concatenate([v2, ones2], axis=1)
        base = kv * GQA_GROUP * HEAD_DIM
        q4 = jnp.concatenate(
            [q_ref[:, base + p * 128: base + (p + 1) * 128] for p in range(pairs)], axis=0)
        s_all = lax.dot_general(q4, k2, nt, preferred_element_type=f32)
        rows = []
        for p in range(pairs):
            halves = []
            for h in range(2):
                sink = sink_ref[kv * GQA_GROUP + 2 * p + h] * LOG2E
                fill = jnp.where(slot, sink, NEG)
                if slot_cols < lk:
                    fill = jnp.concatenate([fill, jnp.full((t, lk - slot_cols), NEG, f32)], axis=1)
                s = jnp.where(mask, s_all[p * t:(p + 1) * t, h * lk:(h + 1) * lk], fill)
                m = jnp.max(s, axis=1, keepdims=True)
                halves.append(jnp.exp2(s - m).astype(bf16))
            rows.append(jnp.concatenate(halves, axis=1))
        probs = jnp.concatenate(rows, axis=0)
        o = jnp.dot(probs, v2, preferred_element_type=f32)
        o = o[:, 0:128] / o[:, 128:256]
        for p in range(pairs):
            cols = slice(base + p * 128, base + (p + 1) * 128)
            ab_ref[:, cols] = (o[p * t:(p + 1) * t, :] * _silu(za_ref[:, cols].astype(f32))).astype(bf16)

    return [functools.partial(head, kv) for kv in range(N_KV_HEADS)]


def _sgu_norm(vb_ref, gain_ref, bias_ref):
    vb = vb_ref[...].astype(f32)
    mu = jnp.mean(vb, axis=1, keepdims=True)
    d = vb - mu
    var = jnp.mean(d * d, axis=1, keepdims=True)
    return d * lax.rsqrt(var + LN_EPS) * gain_ref[...] + bias_ref[...]


def _sgu_stages(u_ref, zb_ref, ws_ref, bs_ref, vn, w_mask, ab_ref):
    vn_bf = vn.astype(bf16)

    def group(g):
        cols = slice(g * SGU_GROUP_DIM, (g + 1) * SGU_GROUP_DIM)
        w_g = jnp.where(w_mask, ws_ref[g], 0.0).astype(bf16)
        s = jnp.dot(w_g, vn_bf[:, cols], preferred_element_type=f32) + bs_ref[:, g:g + 1]
        out_b = u_ref[:, cols].astype(f32) * s * _silu(zb_ref[:, cols].astype(f32))
        ab_ref[:, D_ATTN + g * SGU_GROUP_DIM: D_ATTN + (g + 1) * SGU_GROUP_DIM] = out_b.astype(bf16)

    return [functools.partial(group, g) for g in range(N_SGU_GROUPS)]


def _interleave(attention, sgu, others=()):
    per_head = len(sgu) // len(attention)
    for kv, head in enumerate(attention):
        if kv < len(others):
            others[kv]()
        head()
        for stage in sgu[kv * per_head:(kv + 1) * per_head]:
            stage()


def _mix_prompt_block(n, sink_ref, q_ref, za_ref, u_ref, vb_ref, zb_ref, kvc_ref, kvp_ref,
                      gain_ref, bias_ref, ws_ref, bs_ref, ab_ref, others=()):
    vn = _sgu_norm(vb_ref, gain_ref, bias_ref)
    kvc, kvp = kvc_ref[...], kvp_ref[...]
    k_all = jnp.concatenate([kvp[:, 0:D_KV], kvc[:, 0:D_KV]], axis=0)
    v_all = jnp.concatenate([kvp[:, D_KV:], kvc[:, D_KV:]], axis=0)
    qi = lax.broadcasted_iota(jnp.int32, (WINDOW, 2 * WINDOW), 0)
    kj = lax.broadcasted_iota(jnp.int32, (WINDOW, 2 * WINDOW), 1)
    no_prev = jnp.where(n > 0, 0, 2 * WINDOW)
    mask = ((kj < WINDOW) & (kj > qi + no_prev)) | ((kj >= WINDOW) & (kj - WINDOW <= qi))
    slot = lax.broadcasted_iota(jnp.int32, (WINDOW, WINDOW), 1) == 0
    live_key = lax.broadcasted_iota(jnp.int32, (2 * WINDOW, 2 * HEAD_DIM), 0) != 0
    heads = _attention_stages(q_ref, k_all, v_all, mask, slot, WINDOW, live_key, sink_ref, za_ref, ab_ref,
                              WINDOW, 2 * WINDOW)
    r = lax.broadcasted_iota(jnp.int32, (CHUNK, CHUNK), 0)
    c = lax.broadcasted_iota(jnp.int32, (CHUNK, CHUNK), 1)
    _interleave(heads, _sgu_stages(u_ref, zb_ref, ws_ref, bs_ref, vn, c <= r, ab_ref), others)


ROW_TILES = ROWS_P // TM
TILE_BLOCKS = TM // WINDOW
assert TILE_BLOCKS == D_MODEL // TN


def _mix_merge_kernel(sink_ref, q_ref, za_ref, u_ref, vb_ref, zb_ref, kvc_ref, kvp_ref,
                      gain_ref, bias_ref, ws_ref, bs_ref,
                      abs_ref, gap_ref, gbp_ref, gas_ref, gbs_ref, wa_ref, wb_ref,
                      op_ref, os_ref, ab_even, ab_odd, wa_bf, wb_bf):
    r, c = pl.program_id(0), pl.program_id(1)
    blk = jnp.minimum(r, ROW_TILES - 1) * TILE_BLOCKS + c

    def mix(ab_tile, others=()):
        mix_out = ab_tile.at[pl.ds(pl.multiple_of(c * WINDOW, WINDOW), WINDOW), :]
        _mix_prompt_block(_mod_pow2(blk, SEQ // WINDOW), sink_ref, q_ref, za_ref, u_ref, vb_ref, zb_ref,
                          kvc_ref, kvp_ref, gain_ref, bias_ref, ws_ref, bs_ref, mix_out, others)

    def merged(ab_ref, ga_ref, gb_ref):
        ya = jnp.dot(ab_ref[:, 0:D_ATTN], wa_bf[...], preferred_element_type=f32)
        yb = jnp.dot(ab_ref[:, D_ATTN:], wb_bf[...], preferred_element_type=f32)
        return (_sigmoid(ga_ref[...].astype(f32)) * ya + _sigmoid(gb_ref[...].astype(f32)) * yb).astype(bf16)

    def mix_and_project(mix_tile, done_tile):
        wa_bf[...] = wa_ref[...].astype(bf16)
        wb_bf[...] = wb_ref[...].astype(bf16)
        half = TN // 2
        parts = {}

        def piece(lhs_cols, w_bf, key, cols):
            parts[key] = jnp.dot(done_tile[:, lhs_cols], w_bf[:, cols], preferred_element_type=f32)

        pieces = [functools.partial(piece, lhs_cols, w_bf, (name, hc), slice(hc * half, (hc + 1) * half))
                  for name, lhs_cols, w_bf in (("a", slice(0, D_ATTN), wa_bf), ("b", slice(D_ATTN, D_MODEL), wb_bf))
                  for hc in range(2)]
        mix(mix_tile, pieces)
        for hc in range(2):
            cols = slice(hc * half, (hc + 1) * half)
            op_ref[:, cols] = (_sigmoid(gap_ref[:, cols].astype(f32)) * parts["a", hc]
                               + _sigmoid(gbp_ref[:, cols].astype(f32)) * parts["b", hc]).astype(bf16)

    @pl.when(r == 0)
    def _():
        mix(ab_even)

    @pl.when((r > 0) & (r % 2 == 1))
    def _():
        mix_and_project(ab_odd, ab_even)

    @pl.when((r > 0) & (r % 2 == 0))
    def _():
        mix_and_project(ab_even, ab_odd)

    @pl.when(r == ROW_TILES)
    def _():
        os_ref[...] = merged(abs_ref, gas_ref, gbs_ref)


def _mix_merge(proj_p, proj_s, kv_p, ab_s, sinks, gain, bias, w_s, b_s, w_pa, w_pb, layer):
    seq_blocks = SEQ // WINDOW
    ga_tile, gb_tile = GA_COL // TN, GB_COL // TN
    blk_of = lambda r, c: jnp.minimum(r, ROW_TILES - 1) * TILE_BLOCKS + c
    prev_of = lambda r, c: blk_of(r, c) - (_mod_pow2(blk_of(r, c), seq_blocks) > 0).astype(jnp.int32)
    tile_of = lambda r: jnp.maximum(r - 1, 0)
    col_of = lambda r, c: jnp.where(r == 0, 0, c)
    scol_of = lambda r, c: jnp.where(r == ROW_TILES, c, 0)
    full = lambda shape: pl.BlockSpec(shape, lambda r, c: (0,) * len(shape))
    return pl.pallas_call(
        _mix_merge_kernel,
        grid=(ROW_TILES + 1, TILE_BLOCKS),
        in_specs=[pl.BlockSpec(memory_space=pltpu.SMEM)] + _proj_specs(WINDOW, blk_of) + [
            pl.BlockSpec((WINDOW, 2 * D_KV), lambda r, c: (blk_of(r, c), 0)),
            pl.BlockSpec((WINDOW, 2 * D_KV), lambda r, c: (prev_of(r, c), 0)),
            full((1, D_SGU)), full((1, D_SGU)),
            full((N_SGU_GROUPS, CHUNK, CHUNK)), full((CHUNK, N_SGU_GROUPS)),
            full((ROWS_S, D_MODEL)),
            pl.BlockSpec((TM, TN), lambda r, c: (tile_of(r), ga_tile + col_of(r, c))),
            pl.BlockSpec((TM, TN), lambda r, c: (tile_of(r), gb_tile + col_of(r, c))),
            pl.BlockSpec((ROWS_S, TN), lambda r, c: (0, ga_tile + scol_of(r, c))),
            pl.BlockSpec((ROWS_S, TN), lambda r, c: (0, gb_tile + scol_of(r, c))),
            pl.BlockSpec((None, D_ATTN, TN), lambda r, c: (layer, 0, col_of(r, c))),
            pl.BlockSpec((None, D_SGU, TN), lambda r, c: (layer, 0, col_of(r, c))),
        ],
        out_specs=[pl.BlockSpec((TM, TN), lambda r, c: (tile_of(r), col_of(r, c))),
                   pl.BlockSpec((ROWS_S, TN), lambda r, c: (0, scol_of(r, c)))],
        out_shape=[jax.ShapeDtypeStruct((ROWS_P, D_MODEL), bf16), jax.ShapeDtypeStruct((ROWS_S, D_MODEL), bf16)],
        scratch_shapes=[pltpu.VMEM((TM, D_MODEL), bf16), pltpu.VMEM((TM, D_MODEL), bf16),
                        pltpu.VMEM((D_ATTN, TN), bf16), pltpu.VMEM((D_SGU, TN), bf16)],
        compiler_params=_params(2),
        name="mix_merge",
    )(sinks, proj_p, proj_p, proj_p, proj_p, proj_p, kv_p, kv_p, gain, bias, w_s, b_s,
      ab_s, proj_p, proj_p, proj_s, proj_s, w_pa, w_pb)


def _mix_sample_kernel(sink_ref, q_ref, za_ref, u_ref, vb_ref, zb_ref, kvn_ref, ck_ref, cv_ref,
                       gain_ref, bias_ref, ws_ref, bs_ref, ab_ref, vn_ref):
    t = SEQ_GROUP * DEC_SEQ
    n_cache = SEQ_GROUP * WINDOW
    lk = n_cache + WINDOW
    kvn = kvn_ref[...]
    pad = jnp.zeros((WINDOW - t, D_KV), f32)
    k_all = jnp.concatenate([ck_ref[...].reshape(n_cache, D_KV), kvn[:, 0:D_KV], pad], axis=0)
    v_all = jnp.concatenate([cv_ref[...].reshape(n_cache, D_KV), kvn[:, D_KV:], pad], axis=0)
    qi = lax.broadcasted_iota(jnp.int32, (t, lk), 0)
    kj = lax.broadcasted_iota(jnp.int32, (t, lk), 1)
    q_seq, q_tok = _div_pow2(qi, DEC_SEQ), _mod_pow2(qi, DEC_SEQ)
    in_cache = (_div_pow2(kj, WINDOW) == q_seq) & (_mod_pow2(kj, WINDOW) > q_tok)
    kn = jnp.maximum(kj - n_cache, 0)
    in_new = (kn < t) & (_div_pow2(kn, DEC_SEQ) == q_seq) & (_mod_pow2(kn, DEC_SEQ) <= q_tok)
    mask = ((kj < n_cache) & in_cache) | ((kj >= n_cache) & in_new)
    slot_seq = _div_pow2(lax.broadcasted_iota(jnp.int32, (t, n_cache), 0), DEC_SEQ)
    slot = lax.broadcasted_iota(jnp.int32, (t, n_cache), 1) == slot_seq * WINDOW
    key_row = lax.broadcasted_iota(jnp.int32, (lk, 2 * HEAD_DIM), 0)
    live_key = (key_row >= n_cache) | (_mod_pow2(key_row, WINDOW) != 0)
    heads = _attention_stages(q_ref, k_all, v_all, mask, slot, n_cache, live_key, sink_ref, za_ref, ab_ref, t, lk)

    vn = _sgu_norm(vb_ref, gain_ref, bias_ref)
    vn_ref[...] = vn
    r = lax.broadcasted_iota(jnp.int32, (t, t), 0)
    c = lax.broadcasted_iota(jnp.int32, (t, t), 1)
    w_mask = (_div_pow2(r, DEC_SEQ) == _div_pow2(c, DEC_SEQ)) & (c <= r)
    _interleave(heads, _sgu_stages(u_ref, zb_ref, ws_ref, bs_ref, vn, w_mask, ab_ref))


def _proj_specs(rows, row_of):
    return [pl.BlockSpec((rows, D_ATTN), functools.partial(lambda s, *g: (row_of(*g), s), s))
            for s in range(5)]


def _mix_sample(proj, kv, cache_k, cache_v, layer, sinks, gain, bias, w_s, b_s):
    t = SEQ_GROUP * DEC_SEQ
    groups = DEC_BATCH // SEQ_GROUP
    full = lambda shape: pl.BlockSpec(shape, lambda g: (0,) * len(shape))
    cache_spec = pl.BlockSpec((SEQ_GROUP, WINDOW, D_KV), lambda g: (layer * groups + g, 0, 0))
    return pl.pallas_call(
        _mix_sample_kernel,
        grid=(groups,),
        in_specs=[pl.BlockSpec(memory_space=pltpu.SMEM)] + _proj_specs(t, lambda g: g) + [
            pl.BlockSpec((t, 2 * D_KV), lambda g: (g, 0)),
            cache_spec, cache_spec,
            full((1, D_SGU)), full((1, D_SGU)),
            full((N_SGU_GROUPS, t, t)), full((t, N_SGU_GROUPS)),
        ],
        out_specs=[pl.BlockSpec((t, D_MODEL), lambda g: (g, 0)),
                   pl.BlockSpec((t, D_SGU), lambda g: (g, 0))],
        out_shape=[jax.ShapeDtypeStruct((ROWS_S, D_MODEL), bf16),
                   jax.ShapeDtypeStruct((ROWS_S, D_SGU), f32)],
        compiler_params=_params(1),
        name="mix_sample",
    )(sinks, proj, proj, proj, proj, proj, kv, cache_k, cache_v, gain, bias, w_s, b_s)


def _post_norm(x, y, gate, g_ref, b_ref):
    t = x + (gate * (1.0 / ALPHA)) * y.astype(f32)
    mu = jnp.mean(t, axis=1, keepdims=True)
    d = t - mu
    var = jnp.mean(d * d, axis=1, keepdims=True)
    return d * lax.rsqrt(var + LN_EPS / ALPHA ** 2) * g_ref[...] + b_ref[...]


def _ln_kernel(*refs, n):
    xp_ref, xs_ref = refs[0:2]
    y_refs = refs[2:2 + 2 * n]
    gate_refs = refs[2 + 2 * n:2 + 3 * n]
    gb_refs = refs[2 + 3 * n:2 + 5 * n]
    rest = refs[2 + 5 * n:]
    i = pl.program_id(0)

    def stream(x, sample):
        for l in range(n):
            if sample:
                gate = gate_refs[l][0, 0:ROWS_S, :]
            else:
                gate = _prompt_mod_row(gate_refs[l], 0, i, SEQ // LN_TM)
            x = _post_norm(x, y_refs[2 * l + sample][...], gate, gb_refs[2 * l], gb_refs[2 * l + 1])
        return x

    if n == DEPTH:
        xp_out, xs_out = rest

        @pl.when(i < LN_BLOCKS)
        def _():
            xp_out[...] = stream(xp_ref[...], 0)

        @pl.when(i == LN_BLOCKS)
        def _():
            xs_out[...] = stream(xs_ref[...], 1)
    else:
        nxt_ref, w_ref, hp_out, hs_out, kvp_out, kvs_out, w_bf = rest

        @pl.when(i == 0)
        def _():
            w_bf[...] = w_ref[...].astype(bf16)

        @pl.when(i < LN_BLOCKS)
        def _():
            shift = _prompt_mod_row(nxt_ref, 0, i, SEQ // LN_TM)
            scale = _prompt_mod_row(nxt_ref, 1, i, SEQ // LN_TM)
            h = (stream(xp_ref[...], 0) * (1.0 + scale) + shift).astype(bf16)
            hp_out[...] = h
            kvp_out[...] = jnp.dot(h, w_bf[...], preferred_element_type=f32)

        @pl.when(i == LN_BLOCKS)
        def _():
            h = (stream(xs_ref[...], 1) * (1.0 + nxt_ref[1, 0:ROWS_S, :]) + nxt_ref[0, 0:ROWS_S, :]).astype(bf16)
            hs_out[...] = h
            kvs_out[...] = jnp.dot(h, w_bf[...], preferred_element_type=f32)


def _stream(xp, xs, ys, mod, ln_gain, ln_bias, w_in):
    n = len(ys)
    in_specs = [_LN_ROW_SPEC, _LN_SAMPLE_SPEC] + [_LN_ROW_SPEC, _LN_SAMPLE_SPEC] * n
    in_specs += [pl.BlockSpec((None, 1, MOD_ROWS, D_MODEL), functools.partial(lambda l, i: (l, 2, 0, 0), l))
                 for l in range(n)]
    in_specs += [pl.BlockSpec((None, 1, D_MODEL), functools.partial(lambda l, i: (l, 0, 0), l))
                 for l in range(n) for _ in range(2)]
    args = [xp, xs] + [y for pair in ys for y in pair] + [mod] * n
    for l in range(n):
        args += [ln_gain.reshape(DEPTH, 1, D_MODEL), ln_bias.reshape(DEPTH, 1, D_MODEL)]
    if n == DEPTH:
        out_specs = [_LN_ROW_SPEC, _LN_SAMPLE_SPEC]
        out_shape = [jax.ShapeDtypeStruct((ROWS_P, D_MODEL), f32), jax.ShapeDtypeStruct((ROWS_S, D_MODEL), f32)]
        scratch = []
    else:
        in_specs += [pl.BlockSpec((None, 2, MOD_ROWS, D_MODEL), lambda i: (n, 0, 0, 0)),
                     pl.BlockSpec((None, D_MODEL, TN), lambda i: (n, 0, KV_TILE), pipeline_mode=pl.Buffered(1))]
        args += [mod, w_in]
        out_specs = [_LN_ROW_SPEC, _LN_SAMPLE_SPEC, pl.BlockSpec((LN_TM, 2 * D_KV), _ln_row),
                     pl.BlockSpec((ROWS_S, 2 * D_KV), lambda i: (0, 0))]
        out_shape = [jax.ShapeDtypeStruct((ROWS_P, D_MODEL), bf16), jax.ShapeDtypeStruct((ROWS_S, D_MODEL), bf16),
                     jax.ShapeDtypeStruct((ROWS_P, 2 * D_KV), f32), jax.ShapeDtypeStruct((ROWS_S, 2 * D_KV), f32)]
        scratch = [pltpu.VMEM((D_MODEL, TN), bf16)]
    return pl.pallas_call(
        functools.partial(_ln_kernel, n=n),
        grid=(LN_BLOCKS + 1,),
        in_specs=in_specs,
        out_specs=out_specs,
        out_shape=out_shape,
        scratch_shapes=scratch,
        compiler_params=_params(1),
        name=f"stream{n}",
    )(*args)


def kernel(x_prompt, x_sample, cache_k, cache_v, c_prompt, c_sample, w_ada, b_ada, w_in, attn_sinks,
           sgu_ln_gain, sgu_ln_bias, sgu_w_s, sgu_b_s, w_pa, w_pb, w_o, ln_gain, ln_bias):
    xp = x_prompt.reshape(ROWS_P, D_MODEL)
    xs = x_sample.reshape(ROWS_S, D_MODEL)
    c_all = jnp.concatenate([jnp.repeat(c_sample, DEC_SEQ, axis=0), c_prompt,
                             jnp.zeros((MOD_ROWS - ROWS_S - BATCH, D_MODEL), f32)], axis=0)
    mod = _ada(c_all, w_ada, b_ada)
    ck = cache_k.reshape(DEPTH * DEC_BATCH, WINDOW, D_KV)
    cv = cache_v.reshape(DEPTH * DEC_BATCH, WINDOW, D_KV)

    win_k, win_v, new_k, new_v, sgu_v, ys = [], [], [], [], [], []
    for l in range(DEPTH):
        hp, hs, kv_p, kv_s = _stream(xp, xs, ys, mod, ln_gain, ln_bias, w_in)
        proj_p, proj_s = _matmul(hp, hs, w_in, l, PROJ_COLS // TN, _skip_kv_tile, bf16, "proj")
        gain = sgu_ln_gain[l].reshape(1, D_SGU)
        bias = sgu_ln_bias[l].reshape(1, D_SGU)
        w_small = jnp.tile(sgu_w_s[l][:, :DEC_SEQ, :DEC_SEQ], (1, SEQ_GROUP, SEQ_GROUP))
        b_small = jnp.tile(sgu_b_s[l][:, :DEC_SEQ].T, (SEQ_GROUP, 1))
        ab_s, vn_s = _mix_sample(proj_s, kv_s, ck, cv, l, attn_sinks[l], gain, bias, w_small, b_small)
        m_p, m_s = _mix_merge(proj_p, proj_s, kv_p, ab_s, attn_sinks[l], gain, bias, sgu_w_s[l], sgu_b_s[l].T,
                              w_pa, w_pb, l)
        ys.append(_matmul(m_p, m_s, w_o, l, D_MODEL // TN, lambda j: j, bf16, "out"))

        kv_win = kv_p.reshape(BATCH, SEQ, 2 * D_KV)[:, SEQ - WINDOW:]
        kv_win = kv_win.reshape(BATCH, WINDOW, 2, N_KV_HEADS, HEAD_DIM)
        kv_new = kv_s.reshape(DEC_BATCH, DEC_SEQ, 2, N_KV_HEADS, HEAD_DIM)
        win_k.append(kv_win[:, :, 0])
        win_v.append(kv_win[:, :, 1])
        new_k.append(kv_new[:, :, 0])
        new_v.append(kv_new[:, :, 1])
        sgu_v.append(vn_s.reshape(DEC_BATCH, DEC_SEQ, D_SGU))

    xp, xs = _stream(xp, xs, ys, mod, ln_gain, ln_bias, w_in)
    return (xp.reshape(BATCH, SEQ, D_MODEL), xs.reshape(DEC_BATCH, DEC_SEQ, D_MODEL),
            jnp.stack(win_k), jnp.stack(win_v), jnp.stack(new_k), jnp.stack(new_v), jnp.stack(sgu_v))
```

```python
import functools

import jax
import jax.numpy as jnp
from jax import lax
from jax.experimental import pallas as pl
from jax.experimental.pallas import tpu as pltpu

D_MODEL = 4096
BATCH = 4
SEQ = 2048
DEPTH = 2
DEC_BATCH = 32
DEC_SEQ = 4
HEAD_DIM = 64
D_ATTN = D_MODEL // 2
N_Q_HEADS = D_ATTN // HEAD_DIM
N_KV_HEADS = N_Q_HEADS // 8
GQA_GROUP = N_Q_HEADS // N_KV_HEADS
D_KV = N_KV_HEADS * HEAD_DIM
WINDOW = 128
D_SGU = D_MODEL // 2
N_SGU_GROUPS = 8
SGU_GROUP_DIM = D_SGU // N_SGU_GROUPS
CHUNK = 128
ALPHA = (2 * DEPTH) ** 0.25
LN_EPS = 1e-5
IN_COLS = D_ATTN + 2 * D_KV + D_ATTN + 3 * D_SGU + 2 * D_MODEL

ROWS_P = BATCH * SEQ
ROWS_S = DEC_BATCH * DEC_SEQ
MOD_ROWS = ROWS_S + 8
PROJ_COLS = IN_COLS - 2 * D_KV

TN = 512
TM = 1024
MM_TM = 2048
ADA_TN = 1024
KV_TILE = D_ATTN // TN
LN_TM = 256
SEQ_GROUP = 8
NEG = -1e30
LOG2E = 1.4426950408889634

VMEM_LIMIT = 56 * 1024 * 1024

bf16 = jnp.bfloat16
f32 = jnp.float32


def _params(n_axes, vmem=VMEM_LIMIT):
    return pltpu.CompilerParams(dimension_semantics=("arbitrary",) * n_axes, vmem_limit_bytes=vmem)


def _sigmoid(x):
    return 0.5 + 0.5 * jnp.tanh(0.5 * x)


def _silu(x):
    half = 0.5 * x
    return half + half * jnp.tanh(half)


def _div_pow2(x, n):
    assert n & (n - 1) == 0
    return x >> (n.bit_length() - 1)


def _mod_pow2(x, n):
    assert n & (n - 1) == 0
    return x & (n - 1)


def _ada_kernel(c_ref, w_ref, b_ref, o_ref):
    a = _silu(c_ref[...]).astype(bf16)
    o_ref[...] = jnp.dot(a, w_ref[...].astype(bf16), preferred_element_type=f32) + b_ref[...]


def _ada(c_all, w_ada, b_ada):
    tiles_per_part = D_MODEL // ADA_TN
    return pl.pallas_call(
        _ada_kernel,
        grid=(DEPTH, 3 * tiles_per_part),
        in_specs=[
            pl.BlockSpec((MOD_ROWS, D_MODEL), lambda l, j: (0, 0)),
            pl.BlockSpec((None, D_MODEL, ADA_TN), lambda l, j: (l, 0, j)),
            pl.BlockSpec((None, 1, ADA_TN), lambda l, j: (l, 0, j)),
        ],
        out_specs=pl.BlockSpec((None, None, MOD_ROWS, ADA_TN),
                               lambda l, j: (l, j // tiles_per_part, 0, j % tiles_per_part)),
        out_shape=jax.ShapeDtypeStruct((DEPTH, 3, MOD_ROWS, D_MODEL), f32),
        compiler_params=_params(2),
        name="ada",
    )(c_all, w_ada, b_ada.reshape(DEPTH, 1, 3 * D_MODEL))


def _prompt_mod_row(ref, part, blk, blocks_per_batch):
    return ref[part, pl.ds(ROWS_S + blk // blocks_per_batch, 1), :]


LN_BLOCKS = ROWS_P // LN_TM


def _ln_row(i):
    return (jnp.minimum(i, LN_BLOCKS - 1), 0)


_LN_ROW_SPEC = pl.BlockSpec((LN_TM, D_MODEL), _ln_row)
_LN_SAMPLE_SPEC = pl.BlockSpec((ROWS_S, D_MODEL), lambda i: (0, 0))


def _matmul_kernel(xp_ref, xs_ref, w_ref, op_ref, os_ref, w_bf):
    @pl.when(pl.program_id(1) == 0)
    def _():
        w_bf[...] = w_ref[...].astype(bf16)
        os_ref[...] = jnp.dot(xs_ref[...], w_bf[...], preferred_element_type=f32).astype(os_ref.dtype)

    op_ref[...] = jnp.dot(xp_ref[...], w_bf[...], preferred_element_type=f32).astype(op_ref.dtype)


def _matmul(xp, xs, w, layer, n_tiles, w_tile_of, out_dtype, name):
    k = xp.shape[1]
    out_bytes = jnp.dtype(out_dtype).itemsize
    vmem = (2 * (MM_TM + ROWS_S) * k * 2 + 2 * k * TN * 4 + k * TN * 2 + 2 * (MM_TM + ROWS_S) * TN * out_bytes
            + (4 << 20))
    return pl.pallas_call(
        _matmul_kernel,
        grid=(n_tiles, ROWS_P // MM_TM),
        in_specs=[pl.BlockSpec((MM_TM, k), lambda j, i: (i, 0)),
                  pl.BlockSpec((ROWS_S, k), lambda j, i: (0, 0)),
                  pl.BlockSpec((None, k, TN), lambda j, i: (layer, 0, w_tile_of(j)))],
        out_specs=[pl.BlockSpec((MM_TM, TN), lambda j, i: (i, j)),
                   pl.BlockSpec((ROWS_S, TN), lambda j, i: (0, j))],
        out_shape=[jax.ShapeDtypeStruct((ROWS_P, n_tiles * TN), out_dtype),
                   jax.ShapeDtypeStruct((ROWS_S, n_tiles * TN), out_dtype)],
        scratch_shapes=[pltpu.VMEM((k, TN), bf16)],
        compiler_params=_params(2, vmem),
        name=name,
    )(xp, xs, w)


def _skip_kv_tile(c):
    return c + (c >= KV_TILE).astype(jnp.int32)


GA_COL = 5 * D_ATTN
GB_COL = GA_COL + D_MODEL


def _attention_stages(q_ref, k_all, v_all, mask, slot, slot_cols, live_key, sink_ref, za_ref, ab_ref, t, lk):
    lane = lax.broadcasted_iota(jnp.int32, (lk, 2 * HEAD_DIM), 1)
    low, high = lane < HEAD_DIM, lane >= HEAD_DIM
    ones2 = jnp.concatenate([low, high], axis=0).astype(f32).astype(bf16)
    nt = (((1,), (1,)), ((), ()))
    pairs = GQA_GROUP // 2
    k_scale = HEAD_DIM ** -0.5 * LOG2E

    def head(kv):
        cols = slice((kv // 2) * 2 * HEAD_DIM, (kv // 2 + 1) * 2 * HEAD_DIM)
        kp, vp = k_all[:, cols] * k_scale, v_all[:, cols]
        keep = (low if kv % 2 == 0 else high) & live_key
        k_own = jnp.where(keep, kp, 0.0)
        v_own = jnp.where(keep, vp, 0.0)
        k_oth = pltpu.roll(k_own, HEAD_DIM, 1)
        v_oth = pltpu.roll(v_own, HEAD_DIM, 1)
        if kv % 2 == 0:
            k2 = jnp.concatenate([k_own, k_oth], axis=0).astype(bf16)
            v2 = jnp.concatenate([v_own, v_oth], axis=0).astype(bf16)
        else:
            k2 = jnp.concatenate([k_oth, k_own], axis=0).astype(bf16)
            v2 = jnp.concatenate([v_oth, v_own], axis=0).astype(bf16)
        v2 = jnp.concatenate([v2, ones2], axis=1)
        base = kv * GQA_GROUP * HEAD_DIM
        q4 = jnp.concatenate(
            [q_ref[:, base + p * 128: base + (p + 1) * 128] for p in range(pairs)], axis=0)
        s_all = lax.dot_general(q4, k2, nt, preferred_element_type=f32)
        rows = []
        for p in range(pairs):
            halves = []
            for h in range(2):
                sink = sink_ref[kv * GQA_GROUP + 2 * p + h] * LOG2E
                fill = jnp.where(slot, sink, NEG)
                if slot_cols < lk:
                    fill = jnp.concatenate([fill, jnp.full((t, lk - slot_cols), NEG, f32)], axis=1)
                s = jnp.where(mask, s_all[p * t:(p + 1) * t, h * lk:(h + 1) * lk], fill)
                m = jnp.max(s, axis=1, keepdims=True)
                halves.append(jnp.exp2(s - m).astype(bf16))
            rows.append(jnp.concatenate(halves, axis=1))
        probs = jnp.concatenate(rows, axis=0)
        o = jnp.dot(probs, v2, preferred_element_type=f32)
        o = o[:, 0:128] / o[:, 128:256]
        for p in range(pairs):
            cols = slice(base + p * 128, base + (p + 1) * 128)
            ab_ref[:, cols] = (o[p * t:(p + 1) * t, :] * _silu(za_ref[:, cols].astype(f32))).astype(bf16)

    return [functools.partial(head, kv) for kv in range(N_KV_HEADS)]


def _sgu_norm(vb_ref, gain_ref, bias_ref):
    vb = vb_ref[...].astype(f32)
    mu = jnp.mean(vb, axis=1, keepdims=True)
    d = vb - mu
    var = jnp.mean(d * d, axis=1, keepdims=True)
    return d * lax.rsqrt(var + LN_EPS) * gain_ref[...] + bias_ref[...]


def _sgu_stages(u_ref, zb_ref, ws_ref, bs_ref, vn, w_mask, ab_ref):
    vn_bf = vn.astype(bf16)

    def group(g):
        cols = slice(g * SGU_GROUP_DIM, (g + 1) * SGU_GROUP_DIM)
        w_g = jnp.where(w_mask, ws_ref[g], 0.0).astype(bf16)
        s = jnp.dot(w_g, vn_bf[:, cols], preferred_element_type=f32) + bs_ref[:, g:g + 1]
        out_b = u_ref[:, cols].astype(f32) * s * _silu(zb_ref[:, cols].astype(f32))
        ab_ref[:, D_ATTN + g * SGU_GROUP_DIM: D_ATTN + (g + 1) * SGU_GROUP_DIM] = out_b.astype(bf16)

    return [functools.partial(group, g) for g in range(N_SGU_GROUPS)]


def _interleave(attention, sgu, others=()):
    per_head = len(sgu) // len(attention)
    others = list(others)
    if others:
        others.pop(0)()
    for kv, head in enumerate(attention):
        head()
        if others:
            others.pop(0)()
        for stage in sgu[kv * per_head:(kv + 1) * per_head]:
            stage()


def _mix_prompt_block(n, sink_ref, q_ref, za_ref, u_ref, vb_ref, zb_ref, kvc_ref, kvp_ref,
                      gain_ref, bias_ref, ws_ref, bs_ref, ab_ref, others=()):
    vn = _sgu_norm(vb_ref, gain_ref, bias_ref)
    kvc, kvp = kvc_ref[...], kvp_ref[...]
    k_all = jnp.concatenate([kvp[:, 0:D_KV], kvc[:, 0:D_KV]], axis=0)
    v_all = jnp.concatenate([kvp[:, D_KV:], kvc[:, D_KV:]], axis=0)
    qi = lax.broadcasted_iota(jnp.int32, (WINDOW, 2 * WINDOW), 0)
    kj = lax.broadcasted_iota(jnp.int32, (WINDOW, 2 * WINDOW), 1)
    no_prev = jnp.where(n > 0, 0, 2 * WINDOW)
    mask = ((kj < WINDOW) & (kj > qi + no_prev)) | ((kj >= WINDOW) & (kj - WINDOW <= qi))
    slot = lax.broadcasted_iota(jnp.int32, (WINDOW, WINDOW), 1) == 0
    live_key = lax.broadcasted_iota(jnp.int32, (2 * WINDOW, 2 * HEAD_DIM), 0) != 0
    heads = _attention_stages(q_ref, k_all, v_all, mask, slot, WINDOW, live_key, sink_ref, za_ref, ab_ref,
                              WINDOW, 2 * WINDOW)
    r = lax.broadcasted_iota(jnp.int32, (CHUNK, CHUNK), 0)
    c = lax.broadcasted_iota(jnp.int32, (CHUNK, CHUNK), 1)
    _interleave(heads, _sgu_stages(u_ref, zb_ref, ws_ref, bs_ref, vn, c <= r, ab_ref), others)


ROW_TILES = ROWS_P // TM
TILE_BLOCKS = TM // WINDOW
assert TILE_BLOCKS == D_MODEL // TN


def _mix_merge_kernel(sink_ref, q_ref, za_ref, u_ref, vb_ref, zb_ref, kvc_ref, kvp_ref,
                      gain_ref, bias_ref, ws_ref, bs_ref,
                      abs_ref, gap_ref, gbp_ref, gas_ref, gbs_ref, wa_ref, wb_ref,
                      op_ref, os_ref, ab_even, ab_odd, wa_bf, wb_bf):
    r, c = pl.program_id(0), pl.program_id(1)
    blk = jnp.minimum(r, ROW_TILES - 1) * TILE_BLOCKS + c

    def cast_weights():
        wa_bf[...] = wa_ref[...].astype(bf16)
        wb_bf[...] = wb_ref[...].astype(bf16)

    def mix(ab_tile, others=()):
        mix_out = ab_tile.at[pl.ds(pl.multiple_of(c * WINDOW, WINDOW), WINDOW), :]
        _mix_prompt_block(_mod_pow2(blk, SEQ // WINDOW), sink_ref, q_ref, za_ref, u_ref, vb_ref, zb_ref,
                          kvc_ref, kvp_ref, gain_ref, bias_ref, ws_ref, bs_ref, mix_out, others)

    def merged(ab_ref, ga_ref, gb_ref):
        ya = jnp.dot(ab_ref[:, 0:D_ATTN], wa_bf[...], preferred_element_type=f32)
        yb = jnp.dot(ab_ref[:, D_ATTN:], wb_bf[...], preferred_element_type=f32)
        return (_sigmoid(ga_ref[...].astype(f32)) * ya + _sigmoid(gb_ref[...].astype(f32)) * yb).astype(bf16)

    def mix_and_project(mix_tile, done_tile):
        cast_weights()
        half = TN // 2
        parts = {}

        def piece(lhs_cols, w_bf, key, cols):
            parts[key] = jnp.dot(done_tile[:, lhs_cols], w_bf[:, cols], preferred_element_type=f32)

        pieces = [functools.partial(piece, lhs_cols, w_bf, (name, hc), slice(hc * half, (hc + 1) * half))
                  for name, lhs_cols, w_bf in (("a", slice(0, D_ATTN), wa_bf), ("b", slice(D_ATTN, D_MODEL), wb_bf))
                  for hc in range(2)]
        mix(mix_tile, pieces)
        for hc in range(2):
            cols = slice(hc * half, (hc + 1) * half)
            op_ref[:, cols] = (_sigmoid(gap_ref[:, cols].astype(f32)) * parts["a", hc]
                               + _sigmoid(gbp_ref[:, cols].astype(f32)) * parts["b", hc]).astype(bf16)

    @pl.when(r == 0)
    def _():
        cast_weights()

        def project_sample():
            os_ref[...] = merged(abs_ref, gas_ref, gbs_ref)

        mix(ab_even, [project_sample])

    @pl.when((r > 0) & (r % 2 == 1))
    def _():
        mix_and_project(ab_odd, ab_even)

    @pl.when((r > 0) & (r % 2 == 0))
    def _():
        mix_and_project(ab_even, ab_odd)


def _mix_merge(proj_p, proj_s, kv_p, ab_s, sinks, gain, bias, w_s, b_s, w_pa, w_pb, layer):
    seq_blocks = SEQ // WINDOW
    ga_tile, gb_tile = GA_COL // TN, GB_COL // TN
    blk_of = lambda r, c: jnp.minimum(r, ROW_TILES - 1) * TILE_BLOCKS + c
    prev_of = lambda r, c: blk_of(r, c) - (_mod_pow2(blk_of(r, c), seq_blocks) > 0).astype(jnp.int32)
    tile_of = lambda r: jnp.maximum(r - 1, 0)
    col_of = lambda r, c: jnp.where(r == 0, 0, c)
    scol_of = lambda r, c: jnp.where(r == 0, c, TILE_BLOCKS - 1)
    full = lambda shape: pl.BlockSpec(shape, lambda r, c: (0,) * len(shape))
    return pl.pallas_call(
        _mix_merge_kernel,
        grid=(ROW_TILES + 1, TILE_BLOCKS),
        in_specs=[pl.BlockSpec(memory_space=pltpu.SMEM)] + _proj_specs(WINDOW, blk_of) + [
            pl.BlockSpec((WINDOW, 2 * D_KV), lambda r, c: (blk_of(r, c), 0)),
            pl.BlockSpec((WINDOW, 2 * D_KV), lambda r, c: (prev_of(r, c), 0)),
            full((1, D_SGU)), full((1, D_SGU)),
            full((N_SGU_GROUPS, CHUNK, CHUNK)), full((CHUNK, N_SGU_GROUPS)),
            full((ROWS_S, D_MODEL)),
            pl.BlockSpec((TM, TN), lambda r, c: (tile_of(r), ga_tile + col_of(r, c))),
            pl.BlockSpec((TM, TN), lambda r, c: (tile_of(r), gb_tile + col_of(r, c))),
            pl.BlockSpec((ROWS_S, TN), lambda r, c: (0, ga_tile + scol_of(r, c))),
            pl.BlockSpec((ROWS_S, TN), lambda r, c: (0, gb_tile + scol_of(r, c))),
            pl.BlockSpec((None, D_ATTN, TN), lambda r, c: (layer, 0, c)),
            pl.BlockSpec((None, D_SGU, TN), lambda r, c: (layer, 0, c)),
        ],
        out_specs=[pl.BlockSpec((TM, TN), lambda r, c: (tile_of(r), col_of(r, c))),
                   pl.BlockSpec((ROWS_S, TN), lambda r, c: (0, scol_of(r, c)))],
        out_shape=[jax.ShapeDtypeStruct((ROWS_P, D_MODEL), bf16), jax.ShapeDtypeStruct((ROWS_S, D_MODEL), bf16)],
        scratch_shapes=[pltpu.VMEM((TM, D_MODEL), bf16), pltpu.VMEM((TM, D_MODEL), bf16),
                        pltpu.VMEM((D_ATTN, TN), bf16), pltpu.VMEM((D_SGU, TN), bf16)],
        compiler_params=_params(2),
        name="mix_merge",
    )(sinks, proj_p, proj_p, proj_p, proj_p, proj_p, kv_p, kv_p, gain, bias, w_s, b_s,
      ab_s, proj_p, proj_p, proj_s, proj_s, w_pa, w_pb)


def _mix_sample_kernel(sink_ref, q_ref, za_ref, u_ref, vb_ref, zb_ref, kvn_ref, ck_ref, cv_ref,
                       gain_ref, bias_ref, ws_ref, bs_ref, ab_ref, vn_ref):
    t = SEQ_GROUP * DEC_SEQ
    n_cache = SEQ_GROUP * WINDOW
    lk = n_cache + WINDOW
    kvn = kvn_ref[...]
    pad = jnp.zeros((WINDOW - t, D_KV), f32)
    k_all = jnp.concatenate([ck_ref[...].reshape(n_cache, D_KV), kvn[:, 0:D_KV], pad], axis=0)
    v_all = jnp.concatenate([cv_ref[...].reshape(n_cache, D_KV), kvn[:, D_KV:], pad], axis=0)
    qi = lax.broadcasted_iota(jnp.int32, (t, lk), 0)
    kj = lax.broadcasted_iota(jnp.int32, (t, lk), 1)
    q_seq, q_tok = _div_pow2(qi, DEC_SEQ), _mod_pow2(qi, DEC_SEQ)
    in_cache = (_div_pow2(kj, WINDOW) == q_seq) & (_mod_pow2(kj, WINDOW) > q_tok)
    kn = jnp.maximum(kj - n_cache, 0)
    in_new = (kn < t) & (_div_pow2(kn, DEC_SEQ) == q_seq) & (_mod_pow2(kn, DEC_SEQ) <= q_tok)
    mask = ((kj < n_cache) & in_cache) | ((kj >= n_cache) & in_new)
    slot_seq = _div_pow2(lax.broadcasted_iota(jnp.int32, (t, n_cache), 0), DEC_SEQ)
    slot = lax.broadcasted_iota(jnp.int32, (t, n_cache), 1) == slot_seq * WINDOW
    key_row = lax.broadcasted_iota(jnp.int32, (lk, 2 * HEAD_DIM), 0)
    live_key = (key_row >= n_cache) | (_mod_pow2(key_row, WINDOW) != 0)
    heads = _attention_stages(q_ref, k_all, v_all, mask, slot, n_cache, live_key, sink_ref, za_ref, ab_ref, t, lk)

    vn = _sgu_norm(vb_ref, gain_ref, bias_ref)
    vn_ref[...] = vn
    r = lax.broadcasted_iota(jnp.int32, (t, t), 0)
    c = lax.broadcasted_iota(jnp.int32, (t, t), 1)
    w_mask = (_div_pow2(r, DEC_SEQ) == _div_pow2(c, DEC_SEQ)) & (c <= r)
    _interleave(heads, _sgu_stages(u_ref, zb_ref, ws_ref, bs_ref, vn, w_mask, ab_ref))


def _proj_specs(rows, row_of):
    return [pl.BlockSpec((rows, D_ATTN), functools.partial(lambda s, *g: (row_of(*g), s), s))
            for s in range(5)]


def _mix_sample(proj, kv, cache_k, cache_v, layer, sinks, gain, bias, w_s, b_s):
    t = SEQ_GROUP * DEC_SEQ
    groups = DEC_BATCH // SEQ_GROUP
    full = lambda shape: pl.BlockSpec(shape, lambda g: (0,) * len(shape))
    cache_spec = pl.BlockSpec((SEQ_GROUP, WINDOW, D_KV), lambda g: (layer * groups + g, 0, 0))
    return pl.pallas_call(
        _mix_sample_kernel,
        grid=(groups,),
        in_specs=[pl.BlockSpec(memory_space=pltpu.SMEM)] + _proj_specs(t, lambda g: g) + [
            pl.BlockSpec((t, 2 * D_KV), lambda g: (g, 0)),
            cache_spec, cache_spec,
            full((1, D_SGU)), full((1, D_SGU)),
            full((N_SGU_GROUPS, t, t)), full((t, N_SGU_GROUPS)),
        ],
        out_specs=[pl.BlockSpec((t, D_MODEL), lambda g: (g, 0)),
                   pl.BlockSpec((t, D_SGU), lambda g: (g, 0))],
        out_shape=[jax.ShapeDtypeStruct((ROWS_S, D_MODEL), bf16),
                   jax.ShapeDtypeStruct((ROWS_S, D_SGU), f32)],
        compiler_params=_params(1),
        name="mix_sample",
    )(sinks, proj, proj, proj, proj, proj, kv, cache_k, cache_v, gain, bias, w_s, b_s)


def _post_norm(x, y, gate, g_ref, b_ref):
    t = x + (gate * (1.0 / ALPHA)) * y.astype(f32)
    mu = jnp.mean(t, axis=1, keepdims=True)
    d = t - mu
    var = jnp.mean(d * d, axis=1, keepdims=True)
    return d * lax.rsqrt(var + LN_EPS / ALPHA ** 2) * g_ref[...] + b_ref[...]


def _ln_kernel(*refs, n):
    xp_ref, xs_ref = refs[0:2]
    y_refs = refs[2:2 + 2 * n]
    gate_refs = refs[2 + 2 * n:2 + 3 * n]
    gb_refs = refs[2 + 3 * n:2 + 5 * n]
    rest = refs[2 + 5 * n:]
    i = pl.program_id(0)

    def stream(x, sample):
        for l in range(n):
            if sample:
                gate = gate_refs[l][0, 0:ROWS_S, :]
            else:
                gate = _prompt_mod_row(gate_refs[l], 0, i, SEQ // LN_TM)
            x = _post_norm(x, y_refs[2 * l + sample][...], gate, gb_refs[2 * l], gb_refs[2 * l + 1])
        return x

    if n == DEPTH:
        xp_out, xs_out = rest

        @pl.when(i < LN_BLOCKS)
        def _():
            xp_out[...] = stream(xp_ref[...], 0)

        @pl.when(i == LN_BLOCKS)
        def _():
            xs_out[...] = stream(xs_ref[...], 1)
    else:
        nxt_ref, w_ref, hp_out, hs_out, kvp_out, kvs_out, w_bf = rest

        @pl.when(i == 0)
        def _():
            w_bf[...] = w_ref[...].astype(bf16)

        @pl.when(i < LN_BLOCKS)
        def _():
            shift = _prompt_mod_row(nxt_ref, 0, i, SEQ // LN_TM)
            scale = _prompt_mod_row(nxt_ref, 1, i, SEQ // LN_TM)
            h = (stream(xp_ref[...], 0) * (1.0 + scale) + shift).astype(bf16)
            hp_out[...] = h
            kvp_out[...] = jnp.dot(h, w_bf[...], preferred_element_type=f32)

        @pl.when(i == LN_BLOCKS)
        def _():
            h = (stream(xs_ref[...], 1) * (1.0 + nxt_ref[1, 0:ROWS_S, :]) + nxt_ref[0, 0:ROWS_S, :]).astype(bf16)
            hs_out[...] = h
            kvs_out[...] = jnp.dot(h, w_bf[...], preferred_element_type=f32)


def _stream(xp, xs, ys, mod, ln_gain, ln_bias, w_in):
    n = len(ys)
    in_specs = [_LN_ROW_SPEC, _LN_SAMPLE_SPEC] + [_LN_ROW_SPEC, _LN_SAMPLE_SPEC] * n
    in_specs += [pl.BlockSpec((None, 1, MOD_ROWS, D_MODEL), functools.partial(lambda l, i: (l, 2, 0, 0), l))
                 for l in range(n)]
    in_specs += [pl.BlockSpec((None, 1, D_MODEL), functools.partial(lambda l, i: (l, 0, 0), l))
                 for l in range(n) for _ in range(2)]
    args = [xp, xs] + [y for pair in ys for y in pair] + [mod] * n
    for l in range(n):
        args += [ln_gain.reshape(DEPTH, 1, D_MODEL), ln_bias.reshape(DEPTH, 1, D_MODEL)]
    if n == DEPTH:
        out_specs = [_LN_ROW_SPEC, _LN_SAMPLE_SPEC]
        out_shape = [jax.ShapeDtypeStruct((ROWS_P, D_MODEL), f32), jax.ShapeDtypeStruct((ROWS_S, D_MODEL), f32)]
        scratch = []
    else:
        in_specs += [pl.BlockSpec((None, 2, MOD_ROWS, D_MODEL), lambda i: (n, 0, 0, 0)),
                     pl.BlockSpec((None, D_MODEL, TN), lambda i: (n, 0, KV_TILE), pipeline_mode=pl.Buffered(1))]
        args += [mod, w_in]
        out_specs = [_LN_ROW_SPEC, _LN_SAMPLE_SPEC, pl.BlockSpec((LN_TM, 2 * D_KV), _ln_row),
                     pl.BlockSpec((ROWS_S, 2 * D_KV), lambda i: (0, 0))]
        out_shape = [jax.ShapeDtypeStruct((ROWS_P, D_MODEL), bf16), jax.ShapeDtypeStruct((ROWS_S, D_MODEL), bf16),
                     jax.ShapeDtypeStruct((ROWS_P, 2 * D_KV), f32), jax.ShapeDtypeStruct((ROWS_S, 2 * D_KV), f32)]
        scratch = [pltpu.VMEM((D_MODEL, TN), bf16)]
    return pl.pallas_call(
        functools.partial(_ln_kernel, n=n),
        grid=(LN_BLOCKS + 1,),
        in_specs=in_specs,
        out_specs=out_specs,
        out_shape=out_shape,
        scratch_shapes=scratch,
        compiler_params=_params(1),
        name=f"stream{n}",
    )(*args)


def kernel(x_prompt, x_sample, cache_k, cache_v, c_prompt, c_sample, w_ada, b_ada, w_in, attn_sinks,
           sgu_ln_gain, sgu_ln_bias, sgu_w_s, sgu_b_s, w_pa, w_pb, w_o, ln_gain, ln_bias):
    xp = x_prompt.reshape(ROWS_P, D_MODEL)
    xs = x_sample.reshape(ROWS_S, D_MODEL)
    c_all = jnp.concatenate([jnp.repeat(c_sample, DEC_SEQ, axis=0), c_prompt,
                             jnp.zeros((MOD_ROWS - ROWS_S - BATCH, D_MODEL), f32)], axis=0)
    mod = _ada(c_all, w_ada, b_ada)
    ck = cache_k.reshape(DEPTH * DEC_BATCH, WINDOW, D_KV)
    cv = cache_v.reshape(DEPTH * DEC_BATCH, WINDOW, D_KV)

    win_k, win_v, new_k, new_v, sgu_v, ys = [], [], [], [], [], []
    for l in range(DEPTH):
        hp, hs, kv_p, kv_s = _stream(xp, xs, ys, mod, ln_gain, ln_bias, w_in)
        proj_p, proj_s = _matmul(hp, hs, w_in, l, PROJ_COLS // TN, _skip_kv_tile, bf16, "proj")
        gain = sgu_ln_gain[l].reshape(1, D_SGU)
        bias = sgu_ln_bias[l].reshape(1, D_SGU)
        w_small = jnp.tile(sgu_w_s[l][:, :DEC_SEQ, :DEC_SEQ], (1, SEQ_GROUP, SEQ_GROUP))
        b_small = jnp.tile(sgu_b_s[l][:, :DEC_SEQ].T, (SEQ_GROUP, 1))
        ab_s, vn_s = _mix_sample(proj_s, kv_s, ck, cv, l, attn_sinks[l], gain, bias, w_small, b_small)
        m_p, m_s = _mix_merge(proj_p, proj_s, kv_p, ab_s, attn_sinks[l], gain, bias, sgu_w_s[l], sgu_b_s[l].T,
                              w_pa, w_pb, l)
        ys.append(_matmul(m_p, m_s, w_o, l, D_MODEL // TN, lambda j: j, bf16, "out"))

        kv_win = kv_p.reshape(BATCH, SEQ, 2 * D_KV)[:, SEQ - WINDOW:]
        kv_win = kv_win.reshape(BATCH, WINDOW, 2, N_KV_HEADS, HEAD_DIM)
        kv_new = kv_s.reshape(DEC_BATCH, DEC_SEQ, 2, N_KV_HEADS, HEAD_DIM)
        win_k.append(kv_win[:, :, 0])
        win_v.append(kv_win[:, :, 1])
        new_k.append(kv_new[:, :, 0])
        new_v.append(kv_new[:, :, 1])
        sgu_v.append(vn_s.reshape(DEC_BATCH, DEC_SEQ, D_SGU))

    xp, xs = _stream(xp, xs, ys, mod, ln_gain, ln_bias, w_in)
    return (xp.reshape(BATCH, SEQ, D_MODEL), xs.reshape(DEC_BATCH, DEC_SEQ, D_MODEL),
            jnp.stack(win_k), jnp.stack(win_v), jnp.stack(new_k), jnp.stack(new_v), jnp.stack(sgu_v))
```

```python
import functools

import jax
import jax.numpy as jnp
from jax import lax
from jax.experimental import pallas as pl
from jax.experimental.pallas import tpu as pltpu

D_MODEL = 4096
BATCH = 4
SEQ = 2048
DEPTH = 2
DEC_BATCH = 32
DEC_SEQ = 4
HEAD_DIM = 64
D_ATTN = D_MODEL // 2
N_Q_HEADS = D_ATTN // HEAD_DIM
N_KV_HEADS = N_Q_HEADS // 8
GQA_GROUP = N_Q_HEADS // N_KV_HEADS
D_KV = N_KV_HEADS * HEAD_DIM
WINDOW = 128
D_SGU = D_MODEL // 2
N_SGU_GROUPS = 8
SGU_GROUP_DIM = D_SGU // N_SGU_GROUPS
CHUNK = 128
ALPHA = (2 * DEPTH) ** 0.25
LN_EPS = 1e-5
IN_COLS = D_ATTN + 2 * D_KV + D_ATTN + 3 * D_SGU + 2 * D_MODEL

ROWS_P = BATCH * SEQ
ROWS_S = DEC_BATCH * DEC_SEQ
MOD_ROWS = ROWS_S + 8
PROJ_COLS = IN_COLS - 2 * D_KV

TN = 512
TM = 1024
MM_TM = 2048
ADA_TN = 1024
KV_TILE = D_ATTN // TN
LN_TM = 256
SEQ_GROUP = 8
NEG = -1e30
LOG2E = 1.4426950408889634

VMEM_LIMIT = 56 * 1024 * 1024

bf16 = jnp.bfloat16
f32 = jnp.float32


def _params(n_axes, vmem=VMEM_LIMIT):
    return pltpu.CompilerParams(dimension_semantics=("arbitrary",) * n_axes, vmem_limit_bytes=vmem)


def _sigmoid(x):
    return 0.5 + 0.5 * jnp.tanh(0.5 * x)


def _silu(x):
    half = 0.5 * x
    return half + half * jnp.tanh(half)


def _div_pow2(x, n):
    assert n & (n - 1) == 0
    return x >> (n.bit_length() - 1)


def _mod_pow2(x, n):
    assert n & (n - 1) == 0
    return x & (n - 1)


def _ada_kernel(c_ref, w_ref, b_ref, o_ref):
    a = _silu(c_ref[...]).astype(bf16)
    o_ref[...] = jnp.dot(a, w_ref[...].astype(bf16), preferred_element_type=f32) + b_ref[...]


def _ada(c_all, w_ada, b_ada):
    tiles_per_part = D_MODEL // ADA_TN
    return pl.pallas_call(
        _ada_kernel,
        grid=(DEPTH, 3 * tiles_per_part),
        in_specs=[
            pl.BlockSpec((MOD_ROWS, D_MODEL), lambda l, j: (0, 0)),
            pl.BlockSpec((None, D_MODEL, ADA_TN), lambda l, j: (l, 0, j)),
            pl.BlockSpec((None, 1, ADA_TN), lambda l, j: (l, 0, j)),
        ],
        out_specs=pl.BlockSpec((None, None, MOD_ROWS, ADA_TN),
                               lambda l, j: (l, j // tiles_per_part, 0, j % tiles_per_part)),
        out_shape=jax.ShapeDtypeStruct((DEPTH, 3, MOD_ROWS, D_MODEL), f32),
        compiler_params=_params(2),
        name="ada",
    )(c_all, w_ada, b_ada.reshape(DEPTH, 1, 3 * D_MODEL))


def _prompt_mod_row(ref, part, blk, blocks_per_batch):
    return ref[part, pl.ds(ROWS_S + blk // blocks_per_batch, 1), :]


LN_BLOCKS = ROWS_P // LN_TM


def _ln_row(i):
    return (jnp.minimum(i, LN_BLOCKS - 1), 0)


_LN_ROW_SPEC = pl.BlockSpec((LN_TM, D_MODEL), _ln_row)
_LN_SAMPLE_SPEC = pl.BlockSpec((ROWS_S, D_MODEL), lambda i: (0, 0))


def _matmul_kernel(xp_ref, xs_ref, w_ref, op_ref, os_ref, w_bf):
    @pl.when(pl.program_id(1) == 0)
    def _():
        w_bf[...] = w_ref[...].astype(bf16)
        os_ref[...] = jnp.dot(xs_ref[...], w_bf[...], preferred_element_type=f32).astype(os_ref.dtype)

    op_ref[...] = jnp.dot(xp_ref[...], w_bf[...], preferred_element_type=f32).astype(op_ref.dtype)


def _matmul(xp, xs, w, layer, n_tiles, w_tile_of, out_dtype, name):
    k = xp.shape[1]
    out_bytes = jnp.dtype(out_dtype).itemsize
    vmem = (2 * (MM_TM + ROWS_S) * k * 2 + 2 * k * TN * 4 + k * TN * 2 + 2 * (MM_TM + ROWS_S) * TN * out_bytes
            + (4 << 20))
    return pl.pallas_call(
        _matmul_kernel,
        grid=(n_tiles, ROWS_P // MM_TM),
        in_specs=[pl.BlockSpec((MM_TM, k), lambda j, i: (i, 0)),
                  pl.BlockSpec((ROWS_S, k), lambda j, i: (0, 0)),
                  pl.BlockSpec((None, k, TN), lambda j, i: (layer, 0, w_tile_of(j)))],
        out_specs=[pl.BlockSpec((MM_TM, TN), lambda j, i: (i, j)),
                   pl.BlockSpec((ROWS_S, TN), lambda j, i: (0, j))],
        out_shape=[jax.ShapeDtypeStruct((ROWS_P, n_tiles * TN), out_dtype),
                   jax.ShapeDtypeStruct((ROWS_S, n_tiles * TN), out_dtype)],
        scratch_shapes=[pltpu.VMEM((k, TN), bf16)],
        compiler_params=_params(2, vmem),
        name=name,
    )(xp, xs, w)


def _skip_kv_tile(c):
    return c + (c >= KV_TILE).astype(jnp.int32)


GA_COL = 5 * D_ATTN
GB_COL = GA_COL + D_MODEL


def _attention_stages(q_ref, k_all, v_all, mask, slot, slot_cols, live_key, sink_ref, za_ref, ab_ref, t, lk):
    lane = lax.broadcasted_iota(jnp.int32, (lk, 2 * HEAD_DIM), 1)
    low, high = lane < HEAD_DIM, lane >= HEAD_DIM
    ones2 = jnp.concatenate([low, high], axis=0).astype(f32).astype(bf16)
    nt = (((1,), (1,)), ((), ()))
    pairs = GQA_GROUP // 2
    k_scale = HEAD_DIM ** -0.5 * LOG2E

    def head(kv):
        cols = slice((kv // 2) * 2 * HEAD_DIM, (kv // 2 + 1) * 2 * HEAD_DIM)
        kp, vp = k_all[:, cols] * k_scale, v_all[:, cols]
        keep = (low if kv % 2 == 0 else high) & live_key
        k_own = jnp.where(keep, kp, 0.0)
        v_own = jnp.where(keep, vp, 0.0)
        k_oth = pltpu.roll(k_own, HEAD_DIM, 1)
        v_oth = pltpu.roll(v_own, HEAD_DIM, 1)
        if kv % 2 == 0:
            k2 = jnp.concatenate([k_own, k_oth], axis=0).astype(bf16)
            v2 = jnp.concatenate([v_own, v_oth], axis=0).astype(bf16)
        else:
            k2 = jnp.concatenate([k_oth, k_own], axis=0).astype(bf16)
            v2 = jnp.concatenate([v_oth, v_own], axis=0).astype(bf16)
        v2 = jnp.concatenate([v2, ones2], axis=1)
        base = kv * GQA_GROUP * HEAD_DIM
        q4 = jnp.concatenate(
            [q_ref[:, base + p * 128: base + (p + 1) * 128] for p in range(pairs)], axis=0)
        s_all = lax.dot_general(q4, k2, nt, preferred_element_type=f32)
        rows = []
        for p in range(pairs):
            halves = []
            for h in range(2):
                sink = sink_ref[kv * GQA_GROUP + 2 * p + h] * LOG2E
                fill = jnp.where(slot, sink, NEG)
                if slot_cols < lk:
                    fill = jnp.concatenate([fill, jnp.full((t, lk - slot_cols), NEG, f32)], axis=1)
                s = jnp.where(mask, s_all[p * t:(p + 1) * t, h * lk:(h + 1) * lk], fill)
                m = jnp.max(s, axis=1, keepdims=True)
                halves.append(jnp.exp2(s - m).astype(bf16))
            rows.append(jnp.concatenate(halves, axis=1))
        probs = jnp.concatenate(rows, axis=0)
        o = jnp.dot(probs, v2, preferred_element_type=f32)
        o = o[:, 0:128] / o[:, 128:256]
        for p in range(pairs):
            cols = slice(base + p * 128, base + (p + 1) * 128)
            ab_ref[:, cols] = (o[p * t:(p + 1) * t, :] * _silu(za_ref[:, cols].astype(f32))).astype(bf16)

    return [functools.partial(head, kv) for kv in range(N_KV_HEADS)]


def _sgu_norm(vb_ref, gain_ref, bias_ref):
    vb = vb_ref[...].astype(f32)
    mu = jnp.mean(vb, axis=1, keepdims=True)
    d = vb - mu
    var = jnp.mean(d * d, axis=1, keepdims=True)
    return d * lax.rsqrt(var + LN_EPS) * gain_ref[...] + bias_ref[...]


def _sgu_stages(u_ref, zb_ref, ws_ref, bs_ref, vn, w_mask, ab_ref):
    vn_bf = vn.astype(bf16)

    def group(g):
        cols = slice(g * SGU_GROUP_DIM, (g + 1) * SGU_GROUP_DIM)
        w_g = jnp.where(w_mask, ws_ref[g], 0.0).astype(bf16)
        s = jnp.dot(w_g, vn_bf[:, cols], preferred_element_type=f32) + bs_ref[:, g:g + 1]
        out_b = u_ref[:, cols].astype(f32) * s * _silu(zb_ref[:, cols].astype(f32))
        ab_ref[:, D_ATTN + g * SGU_GROUP_DIM: D_ATTN + (g + 1) * SGU_GROUP_DIM] = out_b.astype(bf16)

    return [functools.partial(group, g) for g in range(N_SGU_GROUPS)]


def _interleave(attention, sgu, others=()):
    per_head = len(sgu) // len(attention)
    others = list(others)
    if others:
        others.pop(0)()
    for kv, head in enumerate(attention):
        head()
        if others:
            others.pop(0)()
        for stage in sgu[kv * per_head:(kv + 1) * per_head]:
            stage()


def _mix_prompt_block(n, sink_ref, q_ref, za_ref, u_ref, vb_ref, zb_ref, kvc_ref, kvp_ref,
                      gain_ref, bias_ref, ws_ref, bs_ref, ab_ref, others=()):
    vn = _sgu_norm(vb_ref, gain_ref, bias_ref)
    kvc, kvp = kvc_ref[...], kvp_ref[...]
    k_all = jnp.concatenate([kvp[:, 0:D_KV], kvc[:, 0:D_KV]], axis=0)
    v_all = jnp.concatenate([kvp[:, D_KV:], kvc[:, D_KV:]], axis=0)
    qi = lax.broadcasted_iota(jnp.int32, (WINDOW, 2 * WINDOW), 0)
    kj = lax.broadcasted_iota(jnp.int32, (WINDOW, 2 * WINDOW), 1)
    no_prev = jnp.where(n > 0, 0, 2 * WINDOW)
    mask = ((kj < WINDOW) & (kj > qi + no_prev)) | ((kj >= WINDOW) & (kj - WINDOW <= qi))
    slot = lax.broadcasted_iota(jnp.int32, (WINDOW, WINDOW), 1) == 0
    live_key = lax.broadcasted_iota(jnp.int32, (2 * WINDOW, 2 * HEAD_DIM), 0) != 0
    heads = _attention_stages(q_ref, k_all, v_all, mask, slot, WINDOW, live_key, sink_ref, za_ref, ab_ref,
                              WINDOW, 2 * WINDOW)
    r = lax.broadcasted_iota(jnp.int32, (CHUNK, CHUNK), 0)
    c = lax.broadcasted_iota(jnp.int32, (CHUNK, CHUNK), 1)
    _interleave(heads, _sgu_stages(u_ref, zb_ref, ws_ref, bs_ref, vn, c <= r, ab_ref), others)


ROW_TILES = ROWS_P // TM
TILE_BLOCKS = TM // WINDOW
assert TILE_BLOCKS == D_MODEL // TN


def _mix_merge_kernel(sink_ref, q_ref, za_ref, u_ref, vb_ref, zb_ref, kvc_ref, kvp_ref,
                      gain_ref, bias_ref, ws_ref, bs_ref,
                      abs_ref, gap_ref, gbp_ref, gas_ref, gbs_ref, wa_ref, wb_ref,
                      op_ref, os_ref, ab_even, ab_odd, wa_bf, wb_bf):
    r, c = pl.program_id(0), pl.program_id(1)
    blk = jnp.minimum(r, ROW_TILES - 1) * TILE_BLOCKS + c

    def cast_weights():
        wa_bf[...] = wa_ref[...].astype(bf16)
        wb_bf[...] = wb_ref[...].astype(bf16)

    def mix(ab_tile, others=()):
        mix_out = ab_tile.at[pl.ds(pl.multiple_of(c * WINDOW, WINDOW), WINDOW), :]
        _mix_prompt_block(_mod_pow2(blk, SEQ // WINDOW), sink_ref, q_ref, za_ref, u_ref, vb_ref, zb_ref,
                          kvc_ref, kvp_ref, gain_ref, bias_ref, ws_ref, bs_ref, mix_out, others)

    def merged(ab_ref, ga_ref, gb_ref):
        ya = jnp.dot(ab_ref[:, 0:D_ATTN], wa_bf[...], preferred_element_type=f32)
        yb = jnp.dot(ab_ref[:, D_ATTN:], wb_bf[...], preferred_element_type=f32)
        return (_sigmoid(ga_ref[...].astype(f32)) * ya + _sigmoid(gb_ref[...].astype(f32)) * yb).astype(bf16)

    def mix_and_project(mix_tile, done_tile):
        cast_weights()
        half = TN // 2
        parts = {}

        def piece(lhs_cols, w_bf, key, cols):
            parts[key] = jnp.dot(done_tile[:, lhs_cols], w_bf[:, cols], preferred_element_type=f32)

        pieces = [functools.partial(piece, lhs_cols, w_bf, (name, hc), slice(hc * half, (hc + 1) * half))
                  for name, lhs_cols, w_bf in (("a", slice(0, D_ATTN), wa_bf), ("b", slice(D_ATTN, D_MODEL), wb_bf))
                  for hc in range(2)]
        mix(mix_tile, pieces)
        for hc in range(2):
            cols = slice(hc * half, (hc + 1) * half)
            op_ref[:, cols] = (_sigmoid(gap_ref[:, cols].astype(f32)) * parts["a", hc]
                               + _sigmoid(gbp_ref[:, cols].astype(f32)) * parts["b", hc]).astype(bf16)

    @pl.when(r == 0)
    def _():
        cast_weights()

        def project_sample():
            os_ref[...] = merged(abs_ref, gas_ref, gbs_ref)

        mix(ab_even, [project_sample])

    @pl.when((r < ROW_TILES) & (r % 2 == 1))
    def _():
        mix_and_project(ab_odd, ab_even)

    @pl.when((r > 0) & (r < ROW_TILES) & (r % 2 == 0))
    def _():
        mix_and_project(ab_even, ab_odd)

    @pl.when(r == ROW_TILES)
    def _():
        cast_weights()
        op_ref[...] = merged(ab_odd if (ROW_TILES - 1) % 2 else ab_even, gap_ref, gbp_ref)


def _mix_merge(proj_p, proj_s, kv_p, ab_s, sinks, gain, bias, w_s, b_s, w_pa, w_pb, layer):
    seq_blocks = SEQ // WINDOW
    ga_tile, gb_tile = GA_COL // TN, GB_COL // TN
    blk_of = lambda r, c: jnp.minimum(r, ROW_TILES - 1) * TILE_BLOCKS + c
    prev_of = lambda r, c: blk_of(r, c) - (_mod_pow2(blk_of(r, c), seq_blocks) > 0).astype(jnp.int32)
    tile_of = lambda r: jnp.maximum(r - 1, 0)
    col_of = lambda r, c: jnp.where(r == 0, 0, c)
    scol_of = lambda r, c: jnp.where(r == 0, c, TILE_BLOCKS - 1)
    full = lambda shape: pl.BlockSpec(shape, lambda r, c: (0,) * len(shape))
    return pl.pallas_call(
        _mix_merge_kernel,
        grid=(ROW_TILES + 1, TILE_BLOCKS),
        in_specs=[pl.BlockSpec(memory_space=pltpu.SMEM)] + _proj_specs(WINDOW, blk_of) + [
            pl.BlockSpec((WINDOW, 2 * D_KV), lambda r, c: (blk_of(r, c), 0)),
            pl.BlockSpec((WINDOW, 2 * D_KV), lambda r, c: (prev_of(r, c), 0)),
            full((1, D_SGU)), full((1, D_SGU)),
            full((N_SGU_GROUPS, CHUNK, CHUNK)), full((CHUNK, N_SGU_GROUPS)),
            full((ROWS_S, D_MODEL)),
            pl.BlockSpec((TM, TN), lambda r, c: (tile_of(r), ga_tile + col_of(r, c))),
            pl.BlockSpec((TM, TN), lambda r, c: (tile_of(r), gb_tile + col_of(r, c))),
            pl.BlockSpec((ROWS_S, TN), lambda r, c: (0, ga_tile + scol_of(r, c))),
            pl.BlockSpec((ROWS_S, TN), lambda r, c: (0, gb_tile + scol_of(r, c))),
            pl.BlockSpec((None, D_ATTN, TN), lambda r, c: (layer, 0, c)),
            pl.BlockSpec((None, D_SGU, TN), lambda r, c: (layer, 0, c)),
        ],
        out_specs=[pl.BlockSpec((TM, TN), lambda r, c: (tile_of(r), col_of(r, c))),
                   pl.BlockSpec((ROWS_S, TN), lambda r, c: (0, scol_of(r, c)))],
        out_shape=[jax.ShapeDtypeStruct((ROWS_P, D_MODEL), bf16), jax.ShapeDtypeStruct((ROWS_S, D_MODEL), bf16)],
        scratch_shapes=[pltpu.VMEM((TM, D_MODEL), bf16), pltpu.VMEM((TM, D_MODEL), bf16),
                        pltpu.VMEM((D_ATTN, TN), bf16), pltpu.VMEM((D_SGU, TN), bf16)],
        compiler_params=_params(2),
        name="mix_merge",
    )(sinks, proj_p, proj_p, proj_p, proj_p, proj_p, kv_p, kv_p, gain, bias, w_s, b_s,
      ab_s, proj_p, proj_p, proj_s, proj_s, w_pa, w_pb)


def _mix_sample_kernel(sink_ref, q_ref, za_ref, u_ref, vb_ref, zb_ref, kvn_ref, ck_ref, cv_ref,
                       gain_ref, bias_ref, ws_ref, bs_ref, ab_ref, vn_ref):
    t = SEQ_GROUP * DEC_SEQ
    n_cache = SEQ_GROUP * WINDOW
    lk = n_cache + WINDOW
    kvn = kvn_ref[...]
    pad = jnp.zeros((WINDOW - t, D_KV), f32)
    k_all = jnp.concatenate([ck_ref[...].reshape(n_cache, D_KV), kvn[:, 0:D_KV], pad], axis=0)
    v_all = jnp.concatenate([cv_ref[...].reshape(n_cache, D_KV), kvn[:, D_KV:], pad], axis=0)
    qi = lax.broadcasted_iota(jnp.int32, (t, lk), 0)
    kj = lax.broadcasted_iota(jnp.int32, (t, lk), 1)
    q_seq, q_tok = _div_pow2(qi, DEC_SEQ), _mod_pow2(qi, DEC_SEQ)
    in_cache = (_div_pow2(kj, WINDOW) == q_seq) & (_mod_pow2(kj, WINDOW) > q_tok)
    kn = jnp.maximum(kj - n_cache, 0)
    in_new = (kn < t) & (_div_pow2(kn, DEC_SEQ) == q_seq) & (_mod_pow2(kn, DEC_SEQ) <= q_tok)
    mask = ((kj < n_cache) & in_cache) | ((kj >= n_cache) & in_new)
    slot_seq = _div_pow2(lax.broadcasted_iota(jnp.int32, (t, n_cache), 0), DEC_SEQ)
    slot = lax.broadcasted_iota(jnp.int32, (t, n_cache), 1) == slot_seq * WINDOW
    key_row = lax.broadcasted_iota(jnp.int32, (lk, 2 * HEAD_DIM), 0)
    live_key = (key_row >= n_cache) | (_mod_pow2(key_row, WINDOW) != 0)
    heads = _attention_stages(q_ref, k_all, v_all, mask, slot, n_cache, live_key, sink_ref, za_ref, ab_ref, t, lk)

    vn = _sgu_norm(vb_ref, gain_ref, bias_ref)
    vn_ref[...] = vn
    r = lax.broadcasted_iota(jnp.int32, (t, t), 0)
    c = lax.broadcasted_iota(jnp.int32, (t, t), 1)
    w_mask = (_div_pow2(r, DEC_SEQ) == _div_pow2(c, DEC_SEQ)) & (c <= r)
    _interleave(heads, _sgu_stages(u_ref, zb_ref, ws_ref, bs_ref, vn, w_mask, ab_ref))


def _proj_specs(rows, row_of):
    return [pl.BlockSpec((rows, D_ATTN), functools.partial(lambda s, *g: (row_of(*g), s), s))
            for s in range(5)]


def _mix_sample(proj, kv, cache_k, cache_v, layer, sinks, gain, bias, w_s, b_s):
    t = SEQ_GROUP * DEC_SEQ
    groups = DEC_BATCH // SEQ_GROUP
    full = lambda shape: pl.BlockSpec(shape, lambda g: (0,) * len(shape))
    cache_spec = pl.BlockSpec((SEQ_GROUP, WINDOW, D_KV), lambda g: (layer * groups + g, 0, 0))
    return pl.pallas_call(
        _mix_sample_kernel,
        grid=(groups,),
        in_specs=[pl.BlockSpec(memory_space=pltpu.SMEM)] + _proj_specs(t, lambda g: g) + [
            pl.BlockSpec((t, 2 * D_KV), lambda g: (g, 0)),
            cache_spec, cache_spec,
            full((1, D_SGU)), full((1, D_SGU)),
            full((N_SGU_GROUPS, t, t)), full((t, N_SGU_GROUPS)),
        ],
        out_specs=[pl.BlockSpec((t, D_MODEL), lambda g: (g, 0)),
                   pl.BlockSpec((t, D_SGU), lambda g: (g, 0))],
        out_shape=[jax.ShapeDtypeStruct((ROWS_S, D_MODEL), bf16),
                   jax.ShapeDtypeStruct((ROWS_S, D_SGU), f32)],
        compiler_params=_params(1),
        name="mix_sample",
    )(sinks, proj, proj, proj, proj, proj, kv, cache_k, cache_v, gain, bias, w_s, b_s)


def _post_norm(x, y, gate, g_ref, b_ref):
    t = x + (gate * (1.0 / ALPHA)) * y.astype(f32)
    mu = jnp.mean(t, axis=1, keepdims=True)
    d = t - mu
    var = jnp.mean(d * d, axis=1, keepdims=True)
    return d * lax.rsqrt(var + LN_EPS / ALPHA ** 2) * g_ref[...] + b_ref[...]


def _ln_kernel(*refs, n):
    xp_ref, xs_ref = refs[0:2]
    y_refs = refs[2:2 + 2 * n]
    gate_refs = refs[2 + 2 * n:2 + 3 * n]
    gb_refs = refs[2 + 3 * n:2 + 5 * n]
    rest = refs[2 + 5 * n:]
    i = pl.program_id(0)

    def stream(x, sample):
        for l in range(n):
            if sample:
                gate = gate_refs[l][0, 0:ROWS_S, :]
            else:
                gate = _prompt_mod_row(gate_refs[l], 0, i, SEQ // LN_TM)
            x = _post_norm(x, y_refs[2 * l + sample][...], gate, gb_refs[2 * l], gb_refs[2 * l + 1])
        return x

    if n == DEPTH:
        xp_out, xs_out = rest

        @pl.when(i < LN_BLOCKS)
        def _():
            xp_out[...] = stream(xp_ref[...], 0)

        @pl.when(i == LN_BLOCKS)
        def _():
            xs_out[...] = stream(xs_ref[...], 1)
    else:
        nxt_ref, w_ref, hp_out, hs_out, kvp_out, kvs_out, w_bf = rest

        @pl.when(i == 0)
        def _():
            w_bf[...] = w_ref[...].astype(bf16)

        @pl.when(i < LN_BLOCKS)
        def _():
            shift = _prompt_mod_row(nxt_ref, 0, i, SEQ // LN_TM)
            scale = _prompt_mod_row(nxt_ref, 1, i, SEQ // LN_TM)
            h = (stream(xp_ref[...], 0) * (1.0 + scale) + shift).astype(bf16)
            hp_out[...] = h
            kvp_out[...] = jnp.dot(h, w_bf[...], preferred_element_type=f32)

        @pl.when(i == LN_BLOCKS)
        def _():
            h = (stream(xs_ref[...], 1) * (1.0 + nxt_ref[1, 0:ROWS_S, :]) + nxt_ref[0, 0:ROWS_S, :]).astype(bf16)
            hs_out[...] = h
            kvs_out[...] = jnp.dot(h, w_bf[...], preferred_element_type=f32)


def _stream(xp, xs, ys, mod, ln_gain, ln_bias, w_in):
    n = len(ys)
    in_specs = [_LN_ROW_SPEC, _LN_SAMPLE_SPEC] + [_LN_ROW_SPEC, _LN_SAMPLE_SPEC] * n
    in_specs += [pl.BlockSpec((None, 1, MOD_ROWS, D_MODEL), functools.partial(lambda l, i: (l, 2, 0, 0), l))
                 for l in range(n)]
    in_specs += [pl.BlockSpec((None, 1, D_MODEL), functools.partial(lambda l, i: (l, 0, 0), l))
                 for l in range(n) for _ in range(2)]
    args = [xp, xs] + [y for pair in ys for y in pair] + [mod] * n
    for l in range(n):
        args += [ln_gain.reshape(DEPTH, 1, D_MODEL), ln_bias.reshape(DEPTH, 1, D_MODEL)]
    if n == DEPTH:
        out_specs = [_LN_ROW_SPEC, _LN_SAMPLE_SPEC]
        out_shape = [jax.ShapeDtypeStruct((ROWS_P, D_MODEL), f32), jax.ShapeDtypeStruct((ROWS_S, D_MODEL), f32)]
        scratch = []
    else:
        in_specs += [pl.BlockSpec((None, 2, MOD_ROWS, D_MODEL), lambda i: (n, 0, 0, 0)),
                     pl.BlockSpec((None, D_MODEL, TN), lambda i: (n, 0, KV_TILE), pipeline_mode=pl.Buffered(1))]
        args += [mod, w_in]
        out_specs = [_LN_ROW_SPEC, _LN_SAMPLE_SPEC, pl.BlockSpec((LN_TM, 2 * D_KV), _ln_row),
                     pl.BlockSpec((ROWS_S, 2 * D_KV), lambda i: (0, 0))]
        out_shape = [jax.ShapeDtypeStruct((ROWS_P, D_MODEL), bf16), jax.ShapeDtypeStruct((ROWS_S, D_MODEL), bf16),
                     jax.ShapeDtypeStruct((ROWS_P, 2 * D_KV), f32), jax.ShapeDtypeStruct((ROWS_S, 2 * D_KV), f32)]
        scratch = [pltpu.VMEM((D_MODEL, TN), bf16)]
    return pl.pallas_call(
        functools.partial(_ln_kernel, n=n),
        grid=(LN_BLOCKS + 1,),
        in_specs=in_specs,
        out_specs=out_specs,
        out_shape=out_shape,
        scratch_shapes=scratch,
        compiler_params=_params(1),
        name=f"stream{n}",
    )(*args)


def kernel(x_prompt, x_sample, cache_k, cache_v, c_prompt, c_sample, w_ada, b_ada, w_in, attn_sinks,
           sgu_ln_gain, sgu_ln_bias, sgu_w_s, sgu_b_s, w_pa, w_pb, w_o, ln_gain, ln_bias):
    xp = x_prompt.reshape(ROWS_P, D_MODEL)
    xs = x_sample.reshape(ROWS_S, D_MODEL)
    c_all = jnp.concatenate([jnp.repeat(c_sample, DEC_SEQ, axis=0), c_prompt,
                             jnp.zeros((MOD_ROWS - ROWS_S - BATCH, D_MODEL), f32)], axis=0)
    mod = _ada(c_all, w_ada, b_ada)
    ck = cache_k.reshape(DEPTH * DEC_BATCH, WINDOW, D_KV)
    cv = cache_v.reshape(DEPTH * DEC_BATCH, WINDOW, D_KV)

    win_k, win_v, new_k, new_v, sgu_v, ys = [], [], [], [], [], []
    for l in range(DEPTH):
        hp, hs, kv_p, kv_s = _stream(xp, xs, ys, mod, ln_gain, ln_bias, w_in)
        proj_p, proj_s = _matmul(hp, hs, w_in, l, PROJ_COLS // TN, _skip_kv_tile, bf16, "proj")
        gain = sgu_ln_gain[l].reshape(1, D_SGU)
        bias = sgu_ln_bias[l].reshape(1, D_SGU)
        w_small = jnp.tile(sgu_w_s[l][:, :DEC_SEQ, :DEC_SEQ], (1, SEQ_GROUP, SEQ_GROUP))
        b_small = jnp.tile(sgu_b_s[l][:, :DEC_SEQ].T, (SEQ_GROUP, 1))
        ab_s, vn_s = _mix_sample(proj_s, kv_s, ck, cv, l, attn_sinks[l], gain, bias, w_small, b_small)
        m_p, m_s = _mix_merge(proj_p, proj_s, kv_p, ab_s, attn_sinks[l], gain, bias, sgu_w_s[l], sgu_b_s[l].T,
                              w_pa, w_pb, l)
        ys.append(_matmul(m_p, m_s, w_o, l, D_MODEL // TN, lambda j: j, bf16, "out"))

        kv_win = kv_p.reshape(BATCH, SEQ, 2 * D_KV)[:, SEQ - WINDOW:]
        kv_win = kv_win.reshape(BATCH, WINDOW, 2, N_KV_HEADS, HEAD_DIM)
        kv_new = kv_s.reshape(DEC_BATCH, DEC_SEQ, 2, N_KV_HEADS, HEAD_DIM)
        win_k.append(kv_win[:, :, 0])
        win_v.append(kv_win[:, :, 1])
        new_k.append(kv_new[:, :, 0])
        new_v.append(kv_new[:, :, 1])
        sgu_v.append(vn_s.reshape(DEC_BATCH, DEC_SEQ, D_SGU))

    xp, xs = _stream(xp, xs, ys, mod, ln_gain, ln_bias, w_in)
    return (xp.reshape(BATCH, SEQ, D_MODEL), xs.reshape(DEC_BATCH, DEC_SEQ, D_MODEL),
            jnp.stack(win_k), jnp.stack(win_v), jnp.stack(new_k), jnp.stack(new_v), jnp.stack(sgu_v))
```

```python
import functools

import jax
import jax.numpy as jnp
from jax import lax
from jax.experimental import pallas as pl
from jax.experimental.pallas import tpu as pltpu

D_MODEL = 4096
BATCH = 4
SEQ = 2048
DEPTH = 2
DEC_BATCH = 32
DEC_SEQ = 4
HEAD_DIM = 64
D_ATTN = D_MODEL // 2
N_Q_HEADS = D_ATTN // HEAD_DIM
N_KV_HEADS = N_Q_HEADS // 8
GQA_GROUP = N_Q_HEADS // N_KV_HEADS
D_KV = N_KV_HEADS * HEAD_DIM
WINDOW = 128
D_SGU = D_MODEL // 2
N_SGU_GROUPS = 8
SGU_GROUP_DIM = D_SGU // N_SGU_GROUPS
CHUNK = 128
ALPHA = (2 * DEPTH) ** 0.25
LN_EPS = 1e-5
IN_COLS = D_ATTN + 2 * D_KV + D_ATTN + 3 * D_SGU + 2 * D_MODEL

ROWS_P = BATCH * SEQ
ROWS_S = DEC_BATCH * DEC_SEQ
MOD_ROWS = ROWS_S + 8
PROJ_COLS = IN_COLS - 2 * D_KV

TN = 512
TM = 1024
MM_TM = 2048
ADA_TN = 1024
KV_TILE = D_ATTN // TN
LN_TM = 256
SEQ_GROUP = 8
NEG = -1e30
LOG2E = 1.4426950408889634

VMEM_LIMIT = 56 * 1024 * 1024
STREAM_VMEM_LIMIT = 62 * 1024 * 1024

bf16 = jnp.bfloat16
f32 = jnp.float32


def _params(n_axes, vmem=VMEM_LIMIT):
    return pltpu.CompilerParams(dimension_semantics=("arbitrary",) * n_axes, vmem_limit_bytes=vmem)


def _sigmoid(x):
    return 0.5 + 0.5 * jnp.tanh(0.5 * x)


def _silu(x):
    half = 0.5 * x
    return half + half * jnp.tanh(half)


def _div_pow2(x, n):
    assert n & (n - 1) == 0
    return x >> (n.bit_length() - 1)


def _mod_pow2(x, n):
    assert n & (n - 1) == 0
    return x & (n - 1)


def _ada_kernel(c_ref, w_ref, b_ref, o_ref):
    a = _silu(c_ref[...]).astype(bf16)
    o_ref[...] = jnp.dot(a, w_ref[...].astype(bf16), preferred_element_type=f32) + b_ref[...]


def _ada(c_all, w_ada, b_ada):
    tiles_per_part = D_MODEL // ADA_TN
    return pl.pallas_call(
        _ada_kernel,
        grid=(DEPTH, 3 * tiles_per_part),
        in_specs=[
            pl.BlockSpec((MOD_ROWS, D_MODEL), lambda l, j: (0, 0)),
            pl.BlockSpec((None, D_MODEL, ADA_TN), lambda l, j: (l, 0, j)),
            pl.BlockSpec((None, 1, ADA_TN), lambda l, j: (l, 0, j)),
        ],
        out_specs=pl.BlockSpec((None, None, MOD_ROWS, ADA_TN),
                               lambda l, j: (l, j // tiles_per_part, 0, j % tiles_per_part)),
        out_shape=jax.ShapeDtypeStruct((DEPTH, 3, MOD_ROWS, D_MODEL), f32),
        compiler_params=_params(2),
        name="ada",
    )(c_all, w_ada, b_ada.reshape(DEPTH, 1, 3 * D_MODEL))


def _prompt_mod_row(ref, part, blk, blocks_per_batch):
    return ref[part, pl.ds(ROWS_S + blk // blocks_per_batch, 1), :]


def _matmul_kernel(xp_ref, xs_ref, w_ref, op_ref, os_ref, w_bf):
    @pl.when(pl.program_id(1) == 0)
    def _():
        w_bf[...] = w_ref[...].astype(bf16)
        os_ref[...] = jnp.dot(xs_ref[...], w_bf[...], preferred_element_type=f32).astype(os_ref.dtype)

    op_ref[...] = jnp.dot(xp_ref[...], w_bf[...], preferred_element_type=f32).astype(op_ref.dtype)


def _matmul(xp, xs, w, layer, n_tiles, w_tile_of, out_dtype, name):
    k = xp.shape[1]
    out_bytes = jnp.dtype(out_dtype).itemsize
    vmem = (2 * (MM_TM + ROWS_S) * k * 2 + 2 * k * TN * 4 + k * TN * 2 + 2 * (MM_TM + ROWS_S) * TN * out_bytes
            + (4 << 20))
    return pl.pallas_call(
        _matmul_kernel,
        grid=(n_tiles, ROWS_P // MM_TM),
        in_specs=[pl.BlockSpec((MM_TM, k), lambda j, i: (i, 0)),
                  pl.BlockSpec((ROWS_S, k), lambda j, i: (0, 0)),
                  pl.BlockSpec((None, k, TN), lambda j, i: (layer, 0, w_tile_of(j)))],
        out_specs=[pl.BlockSpec((MM_TM, TN), lambda j, i: (i, j)),
                   pl.BlockSpec((ROWS_S, TN), lambda j, i: (0, j))],
        out_shape=[jax.ShapeDtypeStruct((ROWS_P, n_tiles * TN), out_dtype),
                   jax.ShapeDtypeStruct((ROWS_S, n_tiles * TN), out_dtype)],
        scratch_shapes=[pltpu.VMEM((k, TN), bf16)],
        compiler_params=_params(2, vmem),
        name=name,
    )(xp, xs, w)


def _skip_kv_tile(c):
    return c + (c >= KV_TILE).astype(jnp.int32)


GA_COL = 5 * D_ATTN
GB_COL = GA_COL + D_MODEL


def _attention_stages(q_ref, k_all, v_all, mask, slot, slot_cols, live_key, sink_ref, za_ref, ab_ref, t, lk):
    lane = lax.broadcasted_iota(jnp.int32, (lk, 2 * HEAD_DIM), 1)
    low, high = lane < HEAD_DIM, lane >= HEAD_DIM
    ones2 = jnp.concatenate([low, high], axis=0).astype(f32).astype(bf16)
    nt = (((1,), (1,)), ((), ()))
    pairs = GQA_GROUP // 2
    k_scale = HEAD_DIM ** -0.5 * LOG2E

    def head(kv):
        cols = slice((kv // 2) * 2 * HEAD_DIM, (kv // 2 + 1) * 2 * HEAD_DIM)
        kp, vp = k_all[:, cols] * k_scale, v_all[:, cols]
        keep = (low if kv % 2 == 0 else high) & live_key
        k_own = jnp.where(keep, kp, 0.0)
        v_own = jnp.where(keep, vp, 0.0)
        k_oth = pltpu.roll(k_own, HEAD_DIM, 1)
        v_oth = pltpu.roll(v_own, HEAD_DIM, 1)
        if kv % 2 == 0:
            k2 = jnp.concatenate([k_own, k_oth], axis=0).astype(bf16)
            v2 = jnp.concatenate([v_own, v_oth], axis=0).astype(bf16)
        else:
            k2 = jnp.concatenate([k_oth, k_own], axis=0).astype(bf16)
            v2 = jnp.concatenate([v_oth, v_own], axis=0).astype(bf16)
        v2 = jnp.concatenate([v2, ones2], axis=1)
        base = kv * GQA_GROUP * HEAD_DIM
        q4 = jnp.concatenate(
            [q_ref[:, base + p * 128: base + (p + 1) * 128] for p in range(pairs)], axis=0)
        s_all = lax.dot_general(q4, k2, nt, preferred_element_type=f32)
        rows = []
        for p in range(pairs):
            halves = []
            for h in range(2):
                sink = sink_ref[kv * GQA_GROUP + 2 * p + h] * LOG2E
                fill = jnp.where(slot, sink, NEG)
                if slot_cols < lk:
                    fill = jnp.concatenate([fill, jnp.full((t, lk - slot_cols), NEG, f32)], axis=1)
                s = jnp.where(mask, s_all[p * t:(p + 1) * t, h * lk:(h + 1) * lk], fill)
                m = jnp.max(s, axis=1, keepdims=True)
                halves.append(jnp.exp2(s - m).astype(bf16))
            rows.append(jnp.concatenate(halves, axis=1))
        probs = jnp.concatenate(rows, axis=0)
        o = jnp.dot(probs, v2, preferred_element_type=f32)
        o = o[:, 0:128] / o[:, 128:256]
        for p in range(pairs):
            cols = slice(base + p * 128, base + (p + 1) * 128)
            ab_ref[:, cols] = (o[p * t:(p + 1) * t, :] * _silu(za_ref[:, cols].astype(f32))).astype(bf16)

    return [functools.partial(head, kv) for kv in range(N_KV_HEADS)]


def _sgu_norm(vb_ref, gain_ref, bias_ref):
    vb = vb_ref[...].astype(f32)
    mu = jnp.mean(vb, axis=1, keepdims=True)
    d = vb - mu
    var = jnp.mean(d * d, axis=1, keepdims=True)
    return d * lax.rsqrt(var + LN_EPS) * gain_ref[...] + bias_ref[...]


def _sgu_stages(u_ref, zb_ref, ws_ref, bs_ref, vn, w_mask, ab_ref):
    vn_bf = vn.astype(bf16)

    def group(g):
        cols = slice(g * SGU_GROUP_DIM, (g + 1) * SGU_GROUP_DIM)
        w_g = jnp.where(w_mask, ws_ref[g], 0.0).astype(bf16)
        s = jnp.dot(w_g, vn_bf[:, cols], preferred_element_type=f32) + bs_ref[:, g:g + 1]
        out_b = u_ref[:, cols].astype(f32) * s * _silu(zb_ref[:, cols].astype(f32))
        ab_ref[:, D_ATTN + g * SGU_GROUP_DIM: D_ATTN + (g + 1) * SGU_GROUP_DIM] = out_b.astype(bf16)

    return [functools.partial(group, g) for g in range(N_SGU_GROUPS)]


def _interleave(attention, sgu, others=()):
    per_head = len(sgu) // len(attention)
    others = list(others)
    if others:
        others.pop(0)()
    for kv, head in enumerate(attention):
        head()
        if others:
            others.pop(0)()
        for stage in sgu[kv * per_head:(kv + 1) * per_head]:
            stage()


def _mix_prompt_block(n, sink_ref, q_ref, za_ref, u_ref, vb_ref, zb_ref, kvc_ref, kvp_ref,
                      gain_ref, bias_ref, ws_ref, bs_ref, ab_ref, others=()):
    vn = _sgu_norm(vb_ref, gain_ref, bias_ref)
    kvc, kvp = kvc_ref[...], kvp_ref[...]
    k_all = jnp.concatenate([kvp[:, 0:D_KV], kvc[:, 0:D_KV]], axis=0)
    v_all = jnp.concatenate([kvp[:, D_KV:], kvc[:, D_KV:]], axis=0)
    qi = lax.broadcasted_iota(jnp.int32, (WINDOW, 2 * WINDOW), 0)
    kj = lax.broadcasted_iota(jnp.int32, (WINDOW, 2 * WINDOW), 1)
    no_prev = jnp.where(n > 0, 0, 2 * WINDOW)
    mask = ((kj < WINDOW) & (kj > qi + no_prev)) | ((kj >= WINDOW) & (kj - WINDOW <= qi))
    slot = lax.broadcasted_iota(jnp.int32, (WINDOW, WINDOW), 1) == 0
    live_key = lax.broadcasted_iota(jnp.int32, (2 * WINDOW, 2 * HEAD_DIM), 0) != 0
    heads = _attention_stages(q_ref, k_all, v_all, mask, slot, WINDOW, live_key, sink_ref, za_ref, ab_ref,
                              WINDOW, 2 * WINDOW)
    r = lax.broadcasted_iota(jnp.int32, (CHUNK, CHUNK), 0)
    c = lax.broadcasted_iota(jnp.int32, (CHUNK, CHUNK), 1)
    _interleave(heads, _sgu_stages(u_ref, zb_ref, ws_ref, bs_ref, vn, c <= r, ab_ref), others)


ROW_TILES = ROWS_P // TM
TILE_BLOCKS = TM // WINDOW
assert TILE_BLOCKS == D_MODEL // TN


def _mix_merge_kernel(sink_ref, q_ref, za_ref, u_ref, vb_ref, zb_ref, kvc_ref, kvp_ref,
                      gain_ref, bias_ref, ws_ref, bs_ref,
                      abs_ref, gap_ref, gbp_ref, gas_ref, gbs_ref, wa_ref, wb_ref,
                      op_ref, os_ref, ab_even, ab_odd, wa_bf, wb_bf):
    r, c = pl.program_id(0), pl.program_id(1)
    blk = jnp.minimum(r, ROW_TILES - 1) * TILE_BLOCKS + c

    def cast_weights():
        wa_bf[...] = wa_ref[...].astype(bf16)
        wb_bf[...] = wb_ref[...].astype(bf16)

    def mix(ab_tile, others=()):
        mix_out = ab_tile.at[pl.ds(pl.multiple_of(c * WINDOW, WINDOW), WINDOW), :]
        _mix_prompt_block(_mod_pow2(blk, SEQ // WINDOW), sink_ref, q_ref, za_ref, u_ref, vb_ref, zb_ref,
                          kvc_ref, kvp_ref, gain_ref, bias_ref, ws_ref, bs_ref, mix_out, others)

    def merged(ab_ref, ga_ref, gb_ref):
        ya = jnp.dot(ab_ref[:, 0:D_ATTN], wa_bf[...], preferred_element_type=f32)
        yb = jnp.dot(ab_ref[:, D_ATTN:], wb_bf[...], preferred_element_type=f32)
        return (_sigmoid(ga_ref[...].astype(f32)) * ya + _sigmoid(gb_ref[...].astype(f32)) * yb).astype(bf16)

    def mix_and_project(mix_tile, done_tile):
        cast_weights()
        half = TN // 2
        parts = {}

        def piece(lhs_cols, w_bf, key, cols):
            parts[key] = jnp.dot(done_tile[:, lhs_cols], w_bf[:, cols], preferred_element_type=f32)

        pieces = [functools.partial(piece, lhs_cols, w_bf, (name, hc), slice(hc * half, (hc + 1) * half))
                  for name, lhs_cols, w_bf in (("a", slice(0, D_ATTN), wa_bf), ("b", slice(D_ATTN, D_MODEL), wb_bf))
                  for hc in range(2)]
        mix(mix_tile, pieces)
        for hc in range(2):
            cols = slice(hc * half, (hc + 1) * half)
            op_ref[:, cols] = (_sigmoid(gap_ref[:, cols].astype(f32)) * parts["a", hc]
                               + _sigmoid(gbp_ref[:, cols].astype(f32)) * parts["b", hc]).astype(bf16)

    @pl.when(r == 0)
    def _():
        cast_weights()

        def project_sample():
            os_ref[...] = merged(abs_ref, gas_ref, gbs_ref)

        mix(ab_even, [project_sample])

    @pl.when((r < ROW_TILES) & (r % 2 == 1))
    def _():
        mix_and_project(ab_odd, ab_even)

    @pl.when((r > 0) & (r < ROW_TILES) & (r % 2 == 0))
    def _():
        mix_and_project(ab_even, ab_odd)

    @pl.when(r == ROW_TILES)
    def _():
        cast_weights()
        op_ref[...] = merged(ab_odd if (ROW_TILES - 1) % 2 else ab_even, gap_ref, gbp_ref)


def _mix_merge(proj_p, proj_s, kv_p, ab_s, sinks, gain, bias, w_s, b_s, w_pa, w_pb, layer):
    seq_blocks = SEQ // WINDOW
    ga_tile, gb_tile = GA_COL // TN, GB_COL // TN
    blk_of = lambda r, c: jnp.minimum(r, ROW_TILES - 1) * TILE_BLOCKS + c
    prev_of = lambda r, c: blk_of(r, c) - (_mod_pow2(blk_of(r, c), seq_blocks) > 0).astype(jnp.int32)
    tile_of = lambda r: jnp.maximum(r - 1, 0)
    col_of = lambda r, c: jnp.where(r == 0, 0, c)
    scol_of = lambda r, c: jnp.where(r == 0, c, TILE_BLOCKS - 1)
    full = lambda shape: pl.BlockSpec(shape, lambda r, c: (0,) * len(shape))
    return pl.pallas_call(
        _mix_merge_kernel,
        grid=(ROW_TILES + 1, TILE_BLOCKS),
        in_specs=[pl.BlockSpec(memory_space=pltpu.SMEM)] + _proj_specs(WINDOW, blk_of) + [
            pl.BlockSpec((WINDOW, 2 * D_KV), lambda r, c: (blk_of(r, c), 0)),
            pl.BlockSpec((WINDOW, 2 * D_KV), lambda r, c: (prev_of(r, c), 0)),
            full((1, D_SGU)), full((1, D_SGU)),
            full((N_SGU_GROUPS, CHUNK, CHUNK)), full((CHUNK, N_SGU_GROUPS)),
            full((ROWS_S, D_MODEL)),
            pl.BlockSpec((TM, TN), lambda r, c: (tile_of(r), ga_tile + col_of(r, c))),
            pl.BlockSpec((TM, TN), lambda r, c: (tile_of(r), gb_tile + col_of(r, c))),
            pl.BlockSpec((ROWS_S, TN), lambda r, c: (0, ga_tile + scol_of(r, c))),
            pl.BlockSpec((ROWS_S, TN), lambda r, c: (0, gb_tile + scol_of(r, c))),
            pl.BlockSpec((None, D_ATTN, TN), lambda r, c: (layer, 0, c)),
            pl.BlockSpec((None, D_SGU, TN), lambda r, c: (layer, 0, c)),
        ],
        out_specs=[pl.BlockSpec((TM, TN), lambda r, c: (tile_of(r), col_of(r, c))),
                   pl.BlockSpec((ROWS_S, TN), lambda r, c: (0, scol_of(r, c)))],
        out_shape=[jax.ShapeDtypeStruct((ROWS_P, D_MODEL), bf16), jax.ShapeDtypeStruct((ROWS_S, D_MODEL), bf16)],
        scratch_shapes=[pltpu.VMEM((TM, D_MODEL), bf16), pltpu.VMEM((TM, D_MODEL), bf16),
                        pltpu.VMEM((D_ATTN, TN), bf16), pltpu.VMEM((D_SGU, TN), bf16)],
        compiler_params=_params(2),
        name="mix_merge",
    )(sinks, proj_p, proj_p, proj_p, proj_p, proj_p, kv_p, kv_p, gain, bias, w_s, b_s,
      ab_s, proj_p, proj_p, proj_s, proj_s, w_pa, w_pb)


def _mix_sample_kernel(sink_ref, q_ref, za_ref, u_ref, vb_ref, zb_ref, kvn_ref, ck_ref, cv_ref,
                       gain_ref, bias_ref, ws_ref, bs_ref, ab_ref, vn_ref):
    t = SEQ_GROUP * DEC_SEQ
    n_cache = SEQ_GROUP * WINDOW
    lk = n_cache + WINDOW
    kvn = kvn_ref[...]
    pad = jnp.zeros((WINDOW - t, D_KV), f32)
    k_all = jnp.concatenate([ck_ref[...].reshape(n_cache, D_KV), kvn[:, 0:D_KV], pad], axis=0)
    v_all = jnp.concatenate([cv_ref[...].reshape(n_cache, D_KV), kvn[:, D_KV:], pad], axis=0)
    qi = lax.broadcasted_iota(jnp.int32, (t, lk), 0)
    kj = lax.broadcasted_iota(jnp.int32, (t, lk), 1)
    q_seq, q_tok = _div_pow2(qi, DEC_SEQ), _mod_pow2(qi, DEC_SEQ)
    in_cache = (_div_pow2(kj, WINDOW) == q_seq) & (_mod_pow2(kj, WINDOW) > q_tok)
    kn = jnp.maximum(kj - n_cache, 0)
    in_new = (kn < t) & (_div_pow2(kn, DEC_SEQ) == q_seq) & (_mod_pow2(kn, DEC_SEQ) <= q_tok)
    mask = ((kj < n_cache) & in_cache) | ((kj >= n_cache) & in_new)
    slot_seq = _div_pow2(lax.broadcasted_iota(jnp.int32, (t, n_cache), 0), DEC_SEQ)
    slot = lax.broadcasted_iota(jnp.int32, (t, n_cache), 1) == slot_seq * WINDOW
    key_row = lax.broadcasted_iota(jnp.int32, (lk, 2 * HEAD_DIM), 0)
    live_key = (key_row >= n_cache) | (_mod_pow2(key_row, WINDOW) != 0)
    heads = _attention_stages(q_ref, k_all, v_all, mask, slot, n_cache, live_key, sink_ref, za_ref, ab_ref, t, lk)

    vn = _sgu_norm(vb_ref, gain_ref, bias_ref)
    vn_ref[...] = vn
    r = lax.broadcasted_iota(jnp.int32, (t, t), 0)
    c = lax.broadcasted_iota(jnp.int32, (t, t), 1)
    w_mask = (_div_pow2(r, DEC_SEQ) == _div_pow2(c, DEC_SEQ)) & (c <= r)
    _interleave(heads, _sgu_stages(u_ref, zb_ref, ws_ref, bs_ref, vn, w_mask, ab_ref))


def _proj_specs(rows, row_of):
    return [pl.BlockSpec((rows, D_ATTN), functools.partial(lambda s, *g: (row_of(*g), s), s))
            for s in range(5)]


def _mix_sample(proj, kv, cache_k, cache_v, layer, sinks, gain, bias, w_s, b_s):
    t = SEQ_GROUP * DEC_SEQ
    groups = DEC_BATCH // SEQ_GROUP
    full = lambda shape: pl.BlockSpec(shape, lambda g: (0,) * len(shape))
    cache_spec = pl.BlockSpec((SEQ_GROUP, WINDOW, D_KV), lambda g: (layer * groups + g, 0, 0))
    return pl.pallas_call(
        _mix_sample_kernel,
        grid=(groups,),
        in_specs=[pl.BlockSpec(memory_space=pltpu.SMEM)] + _proj_specs(t, lambda g: g) + [
            pl.BlockSpec((t, 2 * D_KV), lambda g: (g, 0)),
            cache_spec, cache_spec,
            full((1, D_SGU)), full((1, D_SGU)),
            full((N_SGU_GROUPS, t, t)), full((t, N_SGU_GROUPS)),
        ],
        out_specs=[pl.BlockSpec((t, D_MODEL), lambda g: (g, 0)),
                   pl.BlockSpec((t, D_SGU), lambda g: (g, 0))],
        out_shape=[jax.ShapeDtypeStruct((ROWS_S, D_MODEL), bf16),
                   jax.ShapeDtypeStruct((ROWS_S, D_SGU), f32)],
        compiler_params=_params(1),
        name="mix_sample",
    )(sinks, proj, proj, proj, proj, proj, kv, cache_k, cache_v, gain, bias, w_s, b_s)


def _post_norm(x, y, gate, g_ref, b_ref):
    t = x + (gate * (1.0 / ALPHA)) * y.astype(f32)
    mu = jnp.mean(t, axis=1, keepdims=True)
    d = t - mu
    var = jnp.mean(d * d, axis=1, keepdims=True)
    return d * lax.rsqrt(var + LN_EPS / ALPHA ** 2) * g_ref[...] + b_ref[...]


def _ln_kernel(*refs, n, tm):
    xp_ref, xs_ref = refs[0:2]
    y_refs = refs[2:2 + 2 * n]
    gate_refs = refs[2 + 2 * n:2 + 3 * n]
    gb_refs = refs[2 + 3 * n:2 + 5 * n]
    rest = refs[2 + 5 * n:]
    i = pl.program_id(0)
    blocks = ROWS_P // tm

    def stream(x, sample):
        for l in range(n):
            if sample:
                gate = gate_refs[l][0, 0:ROWS_S, :]
            else:
                gate = _prompt_mod_row(gate_refs[l], 0, i, SEQ // tm)
            x = _post_norm(x, y_refs[2 * l + sample][...], gate, gb_refs[2 * l], gb_refs[2 * l + 1])
        return x

    if n == DEPTH:
        xp_out, xs_out = rest

        @pl.when(i < blocks)
        def _():
            xp_out[...] = stream(xp_ref[...], 0)

        @pl.when(i == blocks)
        def _():
            xs_out[...] = stream(xs_ref[...], 1)
    else:
        nxt_ref, w_ref, hp_out, hs_out, kvp_out, kvs_out, w_bf = rest

        @pl.when(i == 0)
        def _():
            w_bf[...] = w_ref[...].astype(bf16)

        @pl.when(i < blocks)
        def _():
            shift = _prompt_mod_row(nxt_ref, 0, i, SEQ // tm)
            scale = _prompt_mod_row(nxt_ref, 1, i, SEQ // tm)
            h = (stream(xp_ref[...], 0) * (1.0 + scale) + shift).astype(bf16)
            hp_out[...] = h
            kvp_out[...] = jnp.dot(h, w_bf[...], preferred_element_type=f32)

        @pl.when(i == blocks)
        def _():
            h = (stream(xs_ref[...], 1) * (1.0 + nxt_ref[1, 0:ROWS_S, :]) + nxt_ref[0, 0:ROWS_S, :]).astype(bf16)
            hs_out[...] = h
            kvs_out[...] = jnp.dot(h, w_bf[...], preferred_element_type=f32)


def _stream(xp, xs, ys, mod, ln_gain, ln_bias, w_in):
    n = len(ys)
    tm = 2 * LN_TM if n == 0 else LN_TM
    blocks = ROWS_P // tm
    row = lambda i: (jnp.minimum(i, blocks - 1), 0)
    once = dict(pipeline_mode=pl.Buffered(1))
    row_spec = pl.BlockSpec((tm, D_MODEL), row)
    sample_in = pl.BlockSpec((ROWS_S, D_MODEL), lambda i: (0, 0), **once)
    sample_out = pl.BlockSpec((ROWS_S, D_MODEL), lambda i: (0, 0))
    in_specs = [row_spec, sample_in] + [row_spec, sample_in] * n
    in_specs += [pl.BlockSpec((None, 1, MOD_ROWS, D_MODEL), functools.partial(lambda l, i: (l, 2, 0, 0), l), **once)
                 for l in range(n)]
    in_specs += [pl.BlockSpec((None, 1, D_MODEL), functools.partial(lambda l, i: (l, 0, 0), l))
                 for l in range(n) for _ in range(2)]
    args = [xp, xs] + [y for pair in ys for y in pair] + [mod] * n
    for l in range(n):
        args += [ln_gain.reshape(DEPTH, 1, D_MODEL), ln_bias.reshape(DEPTH, 1, D_MODEL)]
    if n == DEPTH:
        out_specs = [row_spec, sample_out]
        out_shape = [jax.ShapeDtypeStruct((ROWS_P, D_MODEL), f32), jax.ShapeDtypeStruct((ROWS_S, D_MODEL), f32)]
        scratch = []
    else:
        in_specs += [pl.BlockSpec((None, 2, MOD_ROWS, D_MODEL), lambda i: (n, 0, 0, 0), **once),
                     pl.BlockSpec((None, D_MODEL, TN), lambda i: (n, 0, KV_TILE), **once)]
        args += [mod, w_in]
        out_specs = [row_spec, sample_out, pl.BlockSpec((tm, 2 * D_KV), row),
                     pl.BlockSpec((ROWS_S, 2 * D_KV), lambda i: (0, 0))]
        out_shape = [jax.ShapeDtypeStruct((ROWS_P, D_MODEL), bf16), jax.ShapeDtypeStruct((ROWS_S, D_MODEL), bf16),
                     jax.ShapeDtypeStruct((ROWS_P, 2 * D_KV), f32), jax.ShapeDtypeStruct((ROWS_S, 2 * D_KV), f32)]
        scratch = [pltpu.VMEM((D_MODEL, TN), bf16)]
    return pl.pallas_call(
        functools.partial(_ln_kernel, n=n, tm=tm),
        grid=(blocks + 1,),
        in_specs=in_specs,
        out_specs=out_specs,
        out_shape=out_shape,
        scratch_shapes=scratch,
        compiler_params=_params(1, STREAM_VMEM_LIMIT),
        name=f"stream{n}",
    )(*args)


def kernel(x_prompt, x_sample, cache_k, cache_v, c_prompt, c_sample, w_ada, b_ada, w_in, attn_sinks,
           sgu_ln_gain, sgu_ln_bias, sgu_w_s, sgu_b_s, w_pa, w_pb, w_o, ln_gain, ln_bias):
    xp = x_prompt.reshape(ROWS_P, D_MODEL)
    xs = x_sample.reshape(ROWS_S, D_MODEL)
    c_all = jnp.concatenate([jnp.repeat(c_sample, DEC_SEQ, axis=0), c_prompt,
                             jnp.zeros((MOD_ROWS - ROWS_S - BATCH, D_MODEL), f32)], axis=0)
    mod = _ada(c_all, w_ada, b_ada)
    ck = cache_k.reshape(DEPTH * DEC_BATCH, WINDOW, D_KV)
    cv = cache_v.reshape(DEPTH * DEC_BATCH, WINDOW, D_KV)

    win_k, win_v, new_k, new_v, sgu_v, ys = [], [], [], [], [], []
    for l in range(DEPTH):
        hp, hs, kv_p, kv_s = _stream(xp, xs, ys, mod, ln_gain, ln_bias, w_in)
        proj_p, proj_s = _matmul(hp, hs, w_in, l, PROJ_COLS // TN, _skip_kv_tile, bf16, "proj")
        gain = sgu_ln_gain[l].reshape(1, D_SGU)
        bias = sgu_ln_bias[l].reshape(1, D_SGU)
        w_small = jnp.tile(sgu_w_s[l][:, :DEC_SEQ, :DEC_SEQ], (1, SEQ_GROUP, SEQ_GROUP))
        b_small = jnp.tile(sgu_b_s[l][:, :DEC_SEQ].T, (SEQ_GROUP, 1))
        ab_s, vn_s = _mix_sample(proj_s, kv_s, ck, cv, l, attn_sinks[l], gain, bias, w_small, b_small)
        m_p, m_s = _mix_merge(proj_p, proj_s, kv_p, ab_s, attn_sinks[l], gain, bias, sgu_w_s[l], sgu_b_s[l].T,
                              w_pa, w_pb, l)
        ys.append(_matmul(m_p, m_s, w_o, l, D_MODEL // TN, lambda j: j, bf16, "out"))

        kv_win = kv_p.reshape(BATCH, SEQ, 2 * D_KV)[:, SEQ - WINDOW:]
        kv_win = kv_win.reshape(BATCH, WINDOW, 2, N_KV_HEADS, HEAD_DIM)
        kv_new = kv_s.reshape(DEC_BATCH, DEC_SEQ, 2, N_KV_HEADS, HEAD_DIM)
        win_k.append(kv_win[:, :, 0])
        win_v.append(kv_win[:, :, 1])
        new_k.append(kv_new[:, :, 0])
        new_v.append(kv_new[:, :, 1])
        sgu_v.append(vn_s.reshape(DEC_BATCH, DEC_SEQ, D_SGU))

    xp, xs = _stream(xp, xs, ys, mod, ln_gain, ln_bias, w_in)
    return (xp.reshape(BATCH, SEQ, D_MODEL), xs.reshape(DEC_BATCH, DEC_SEQ, D_MODEL),
            jnp.stack(win_k), jnp.stack(win_v), jnp.stack(new_k), jnp.stack(new_v), jnp.stack(sgu_v))
```

```python
import functools

import jax
import jax.numpy as jnp
from jax import lax
from jax.experimental import pallas as pl
from jax.experimental.pallas import tpu as pltpu

D_MODEL = 4096
BATCH = 4
SEQ = 2048
DEPTH = 2
DEC_BATCH = 32
DEC_SEQ = 4
HEAD_DIM = 64
D_ATTN = D_MODEL // 2
N_Q_HEADS = D_ATTN // HEAD_DIM
N_KV_HEADS = N_Q_HEADS // 8
GQA_GROUP = N_Q_HEADS // N_KV_HEADS
D_KV = N_KV_HEADS * HEAD_DIM
WINDOW = 128
D_SGU = D_MODEL // 2
N_SGU_GROUPS = 8
SGU_GROUP_DIM = D_SGU // N_SGU_GROUPS
CHUNK = 128
ALPHA = (2 * DEPTH) ** 0.25
LN_EPS = 1e-5
IN_COLS = D_ATTN + 2 * D_KV + D_ATTN + 3 * D_SGU + 2 * D_MODEL

ROWS_P = BATCH * SEQ
ROWS_S = DEC_BATCH * DEC_SEQ
MOD_ROWS = ROWS_S + 8
PROJ_COLS = IN_COLS - 2 * D_KV

TN = 512
TM = 1024
MM_TM = 2048
KV_TILE = D_ATTN // TN
LN_TM = 256
SEQ_GROUP = 8
NEG = -1e30
LOG2E = 1.4426950408889634

VMEM_LIMIT = 56 * 1024 * 1024

bf16 = jnp.bfloat16
f32 = jnp.float32


def _params(n_axes, vmem=VMEM_LIMIT):
    return pltpu.CompilerParams(dimension_semantics=("arbitrary",) * n_axes, vmem_limit_bytes=vmem)


def _sigmoid(x):
    return 0.5 + 0.5 * jnp.tanh(0.5 * x)


def _silu(x):
    half = 0.5 * x
    return half + half * jnp.tanh(half)


def _div_pow2(x, n):
    assert n & (n - 1) == 0
    return x >> (n.bit_length() - 1)


def _mod_pow2(x, n):
    assert n & (n - 1) == 0
    return x & (n - 1)


def _ada_kernel(c_ref, w_ref, b_ref, o_ref):
    a = _silu(c_ref[...]).astype(bf16)
    o_ref[...] = jnp.dot(a, w_ref[...].astype(bf16), preferred_element_type=f32) + b_ref[...]


def _ada(c_all, w_ada, b_ada):
    tiles_per_part = D_MODEL // TN
    return pl.pallas_call(
        _ada_kernel,
        grid=(DEPTH, 3 * tiles_per_part),
        in_specs=[
            pl.BlockSpec((MOD_ROWS, D_MODEL), lambda l, j: (0, 0)),
            pl.BlockSpec((None, D_MODEL, TN), lambda l, j: (l, 0, j)),
            pl.BlockSpec((None, 1, TN), lambda l, j: (l, 0, j)),
        ],
        out_specs=pl.BlockSpec((None, None, MOD_ROWS, TN),
                               lambda l, j: (l, j // tiles_per_part, 0, j % tiles_per_part)),
        out_shape=jax.ShapeDtypeStruct((DEPTH, 3, MOD_ROWS, D_MODEL), f32),
        compiler_params=_params(2),
        name="ada",
    )(c_all, w_ada, b_ada.reshape(DEPTH, 1, 3 * D_MODEL))


def _prompt_mod_row(ref, part, blk, blocks_per_batch):
    return ref[part, pl.ds(ROWS_S + blk // blocks_per_batch, 1), :]


def _matmul_kernel(xp_ref, xs_ref, w_ref, op_ref, os_ref, w_bf):
    @pl.when(pl.program_id(1) == 0)
    def _():
        w_bf[...] = w_ref[...].astype(bf16)
        os_ref[...] = jnp.dot(xs_ref[...], w_bf[...], preferred_element_type=f32).astype(os_ref.dtype)

    op_ref[...] = jnp.dot(xp_ref[...], w_bf[...], preferred_element_type=f32).astype(op_ref.dtype)


def _matmul(xp, xs, w, layer, n_tiles, w_tile_of, out_dtype, name):
    k = xp.shape[1]
    out_bytes = jnp.dtype(out_dtype).itemsize
    vmem = (2 * (MM_TM + ROWS_S) * k * 2 + 2 * k * TN * 4 + k * TN * 2 + 2 * (MM_TM + ROWS_S) * TN * out_bytes
            + (4 << 20))
    return pl.pallas_call(
        _matmul_kernel,
        grid=(n_tiles, ROWS_P // MM_TM),
        in_specs=[pl.BlockSpec((MM_TM, k), lambda j, i: (i, 0)),
                  pl.BlockSpec((ROWS_S, k), lambda j, i: (0, 0)),
                  pl.BlockSpec((None, k, TN), lambda j, i: (layer, 0, w_tile_of(j)))],
        out_specs=[pl.BlockSpec((MM_TM, TN), lambda j, i: (i, j)),
                   pl.BlockSpec((ROWS_S, TN), lambda j, i: (0, j))],
        out_shape=[jax.ShapeDtypeStruct((ROWS_P, n_tiles * TN), out_dtype),
                   jax.ShapeDtypeStruct((ROWS_S, n_tiles * TN), out_dtype)],
        scratch_shapes=[pltpu.VMEM((k, TN), bf16)],
        compiler_params=_params(2, vmem),
        name=name,
    )(xp, xs, w)


def _skip_kv_tile(c):
    return c + (c >= KV_TILE).astype(jnp.int32)


GA_COL = 5 * D_ATTN
GB_COL = GA_COL + D_MODEL


def _attention_stages(q_ref, k_all, v_all, mask, slot, slot_cols, live_key, sink_ref, za_ref, ab_ref, t, lk):
    lane = lax.broadcasted_iota(jnp.int32, (lk, 2 * HEAD_DIM), 1)
    low, high = lane < HEAD_DIM, lane >= HEAD_DIM
    ones2 = jnp.concatenate([low, high], axis=0).astype(f32).astype(bf16)
    nt = (((1,), (1,)), ((), ()))
    pairs = GQA_GROUP // 2
    k_scale = HEAD_DIM ** -0.5 * LOG2E

    def head(kv):
        cols = slice((kv // 2) * 2 * HEAD_DIM, (kv // 2 + 1) * 2 * HEAD_DIM)
        kp, vp = k_all[:, cols] * k_scale, v_all[:, cols]
        keep = (low if kv % 2 == 0 else high) & live_key
        k_own = jnp.where(keep, kp, 0.0)
        v_own = jnp.where(keep, vp, 0.0)
        k_oth = pltpu.roll(k_own, HEAD_DIM, 1)
        v_oth = pltpu.roll(v_own, HEAD_DIM, 1)
        if kv % 2 == 0:
            k2 = jnp.concatenate([k_own, k_oth], axis=0).astype(bf16)
            v2 = jnp.concatenate([v_own, v_oth], axis=0).astype(bf16)
        else:
            k2 = jnp.concatenate([k_oth, k_own], axis=0).astype(bf16)
            v2 = jnp.concatenate([v_oth, v_own], axis=0).astype(bf16)
        v2 = jnp.concatenate([v2, ones2], axis=1)
        base = kv * GQA_GROUP * HEAD_DIM
        q4 = jnp.concatenate(
            [q_ref[:, base + p * 128: base + (p + 1) * 128] for p in range(pairs)], axis=0)
        s_all = lax.dot_general(q4, k2, nt, preferred_element_type=f32)
        rows = []
        for p in range(pairs):
            halves = []
            for h in range(2):
                sink = sink_ref[kv * GQA_GROUP + 2 * p + h] * LOG2E
                fill = jnp.where(slot, sink, NEG)
                if slot_cols < lk:
                    fill = jnp.concatenate([fill, jnp.full((t, lk - slot_cols), NEG, f32)], axis=1)
                s = jnp.where(mask, s_all[p * t:(p + 1) * t, h * lk:(h + 1) * lk], fill)
                m = jnp.max(s, axis=1, keepdims=True)
                halves.append(jnp.exp2(s - m).astype(bf16))
            rows.append(jnp.concatenate(halves, axis=1))
        probs = jnp.concatenate(rows, axis=0)
        o = jnp.dot(probs, v2, preferred_element_type=f32)
        o = o[:, 0:128] / o[:, 128:256]
        for p in range(pairs):
            cols = slice(base + p * 128, base + (p + 1) * 128)
            ab_ref[:, cols] = (o[p * t:(p + 1) * t, :] * _silu(za_ref[:, cols].astype(f32))).astype(bf16)

    return [functools.partial(head, kv) for kv in range(N_KV_HEADS)]


def _sgu_norm(vb_ref, gain_ref, bias_ref):
    vb = vb_ref[...].astype(f32)
    mu = jnp.mean(vb, axis=1, keepdims=True)
    d = vb - mu
    var = jnp.mean(d * d, axis=1, keepdims=True)
    return d * lax.rsqrt(var + LN_EPS) * gain_ref[...] + bias_ref[...]


def _sgu_stages(u_ref, zb_ref, ws_ref, bs_ref, vn, w_mask, ab_ref):
    vn_bf = vn.astype(bf16)

    def group(g):
        cols = slice(g * SGU_GROUP_DIM, (g + 1) * SGU_GROUP_DIM)
        w_g = jnp.where(w_mask, ws_ref[g], 0.0).astype(bf16)
        s = jnp.dot(w_g, vn_bf[:, cols], preferred_element_type=f32) + bs_ref[:, g:g + 1]
        out_b = u_ref[:, cols].astype(f32) * s * _silu(zb_ref[:, cols].astype(f32))
        ab_ref[:, D_ATTN + g * SGU_GROUP_DIM: D_ATTN + (g + 1) * SGU_GROUP_DIM] = out_b.astype(bf16)

    return [functools.partial(group, g) for g in range(N_SGU_GROUPS)]


def _interleave(attention, sgu, others=()):
    per_head = len(sgu) // len(attention)
    others = list(others)
    if others:
        others.pop(0)()
    for kv, head in enumerate(attention):
        head()
        if others:
            others.pop(0)()
        for stage in sgu[kv * per_head:(kv + 1) * per_head]:
            stage()


def _mix_prompt_block(n, sink_ref, q_ref, za_ref, u_ref, vb_ref, zb_ref, kvc_ref, kvp_ref,
                      gain_ref, bias_ref, ws_ref, bs_ref, ab_ref, others=()):
    vn = _sgu_norm(vb_ref, gain_ref, bias_ref)
    kvc, kvp = kvc_ref[...], kvp_ref[...]
    k_all = jnp.concatenate([kvp[:, 0:D_KV], kvc[:, 0:D_KV]], axis=0)
    v_all = jnp.concatenate([kvp[:, D_KV:], kvc[:, D_KV:]], axis=0)
    qi = lax.broadcasted_iota(jnp.int32, (WINDOW, 2 * WINDOW), 0)
    kj = lax.broadcasted_iota(jnp.int32, (WINDOW, 2 * WINDOW), 1)
    no_prev = jnp.where(n > 0, 0, 2 * WINDOW)
    mask = ((kj < WINDOW) & (kj > qi + no_prev)) | ((kj >= WINDOW) & (kj - WINDOW <= qi))
    slot = lax.broadcasted_iota(jnp.int32, (WINDOW, WINDOW), 1) == 0
    live_key = lax.broadcasted_iota(jnp.int32, (2 * WINDOW, 2 * HEAD_DIM), 0) != 0
    heads = _attention_stages(q_ref, k_all, v_all, mask, slot, WINDOW, live_key, sink_ref, za_ref, ab_ref,
                              WINDOW, 2 * WINDOW)
    r = lax.broadcasted_iota(jnp.int32, (CHUNK, CHUNK), 0)
    c = lax.broadcasted_iota(jnp.int32, (CHUNK, CHUNK), 1)
    _interleave(heads, _sgu_stages(u_ref, zb_ref, ws_ref, bs_ref, vn, c <= r, ab_ref), others)


ROW_TILES = ROWS_P // TM
TILE_BLOCKS = TM // WINDOW
assert TILE_BLOCKS == D_MODEL // TN


def _mix_merge_kernel(sink_ref, q_ref, za_ref, u_ref, vb_ref, zb_ref, kvc_ref, kvp_ref,
                      gain_ref, bias_ref, ws_ref, bs_ref,
                      abs_ref, gap_ref, gbp_ref, gas_ref, gbs_ref, wa_ref, wb_ref,
                      op_ref, os_ref, ab_even, ab_odd, wa_bf, wb_bf):
    r, c = pl.program_id(0), pl.program_id(1)
    blk = jnp.minimum(r, ROW_TILES - 1) * TILE_BLOCKS + c

    def cast_weights():
        wa_bf[...] = wa_ref[...].astype(bf16)
        wb_bf[...] = wb_ref[...].astype(bf16)

    def mix(ab_tile, others=()):
        mix_out = ab_tile.at[pl.ds(pl.multiple_of(c * WINDOW, WINDOW), WINDOW), :]
        _mix_prompt_block(_mod_pow2(blk, SEQ // WINDOW), sink_ref, q_ref, za_ref, u_ref, vb_ref, zb_ref,
                          kvc_ref, kvp_ref, gain_ref, bias_ref, ws_ref, bs_ref, mix_out, others)

    def merged(ab_ref, ga_ref, gb_ref):
        ya = jnp.dot(ab_ref[:, 0:D_ATTN], wa_bf[...], preferred_element_type=f32)
        yb = jnp.dot(ab_ref[:, D_ATTN:], wb_bf[...], preferred_element_type=f32)
        return (_sigmoid(ga_ref[...].astype(f32)) * ya + _sigmoid(gb_ref[...].astype(f32)) * yb).astype(bf16)

    def mix_and_project(mix_tile, done_tile):
        cast_weights()
        half = TN // 2
        parts = {}

        def piece(lhs_cols, w_bf, key, cols):
            parts[key] = jnp.dot(done_tile[:, lhs_cols], w_bf[:, cols], preferred_element_type=f32)

        pieces = [functools.partial(piece, lhs_cols, w_bf, (name, hc), slice(hc * half, (hc + 1) * half))
                  for name, lhs_cols, w_bf in (("a", slice(0, D_ATTN), wa_bf), ("b", slice(D_ATTN, D_MODEL), wb_bf))
                  for hc in range(2)]
        mix(mix_tile, pieces)
        for hc in range(2):
            cols = slice(hc * half, (hc + 1) * half)
            op_ref[:, cols] = (_sigmoid(gap_ref[:, cols].astype(f32)) * parts["a", hc]
                               + _sigmoid(gbp_ref[:, cols].astype(f32)) * parts["b", hc]).astype(bf16)

    @pl.when(r == 0)
    def _():
        cast_weights()

        def project_sample():
            os_ref[...] = merged(abs_ref, gas_ref, gbs_ref)

        mix(ab_even, [project_sample])

    @pl.when((r < ROW_TILES) & (r % 2 == 1))
    def _():
        mix_and_project(ab_odd, ab_even)

    @pl.when((r > 0) & (r < ROW_TILES) & (r % 2 == 0))
    def _():
        mix_and_project(ab_even, ab_odd)

    @pl.when(r == ROW_TILES)
    def _():
        cast_weights()
        op_ref[...] = merged(ab_odd if (ROW_TILES - 1) % 2 else ab_even, gap_ref, gbp_ref)


def _mix_merge(proj_p, proj_s, kv_p, ab_s, sinks, gain, bias, w_s, b_s, w_pa, w_pb, layer):
    seq_blocks = SEQ // WINDOW
    ga_tile, gb_tile = GA_COL // TN, GB_COL // TN
    blk_of = lambda r, c: jnp.minimum(r, ROW_TILES - 1) * TILE_BLOCKS + c
    prev_of = lambda r, c: blk_of(r, c) - (_mod_pow2(blk_of(r, c), seq_blocks) > 0).astype(jnp.int32)
    tile_of = lambda r: jnp.maximum(r - 1, 0)
    col_of = lambda r, c: jnp.where(r == 0, 0, c)
    scol_of = lambda r, c: jnp.where(r == 0, c, TILE_BLOCKS - 1)
    full = lambda shape: pl.BlockSpec(shape, lambda r, c: (0,) * len(shape))
    return pl.pallas_call(
        _mix_merge_kernel,
        grid=(ROW_TILES + 1, TILE_BLOCKS),
        in_specs=[pl.BlockSpec(memory_space=pltpu.SMEM)] + _proj_specs(WINDOW, blk_of) + [
            pl.BlockSpec((WINDOW, 2 * D_KV), lambda r, c: (blk_of(r, c), 0)),
            pl.BlockSpec((WINDOW, 2 * D_KV), lambda r, c: (prev_of(r, c), 0)),
            full((1, D_SGU)), full((1, D_SGU)),
            full((N_SGU_GROUPS, CHUNK, CHUNK)), full((CHUNK, N_SGU_GROUPS)),
            full((ROWS_S, D_MODEL)),
            pl.BlockSpec((TM, TN), lambda r, c: (tile_of(r), ga_tile + col_of(r, c))),
            pl.BlockSpec((TM, TN), lambda r, c: (tile_of(r), gb_tile + col_of(r, c))),
            pl.BlockSpec((ROWS_S, TN), lambda r, c: (0, ga_tile + scol_of(r, c))),
            pl.BlockSpec((ROWS_S, TN), lambda r, c: (0, gb_tile + scol_of(r, c))),
            pl.BlockSpec((None, D_ATTN, TN), lambda r, c: (layer, 0, c)),
            pl.BlockSpec((None, D_SGU, TN), lambda r, c: (layer, 0, c)),
        ],
        out_specs=[pl.BlockSpec((TM, TN), lambda r, c: (tile_of(r), col_of(r, c))),
                   pl.BlockSpec((ROWS_S, TN), lambda r, c: (0, scol_of(r, c)))],
        out_shape=[jax.ShapeDtypeStruct((ROWS_P, D_MODEL), bf16), jax.ShapeDtypeStruct((ROWS_S, D_MODEL), bf16)],
        scratch_shapes=[pltpu.VMEM((TM, D_MODEL), bf16), pltpu.VMEM((TM, D_MODEL), bf16),
                        pltpu.VMEM((D_ATTN, TN), bf16), pltpu.VMEM((D_SGU, TN), bf16)],
        compiler_params=_params(2),
        name="mix_merge",
    )(sinks, proj_p, proj_p, proj_p, proj_p, proj_p, kv_p, kv_p, gain, bias, w_s, b_s,
      ab_s, proj_p, proj_p, proj_s, proj_s, w_pa, w_pb)


def _mix_sample_kernel(sink_ref, q_ref, za_ref, u_ref, vb_ref, zb_ref, kvn_ref, ck_ref, cv_ref,
                       gain_ref, bias_ref, ws_ref, bs_ref, ab_ref, vn_ref):
    t = SEQ_GROUP * DEC_SEQ
    n_cache = SEQ_GROUP * WINDOW
    lk = n_cache + WINDOW
    kvn = kvn_ref[...]
    pad = jnp.zeros((WINDOW - t, D_KV), f32)
    k_all = jnp.concatenate([ck_ref[...].reshape(n_cache, D_KV), kvn[:, 0:D_KV], pad], axis=0)
    v_all = jnp.concatenate([cv_ref[...].reshape(n_cache, D_KV), kvn[:, D_KV:], pad], axis=0)
    qi = lax.broadcasted_iota(jnp.int32, (t, lk), 0)
    kj = lax.broadcasted_iota(jnp.int32, (t, lk), 1)
    q_seq, q_tok = _div_pow2(qi, DEC_SEQ), _mod_pow2(qi, DEC_SEQ)
    in_cache = (_div_pow2(kj, WINDOW) == q_seq) & (_mod_pow2(kj, WINDOW) > q_tok)
    kn = jnp.maximum(kj - n_cache, 0)
    in_new = (kn < t) & (_div_pow2(kn, DEC_SEQ) == q_seq) & (_mod_pow2(kn, DEC_SEQ) <= q_tok)
    mask = ((kj < n_cache) & in_cache) | ((kj >= n_cache) & in_new)
    slot_seq = _div_pow2(lax.broadcasted_iota(jnp.int32, (t, n_cache), 0), DEC_SEQ)
    slot = lax.broadcasted_iota(jnp.int32, (t, n_cache), 1) == slot_seq * WINDOW
    key_row = lax.broadcasted_iota(jnp.int32, (lk, 2 * HEAD_DIM), 0)
    live_key = (key_row >= n_cache) | (_mod_pow2(key_row, WINDOW) != 0)
    heads = _attention_stages(q_ref, k_all, v_all, mask, slot, n_cache, live_key, sink_ref, za_ref, ab_ref, t, lk)

    vn = _sgu_norm(vb_ref, gain_ref, bias_ref)
    vn_ref[...] = vn
    r = lax.broadcasted_iota(jnp.int32, (t, t), 0)
    c = lax.broadcasted_iota(jnp.int32, (t, t), 1)
    w_mask = (_div_pow2(r, DEC_SEQ) == _div_pow2(c, DEC_SEQ)) & (c <= r)
    _interleave(heads, _sgu_stages(u_ref, zb_ref, ws_ref, bs_ref, vn, w_mask, ab_ref))


def _proj_specs(rows, row_of):
    return [pl.BlockSpec((rows, D_ATTN), functools.partial(lambda s, *g: (row_of(*g), s), s))
            for s in range(5)]


def _mix_sample(proj, kv, cache_k, cache_v, layer, sinks, gain, bias, w_s, b_s):
    t = SEQ_GROUP * DEC_SEQ
    groups = DEC_BATCH // SEQ_GROUP
    full = lambda shape: pl.BlockSpec(shape, lambda g: (0,) * len(shape))
    cache_spec = pl.BlockSpec((SEQ_GROUP, WINDOW, D_KV), lambda g: (layer * groups + g, 0, 0))
    return pl.pallas_call(
        _mix_sample_kernel,
        grid=(groups,),
        in_specs=[pl.BlockSpec(memory_space=pltpu.SMEM)] + _proj_specs(t, lambda g: g) + [
            pl.BlockSpec((t, 2 * D_KV), lambda g: (g, 0)),
            cache_spec, cache_spec,
            full((1, D_SGU)), full((1, D_SGU)),
            full((N_SGU_GROUPS, t, t)), full((t, N_SGU_GROUPS)),
        ],
        out_specs=[pl.BlockSpec((t, D_MODEL), lambda g: (g, 0)),
                   pl.BlockSpec((t, D_SGU), lambda g: (g, 0))],
        out_shape=[jax.ShapeDtypeStruct((ROWS_S, D_MODEL), bf16),
                   jax.ShapeDtypeStruct((ROWS_S, D_SGU), f32)],
        compiler_params=_params(1),
        name="mix_sample",
    )(sinks, proj, proj, proj, proj, proj, kv, cache_k, cache_v, gain, bias, w_s, b_s)


def _post_norm(x, y, gate, g_ref, b_ref):
    t = x + (gate * (1.0 / ALPHA)) * y.astype(f32)
    mu = jnp.mean(t, axis=1, keepdims=True)
    d = t - mu
    var = jnp.mean(d * d, axis=1, keepdims=True)
    return d * lax.rsqrt(var + LN_EPS / ALPHA ** 2) * g_ref[...] + b_ref[...]


def _ln_kernel(*refs, n, tm):
    xp_ref, xs_ref = refs[0:2]
    y_refs = refs[2:2 + 2 * n]
    gate_refs = refs[2 + 2 * n:2 + 3 * n]
    gb_refs = refs[2 + 3 * n:2 + 5 * n]
    rest = refs[2 + 5 * n:]
    i = pl.program_id(0)
    blocks = ROWS_P // tm

    def stream(x, sample):
        for l in range(n):
            if sample:
                gate = gate_refs[l][0, 0:ROWS_S, :]
            else:
                gate = _prompt_mod_row(gate_refs[l], 0, i, SEQ // tm)
            x = _post_norm(x, y_refs[2 * l + sample][...], gate, gb_refs[2 * l], gb_refs[2 * l + 1])
        return x

    if n == DEPTH:
        xp_out, xs_out = rest

        @pl.when(i < blocks)
        def _():
            xp_out[...] = stream(xp_ref[...], 0)

        @pl.when(i == blocks)
        def _():
            xs_out[...] = stream(xs_ref[...], 1)
    else:
        nxt_ref, w_ref, hp_out, hs_out, kvp_out, kvs_out, w_bf = rest

        @pl.when(i == 0)
        def _():
            w_bf[...] = w_ref[...].astype(bf16)

        @pl.when(i < blocks)
        def _():
            shift = _prompt_mod_row(nxt_ref, 0, i, SEQ // tm)
            scale = _prompt_mod_row(nxt_ref, 1, i, SEQ // tm)
            h = (stream(xp_ref[...], 0) * (1.0 + scale) + shift).astype(bf16)
            hp_out[...] = h
            kvp_out[...] = jnp.dot(h, w_bf[...], preferred_element_type=f32)

        @pl.when(i == blocks)
        def _():
            h = (stream(xs_ref[...], 1) * (1.0 + nxt_ref[1, 0:ROWS_S, :]) + nxt_ref[0, 0:ROWS_S, :]).astype(bf16)
            hs_out[...] = h
            kvs_out[...] = jnp.dot(h, w_bf[...], preferred_element_type=f32)


def _stream(xp, xs, ys, mod, ln_gain, ln_bias, w_in):
    n = len(ys)
    tm = 2 * LN_TM if n == 0 else LN_TM
    blocks = ROWS_P // tm
    row = lambda i: (jnp.minimum(i, blocks - 1), 0)
    once = dict(pipeline_mode=pl.Buffered(1))
    row_spec = pl.BlockSpec((tm, D_MODEL), row)
    sample_in = pl.BlockSpec((ROWS_S, D_MODEL), lambda i: (0, 0), **once)
    sample_out = pl.BlockSpec((ROWS_S, D_MODEL), lambda i: (0, 0))
    in_specs = [row_spec, sample_in] + [row_spec, sample_in] * n
    in_specs += [pl.BlockSpec((None, 1, MOD_ROWS, D_MODEL), functools.partial(lambda l, i: (l, 2, 0, 0), l), **once)
                 for l in range(n)]
    in_specs += [pl.BlockSpec((None, 1, D_MODEL), functools.partial(lambda l, i: (l, 0, 0), l))
                 for l in range(n) for _ in range(2)]
    args = [xp, xs] + [y for pair in ys for y in pair] + [mod] * n
    for l in range(n):
        args += [ln_gain.reshape(DEPTH, 1, D_MODEL), ln_bias.reshape(DEPTH, 1, D_MODEL)]
    if n == DEPTH:
        out_specs = [row_spec, sample_out]
        out_shape = [jax.ShapeDtypeStruct((ROWS_P, D_MODEL), f32), jax.ShapeDtypeStruct((ROWS_S, D_MODEL), f32)]
        scratch = []
    else:
        in_specs += [pl.BlockSpec((None, 2, MOD_ROWS, D_MODEL), lambda i: (n, 0, 0, 0), **once),
                     pl.BlockSpec((None, D_MODEL, TN), lambda i: (n, 0, KV_TILE), **once)]
        args += [mod, w_in]
        out_specs = [row_spec, sample_out, pl.BlockSpec((tm, 2 * D_KV), row),
                     pl.BlockSpec((ROWS_S, 2 * D_KV), lambda i: (0, 0))]
        out_shape = [jax.ShapeDtypeStruct((ROWS_P, D_MODEL), bf16), jax.ShapeDtypeStruct((ROWS_S, D_MODEL), bf16),
                     jax.ShapeDtypeStruct((ROWS_P, 2 * D_KV), f32), jax.ShapeDtypeStruct((ROWS_S, 2 * D_KV), f32)]
        scratch = [pltpu.VMEM((D_MODEL, TN), bf16)]
    return pl.pallas_call(
        functools.partial(_ln_kernel, n=n, tm=tm),
        grid=(blocks + 1,),
        in_specs=in_specs,
        out_specs=out_specs,
        out_shape=out_shape,
        scratch_shapes=scratch,
        compiler_params=_params(1),
        name=f"stream{n}",
    )(*args)


def kernel(x_prompt, x_sample, cache_k, cache_v, c_prompt, c_sample, w_ada, b_ada, w_in, attn_sinks,
           sgu_ln_gain, sgu_ln_bias, sgu_w_s, sgu_b_s, w_pa, w_pb, w_o, ln_gain, ln_bias):
    xp = x_prompt.reshape(ROWS_P, D_MODEL)
    xs = x_sample.reshape(ROWS_S, D_MODEL)
    c_all = jnp.concatenate([jnp.repeat(c_sample, DEC_SEQ, axis=0), c_prompt,
                             jnp.zeros((MOD_ROWS - ROWS_S - BATCH, D_MODEL), f32)], axis=0)
    mod = _ada(c_all, w_ada, b_ada)
    ck = cache_k.reshape(DEPTH * DEC_BATCH, WINDOW, D_KV)
    cv = cache_v.reshape(DEPTH * DEC_BATCH, WINDOW, D_KV)

    win_k, win_v, new_k, new_v, sgu_v, ys = [], [], [], [], [], []
    for l in range(DEPTH):
        hp, hs, kv_p, kv_s = _stream(xp, xs, ys, mod, ln_gain, ln_bias, w_in)
        proj_p, proj_s = _matmul(hp, hs, w_in, l, PROJ_COLS // TN, _skip_kv_tile, bf16, "proj")
        gain = sgu_ln_gain[l].reshape(1, D_SGU)
        bias = sgu_ln_bias[l].reshape(1, D_SGU)
        w_small = jnp.tile(sgu_w_s[l][:, :DEC_SEQ, :DEC_SEQ], (1, SEQ_GROUP, SEQ_GROUP))
        b_small = jnp.tile(sgu_b_s[l][:, :DEC_SEQ].T, (SEQ_GROUP, 1))
        ab_s, vn_s = _mix_sample(proj_s, kv_s, ck, cv, l, attn_sinks[l], gain, bias, w_small, b_small)
        m_p, m_s = _mix_merge(proj_p, proj_s, kv_p, ab_s, attn_sinks[l], gain, bias, sgu_w_s[l], sgu_b_s[l].T,
                              w_pa, w_pb, l)
        ys.append(_matmul(m_p, m_s, w_o, l, D_MODEL // TN, lambda j: j, bf16, "out"))

        kv_win = kv_p.reshape(BATCH, SEQ, 2 * D_KV)[:, SEQ - WINDOW:]
        kv_win = kv_win.reshape(BATCH, WINDOW, 2, N_KV_HEADS, HEAD_DIM)
        kv_new = kv_s.reshape(DEC_BATCH, DEC_SEQ, 2, N_KV_HEADS, HEAD_DIM)
        win_k.append(kv_win[:, :, 0])
        win_v.append(kv_win[:, :, 1])
        new_k.append(kv_new[:, :, 0])
        new_v.append(kv_new[:, :, 1])
        sgu_v.append(vn_s.reshape(DEC_BATCH, DEC_SEQ, D_SGU))

    xp, xs = _stream(xp, xs, ys, mod, ln_gain, ln_bias, w_in)
    return (xp.reshape(BATCH, SEQ, D_MODEL), xs.reshape(DEC_BATCH, DEC_SEQ, D_MODEL),
            jnp.stack(win_k), jnp.stack(win_v), jnp.stack(new_k), jnp.stack(new_v), jnp.stack(sgu_v))
```

```python
import functools

import jax
import jax.numpy as jnp
from jax import lax
from jax.experimental import pallas as pl
from jax.experimental.pallas import tpu as pltpu

D_MODEL = 4096
BATCH = 4
SEQ = 2048
DEPTH = 2
DEC_BATCH = 32
DEC_SEQ = 4
HEAD_DIM = 64
D_ATTN = D_MODEL // 2
N_Q_HEADS = D_ATTN // HEAD_DIM
N_KV_HEADS = N_Q_HEADS // 8
GQA_GROUP = N_Q_HEADS // N_KV_HEADS
D_KV = N_KV_HEADS * HEAD_DIM
WINDOW = 128
D_SGU = D_MODEL // 2
N_SGU_GROUPS = 8
SGU_GROUP_DIM = D_SGU // N_SGU_GROUPS
CHUNK = 128
ALPHA = (2 * DEPTH) ** 0.25
LN_EPS = 1e-5
IN_COLS = D_ATTN + 2 * D_KV + D_ATTN + 3 * D_SGU + 2 * D_MODEL

ROWS_P = BATCH * SEQ
ROWS_S = DEC_BATCH * DEC_SEQ
MOD_ROWS = ROWS_S + 8
PROJ_COLS = IN_COLS - 2 * D_KV

TN = 512
TM = 1024
MM_TM = 2048
ADA_TN = 1024
KV_TILE = D_ATTN // TN
LN_TM = 256
SEQ_GROUP = 8
NEG = -1e30
LOG2E = 1.4426950408889634

VMEM_LIMIT = 56 * 1024 * 1024
STREAM_VMEM_LIMIT = 62 * 1024 * 1024

bf16 = jnp.bfloat16
f32 = jnp.float32


def _params(n_axes, vmem=VMEM_LIMIT):
    return pltpu.CompilerParams(dimension_semantics=("arbitrary",) * n_axes, vmem_limit_bytes=vmem)


def _sigmoid(x):
    return 0.5 + 0.5 * jnp.tanh(0.5 * x)


def _silu(x):
    half = 0.5 * x
    return half + half * jnp.tanh(half)


def _div_pow2(x, n):
    assert n & (n - 1) == 0
    return x >> (n.bit_length() - 1)


def _mod_pow2(x, n):
    assert n & (n - 1) == 0
    return x & (n - 1)


def _ada_kernel(c_ref, w_ref, b_ref, o_ref):
    a = _silu(c_ref[...]).astype(bf16)
    o_ref[...] = jnp.dot(a, w_ref[...].astype(bf16), preferred_element_type=f32) + b_ref[...]


def _ada(c_all, w_ada, b_ada):
    tiles_per_part = D_MODEL // ADA_TN
    return pl.pallas_call(
        _ada_kernel,
        grid=(DEPTH, 3 * tiles_per_part),
        in_specs=[
            pl.BlockSpec((MOD_ROWS, D_MODEL), lambda l, j: (0, 0)),
            pl.BlockSpec((None, D_MODEL, ADA_TN), lambda l, j: (l, 0, j)),
            pl.BlockSpec((None, 1, ADA_TN), lambda l, j: (l, 0, j)),
        ],
        out_specs=pl.BlockSpec((None, None, MOD_ROWS, ADA_TN),
                               lambda l, j: (l, j // tiles_per_part, 0, j % tiles_per_part)),
        out_shape=jax.ShapeDtypeStruct((DEPTH, 3, MOD_ROWS, D_MODEL), f32),
        compiler_params=_params(2),
        name="ada",
    )(c_all, w_ada, b_ada.reshape(DEPTH, 1, 3 * D_MODEL))


def _prompt_mod_row(ref, part, blk, blocks_per_batch):
    return ref[part, pl.ds(ROWS_S + blk // blocks_per_batch, 1), :]


def _matmul_kernel(xp_ref, xs_ref, w_ref, op_ref, os_ref, w_bf):
    @pl.when(pl.program_id(1) == 0)
    def _():
        w_bf[...] = w_ref[...].astype(bf16)
        os_ref[...] = jnp.dot(xs_ref[...], w_bf[...], preferred_element_type=f32).astype(os_ref.dtype)

    op_ref[...] = jnp.dot(xp_ref[...], w_bf[...], preferred_element_type=f32).astype(op_ref.dtype)


def _matmul(xp, xs, w, layer, n_tiles, w_tile_of, out_dtype, name):
    k = xp.shape[1]
    out_bytes = jnp.dtype(out_dtype).itemsize
    vmem = (2 * (MM_TM + ROWS_S) * k * 2 + 2 * k * TN * 4 + k * TN * 2 + 2 * (MM_TM + ROWS_S) * TN * out_bytes
            + (4 << 20))
    return pl.pallas_call(
        _matmul_kernel,
        grid=(n_tiles, ROWS_P // MM_TM),
        in_specs=[pl.BlockSpec((MM_TM, k), lambda j, i: (i, 0)),
                  pl.BlockSpec((ROWS_S, k), lambda j, i: (0, 0)),
                  pl.BlockSpec((None, k, TN), lambda j, i: (layer, 0, w_tile_of(j)))],
        out_specs=[pl.BlockSpec((MM_TM, TN), lambda j, i: (i, j)),
                   pl.BlockSpec((ROWS_S, TN), lambda j, i: (0, j))],
        out_shape=[jax.ShapeDtypeStruct((ROWS_P, n_tiles * TN), out_dtype),
                   jax.ShapeDtypeStruct((ROWS_S, n_tiles * TN), out_dtype)],
        scratch_shapes=[pltpu.VMEM((k, TN), bf16)],
        compiler_params=_params(2, vmem),
        name=name,
    )(xp, xs, w)


def _skip_kv_tile(c):
    return c + (c >= KV_TILE).astype(jnp.int32)


GA_COL = 5 * D_ATTN
GB_COL = GA_COL + D_MODEL


def _attention_stages(q_ref, k_all, v_all, mask, slot, slot_cols, live_key, sink_ref, za_ref, ab_ref, t, lk):
    lane = lax.broadcasted_iota(jnp.int32, (lk, 2 * HEAD_DIM), 1)
    low, high = lane < HEAD_DIM, lane >= HEAD_DIM
    ones2 = jnp.concatenate([low, high], axis=0).astype(f32).astype(bf16)
    nt = (((1,), (1,)), ((), ()))
    pairs = GQA_GROUP // 2
    k_scale = HEAD_DIM ** -0.5 * LOG2E

    def head(kv):
        cols = slice((kv // 2) * 2 * HEAD_DIM, (kv // 2 + 1) * 2 * HEAD_DIM)
        kp, vp = k_all[:, cols] * k_scale, v_all[:, cols]
        keep = (low if kv % 2 == 0 else high) & live_key
        k_own = jnp.where(keep, kp, 0.0)
        v_own = jnp.where(keep, vp, 0.0)
        k_oth = pltpu.roll(k_own, HEAD_DIM, 1)
        v_oth = pltpu.roll(v_own, HEAD_DIM, 1)
        if kv % 2 == 0:
            k2 = jnp.concatenate([k_own, k_oth], axis=0).astype(bf16)
            v2 = jnp.concatenate([v_own, v_oth], axis=0).astype(bf16)
        else:
            k2 = jnp.concatenate([k_oth, k_own], axis=0).astype(bf16)
            v2 = jnp.concatenate([v_oth, v_own], axis=0).astype(bf16)
        v2 = jnp.concatenate([v2, ones2], axis=1)
        base = kv * GQA_GROUP * HEAD_DIM
        q4 = jnp.concatenate(
            [q_ref[:, base + p * 128: base + (p + 1) * 128] for p in range(pairs)], axis=0)
        s_all = lax.dot_general(q4, k2, nt, preferred_element_type=f32)
        rows = []
        for p in range(pairs):
            halves = []
            for h in range(2):
                sink = sink_ref[kv * GQA_GROUP + 2 * p + h] * LOG2E
                fill = jnp.where(slot, sink, NEG)
                if slot_cols < lk:
                    fill = jnp.concatenate([fill, jnp.full((t, lk - slot_cols), NEG, f32)], axis=1)
                s = jnp.where(mask, s_all[p * t:(p + 1) * t, h * lk:(h + 1) * lk], fill)
                m = jnp.max(s, axis=1, keepdims=True)
                halves.append(jnp.exp2(s - m).astype(bf16))
            rows.append(jnp.concatenate(halves, axis=1))
        probs = jnp.concatenate(rows, axis=0)
        o = jnp.dot(probs, v2, preferred_element_type=f32)
        o = o[:, 0:128] / o[:, 128:256]
        for p in range(pairs):
            cols = slice(base + p * 128, base + (p + 1) * 128)
            ab_ref[:, cols] = (o[p * t:(p + 1) * t, :] * _silu(za_ref[:, cols].astype(f32))).astype(bf16)

    return [functools.partial(head, kv) for kv in range(N_KV_HEADS)]


def _sgu_norm(vb_ref, gain_ref, bias_ref):
    vb = vb_ref[...].astype(f32)
    mu = jnp.mean(vb, axis=1, keepdims=True)
    d = vb - mu
    var = jnp.mean(d * d, axis=1, keepdims=True)
    return d * lax.rsqrt(var + LN_EPS) * gain_ref[...] + bias_ref[...]


def _sgu_stages(u_ref, zb_ref, ws_ref, bs_ref, vn, w_mask, ab_ref):
    vn_bf = vn.astype(bf16)

    def group(g):
        cols = slice(g * SGU_GROUP_DIM, (g + 1) * SGU_GROUP_DIM)
        w_g = jnp.where(w_mask, ws_ref[g], 0.0).astype(bf16)
        s = jnp.dot(w_g, vn_bf[:, cols], preferred_element_type=f32) + bs_ref[:, g:g + 1]
        out_b = u_ref[:, cols].astype(f32) * s * _silu(zb_ref[:, cols].astype(f32))
        ab_ref[:, D_ATTN + g * SGU_GROUP_DIM: D_ATTN + (g + 1) * SGU_GROUP_DIM] = out_b.astype(bf16)

    return [functools.partial(group, g) for g in range(N_SGU_GROUPS)]


def _interleave(attention, sgu, others=()):
    per_head = len(sgu) // len(attention)
    others = list(others)
    if others:
        others.pop(0)()
    for kv, head in enumerate(attention):
        head()
        if others:
            others.pop(0)()
        for stage in sgu[kv * per_head:(kv + 1) * per_head]:
            stage()


def _mix_prompt_block(n, sink_ref, q_ref, za_ref, u_ref, vb_ref, zb_ref, kvc_ref, kvp_ref,
                      gain_ref, bias_ref, ws_ref, bs_ref, ab_ref, others=()):
    vn = _sgu_norm(vb_ref, gain_ref, bias_ref)
    kvc, kvp = kvc_ref[...], kvp_ref[...]
    k_all = jnp.concatenate([kvp[:, 0:D_KV], kvc[:, 0:D_KV]], axis=0)
    v_all = jnp.concatenate([kvp[:, D_KV:], kvc[:, D_KV:]], axis=0)
    qi = lax.broadcasted_iota(jnp.int32, (WINDOW, 2 * WINDOW), 0)
    kj = lax.broadcasted_iota(jnp.int32, (WINDOW, 2 * WINDOW), 1)
    no_prev = jnp.where(n > 0, 0, 2 * WINDOW)
    mask = ((kj < WINDOW) & (kj > qi + no_prev)) | ((kj >= WINDOW) & (kj - WINDOW <= qi))
    slot = lax.broadcasted_iota(jnp.int32, (WINDOW, WINDOW), 1) == 0
    live_key = lax.broadcasted_iota(jnp.int32, (2 * WINDOW, 2 * HEAD_DIM), 0) != 0
    heads = _attention_stages(q_ref, k_all, v_all, mask, slot, WINDOW, live_key, sink_ref, za_ref, ab_ref,
                              WINDOW, 2 * WINDOW)
    r = lax.broadcasted_iota(jnp.int32, (CHUNK, CHUNK), 0)
    c = lax.broadcasted_iota(jnp.int32, (CHUNK, CHUNK), 1)
    _interleave(heads, _sgu_stages(u_ref, zb_ref, ws_ref, bs_ref, vn, c <= r, ab_ref), others)


ROW_TILES = ROWS_P // TM
TILE_BLOCKS = TM // WINDOW
assert TILE_BLOCKS == D_MODEL // TN


def _mix_merge_kernel(sink_ref, q_ref, za_ref, u_ref, vb_ref, zb_ref, kvc_ref, kvp_ref,
                      gain_ref, bias_ref, ws_ref, bs_ref,
                      abs_ref, gap_ref, gbp_ref, gas_ref, gbs_ref, wa_ref, wb_ref,
                      op_ref, os_ref, ab_even, ab_odd, wa_bf, wb_bf):
    r, c = pl.program_id(0), pl.program_id(1)
    blk = jnp.minimum(r, ROW_TILES - 1) * TILE_BLOCKS + c

    def cast_weights():
        wa_bf[...] = wa_ref[...].astype(bf16)
        wb_bf[...] = wb_ref[...].astype(bf16)

    def mix(ab_tile, others=()):
        mix_out = ab_tile.at[pl.ds(pl.multiple_of(c * WINDOW, WINDOW), WINDOW), :]
        _mix_prompt_block(_mod_pow2(blk, SEQ // WINDOW), sink_ref, q_ref, za_ref, u_ref, vb_ref, zb_ref,
                          kvc_ref, kvp_ref, gain_ref, bias_ref, ws_ref, bs_ref, mix_out, others)

    def merged(ab_ref, ga_ref, gb_ref):
        ya = jnp.dot(ab_ref[:, 0:D_ATTN], wa_bf[...], preferred_element_type=f32)
        yb = jnp.dot(ab_ref[:, D_ATTN:], wb_bf[...], preferred_element_type=f32)
        return (_sigmoid(ga_ref[...].astype(f32)) * ya + _sigmoid(gb_ref[...].astype(f32)) * yb).astype(bf16)

    def mix_and_project(mix_tile, done_tile):
        cast_weights()
        half = TN // 2
        parts = {}

        def piece(lhs_cols, w_bf, key, cols):
            parts[key] = jnp.dot(done_tile[:, lhs_cols], w_bf[:, cols], preferred_element_type=f32)

        pieces = [functools.partial(piece, lhs_cols, w_bf, (name, hc), slice(hc * half, (hc + 1) * half))
                  for name, lhs_cols, w_bf in (("a", slice(0, D_ATTN), wa_bf), ("b", slice(D_ATTN, D_MODEL), wb_bf))
                  for hc in range(2)]
        mix(mix_tile, pieces)
        for hc in range(2):
            cols = slice(hc * half, (hc + 1) * half)
            op_ref[:, cols] = (_sigmoid(gap_ref[:, cols].astype(f32)) * parts["a", hc]
                               + _sigmoid(gbp_ref[:, cols].astype(f32)) * parts["b", hc]).astype(bf16)

    @pl.when(r == 0)
    def _():
        cast_weights()

        def project_sample():
            os_ref[...] = merged(abs_ref, gas_ref, gbs_ref)

        mix(ab_even, [project_sample])

    @pl.when((r < ROW_TILES) & (r % 2 == 1))
    def _():
        mix_and_project(ab_odd, ab_even)

    @pl.when((r > 0) & (r < ROW_TILES) & (r % 2 == 0))
    def _():
        mix_and_project(ab_even, ab_odd)

    @pl.when(r == ROW_TILES)
    def _():
        cast_weights()
        op_ref[...] = merged(ab_odd if (ROW_TILES - 1) % 2 else ab_even, gap_ref, gbp_ref)


def _mix_merge(proj_p, proj_s, kv_p, ab_s, sinks, gain, bias, w_s, b_s, w_pa, w_pb, layer):
    seq_blocks = SEQ // WINDOW
    ga_tile, gb_tile = GA_COL // TN, GB_COL // TN
    blk_of = lambda r, c: jnp.minimum(r, ROW_TILES - 1) * TILE_BLOCKS + c
    prev_of = lambda r, c: blk_of(r, c) - (_mod_pow2(blk_of(r, c), seq_blocks) > 0).astype(jnp.int32)
    tile_of = lambda r: jnp.maximum(r - 1, 0)
    col_of = lambda r, c: jnp.where(r == 0, 0, c)
    scol_of = lambda r, c: jnp.where(r == 0, c, TILE_BLOCKS - 1)
    full = lambda shape: pl.BlockSpec(shape, lambda r, c: (0,) * len(shape))
    return pl.pallas_call(
        _mix_merge_kernel,
        grid=(ROW_TILES + 1, TILE_BLOCKS),
        in_specs=[pl.BlockSpec(memory_space=pltpu.SMEM)] + _proj_specs(WINDOW, blk_of) + [
            pl.BlockSpec((WINDOW, 2 * D_KV), lambda r, c: (blk_of(r, c), 0)),
            pl.BlockSpec((WINDOW, 2 * D_KV), lambda r, c: (prev_of(r, c), 0)),
            full((1, D_SGU)), full((1, D_SGU)),
            full((N_SGU_GROUPS, CHUNK, CHUNK)), full((CHUNK, N_SGU_GROUPS)),
            full((ROWS_S, D_MODEL)),
            pl.BlockSpec((TM, TN), lambda r, c: (tile_of(r), ga_tile + col_of(r, c))),
            pl.BlockSpec((TM, TN), lambda r, c: (tile_of(r), gb_tile + col_of(r, c))),
            pl.BlockSpec((ROWS_S, TN), lambda r, c: (0, ga_tile + scol_of(r, c))),
            pl.BlockSpec((ROWS_S, TN), lambda r, c: (0, gb_tile + scol_of(r, c))),
            pl.BlockSpec((None, D_ATTN, TN), lambda r, c: (layer, 0, c)),
            pl.BlockSpec((None, D_SGU, TN), lambda r, c: (layer, 0, c)),
        ],
        out_specs=[pl.BlockSpec((TM, TN), lambda r, c: (tile_of(r), col_of(r, c))),
                   pl.BlockSpec((ROWS_S, TN), lambda r, c: (0, scol_of(r, c)))],
        out_shape=[jax.ShapeDtypeStruct((ROWS_P, D_MODEL), bf16), jax.ShapeDtypeStruct((ROWS_S, D_MODEL), bf16)],
        scratch_shapes=[pltpu.VMEM((TM, D_MODEL), bf16), pltpu.VMEM((TM, D_MODEL), bf16),
                        pltpu.VMEM((D_ATTN, TN), bf16), pltpu.VMEM((D_SGU, TN), bf16)],
        compiler_params=_params(2),
        name="mix_merge",
    )(sinks, proj_p, proj_p, proj_p, proj_p, proj_p, kv_p, kv_p, gain, bias, w_s, b_s,
      ab_s, proj_p, proj_p, proj_s, proj_s, w_pa, w_pb)


def _mix_sample_kernel(sink_ref, q_ref, za_ref, u_ref, vb_ref, zb_ref, kvn_ref, ck_ref, cv_ref,
                       gain_ref, bias_ref, ws_ref, bs_ref, ab_ref, vn_ref):
    t = SEQ_GROUP * DEC_SEQ
    n_cache = SEQ_GROUP * WINDOW
    lk = n_cache + WINDOW
    kvn = kvn_ref[...]
    pad = jnp.zeros((WINDOW - t, D_KV), f32)
    k_all = jnp.concatenate([ck_ref[...].reshape(n_cache, D_KV), kvn[:, 0:D_KV], pad], axis=0)
    v_all = jnp.concatenate([cv_ref[...].reshape(n_cache, D_KV), kvn[:, D_KV:], pad], axis=0)
    qi = lax.broadcasted_iota(jnp.int32, (t, lk), 0)
    kj = lax.broadcasted_iota(jnp.int32, (t, lk), 1)
    q_seq, q_tok = _div_pow2(qi, DEC_SEQ), _mod_pow2(qi, DEC_SEQ)
    in_cache = (_div_pow2(kj, WINDOW) == q_seq) & (_mod_pow2(kj, WINDOW) > q_tok)
    kn = jnp.maximum(kj - n_cache, 0)
    in_new = (kn < t) & (_div_pow2(kn, DEC_SEQ) == q_seq) & (_mod_pow2(kn, DEC_SEQ) <= q_tok)
    mask = ((kj < n_cache) & in_cache) | ((kj >= n_cache) & in_new)
    slot_seq = _div_pow2(lax.broadcasted_iota(jnp.int32, (t, n_cache), 0), DEC_SEQ)
    slot = lax.broadcasted_iota(jnp.int32, (t, n_cache), 1) == slot_seq * WINDOW
    key_row = lax.broadcasted_iota(jnp.int32, (lk, 2 * HEAD_DIM), 0)
    live_key = (key_row >= n_cache) | (_mod_pow2(key_row, WINDOW) != 0)
    heads = _attention_stages(q_ref, k_all, v_all, mask, slot, n_cache, live_key, sink_ref, za_ref, ab_ref, t, lk)

    vn = _sgu_norm(vb_ref, gain_ref, bias_ref)
    vn_ref[...] = vn
    r = lax.broadcasted_iota(jnp.int32, (t, t), 0)
    c = lax.broadcasted_iota(jnp.int32, (t, t), 1)
    w_mask = (_div_pow2(r, DEC_SEQ) == _div_pow2(c, DEC_SEQ)) & (c <= r)
    _interleave(heads, _sgu_stages(u_ref, zb_ref, ws_ref, bs_ref, vn, w_mask, ab_ref))


def _proj_specs(rows, row_of):
    return [pl.BlockSpec((rows, D_ATTN), functools.partial(lambda s, *g: (row_of(*g), s), s))
            for s in range(5)]


def _mix_sample(proj, kv, cache_k, cache_v, layer, sinks, gain, bias, w_s, b_s):
    t = SEQ_GROUP * DEC_SEQ
    groups = DEC_BATCH // SEQ_GROUP
    full = lambda shape: pl.BlockSpec(shape, lambda g: (0,) * len(shape))
    cache_spec = pl.BlockSpec((SEQ_GROUP, WINDOW, D_KV), lambda g: (layer * groups + g, 0, 0))
    return pl.pallas_call(
        _mix_sample_kernel,
        grid=(groups,),
        in_specs=[pl.BlockSpec(memory_space=pltpu.SMEM)] + _proj_specs(t, lambda g: g) + [
            pl.BlockSpec((t, 2 * D_KV), lambda g: (g, 0)),
            cache_spec, cache_spec,
            full((1, D_SGU)), full((1, D_SGU)),
            full((N_SGU_GROUPS, t, t)), full((t, N_SGU_GROUPS)),
        ],
        out_specs=[pl.BlockSpec((t, D_MODEL), lambda g: (g, 0)),
                   pl.BlockSpec((t, D_SGU), lambda g: (g, 0))],
        out_shape=[jax.ShapeDtypeStruct((ROWS_S, D_MODEL), bf16),
                   jax.ShapeDtypeStruct((ROWS_S, D_SGU), f32)],
        compiler_params=_params(1),
        name="mix_sample",
    )(sinks, proj, proj, proj, proj, proj, kv, cache_k, cache_v, gain, bias, w_s, b_s)


def _post_norm(x, y, gate, g_ref, b_ref):
    t = x + (gate * (1.0 / ALPHA)) * y.astype(f32)
    mu = jnp.mean(t, axis=1, keepdims=True)
    d = t - mu
    var = jnp.mean(d * d, axis=1, keepdims=True)
    return d * lax.rsqrt(var + LN_EPS / ALPHA ** 2) * g_ref[...] + b_ref[...]


def _ln_kernel(*refs, n, tm):
    xp_ref, xs_ref = refs[0:2]
    y_refs = refs[2:2 + 2 * n]
    gate_refs = refs[2 + 2 * n:2 + 3 * n]
    gb_refs = refs[2 + 3 * n:2 + 5 * n]
    rest = refs[2 + 5 * n:]
    i = pl.program_id(0)
    blocks = ROWS_P // tm

    def stream(x, sample):
        for l in range(n):
            if sample:
                gate = gate_refs[l][0, 0:ROWS_S, :]
            else:
                gate = _prompt_mod_row(gate_refs[l], 0, i, SEQ // tm)
            x = _post_norm(x, y_refs[2 * l + sample][...], gate, gb_refs[2 * l], gb_refs[2 * l + 1])
        return x

    if n == DEPTH:
        xp_out, xs_out = rest

        @pl.when(i < blocks)
        def _():
            xp_out[...] = stream(xp_ref[...], 0)

        @pl.when(i == blocks)
        def _():
            xs_out[...] = stream(xs_ref[...], 1)
    else:
        nxt_ref, w_ref, hp_out, hs_out, kvp_out, kvs_out, w_bf = rest

        @pl.when(i == 0)
        def _():
            w_bf[...] = w_ref[...].astype(bf16)

        @pl.when(i < blocks)
        def _():
            shift = _prompt_mod_row(nxt_ref, 0, i, SEQ // tm)
            scale = _prompt_mod_row(nxt_ref, 1, i, SEQ // tm)
            h = (stream(xp_ref[...], 0) * (1.0 + scale) + shift).astype(bf16)
            hp_out[...] = h
            kvp_out[...] = jnp.dot(h, w_bf[...], preferred_element_type=f32)

        @pl.when(i == blocks)
        def _():
            h = (stream(xs_ref[...], 1) * (1.0 + nxt_ref[1, 0:ROWS_S, :]) + nxt_ref[0, 0:ROWS_S, :]).astype(bf16)
            hs_out[...] = h
            kvs_out[...] = jnp.dot(h, w_bf[...], preferred_element_type=f32)


def _stream(xp, xs, ys, mod, ln_gain, ln_bias, w_in):
    n = len(ys)
    tm = 2 * LN_TM if n == 0 else LN_TM
    blocks = ROWS_P // tm
    row = lambda i: (jnp.minimum(i, blocks - 1), 0)
    once = dict(pipeline_mode=pl.Buffered(1))
    row_spec = pl.BlockSpec((tm, D_MODEL), row)
    sample_in = pl.BlockSpec((ROWS_S, D_MODEL), lambda i: (0, 0), **once)
    sample_out = pl.BlockSpec((ROWS_S, D_MODEL), lambda i: (0, 0))
    in_specs = [row_spec, sample_in] + [row_spec, sample_in] * n
    in_specs += [pl.BlockSpec((None, 1, MOD_ROWS, D_MODEL), functools.partial(lambda l, i: (l, 2, 0, 0), l), **once)
                 for l in range(n)]
    in_specs += [pl.BlockSpec((None, 1, D_MODEL), functools.partial(lambda l, i: (l, 0, 0), l))
                 for l in range(n) for _ in range(2)]
    args = [xp, xs] + [y for pair in ys for y in pair] + [mod] * n
    for l in range(n):
        args += [ln_gain.reshape(DEPTH, 1, D_MODEL), ln_bias.reshape(DEPTH, 1, D_MODEL)]
    if n == DEPTH:
        out_specs = [row_spec, sample_out]
        out_shape = [jax.ShapeDtypeStruct((ROWS_P, D_MODEL), f32), jax.ShapeDtypeStruct((ROWS_S, D_MODEL), f32)]
        scratch = []
    else:
        in_specs += [pl.BlockSpec((None, 2, MOD_ROWS, D_MODEL), lambda i: (n, 0, 0, 0), **once),
                     pl.BlockSpec((None, D_MODEL, TN), lambda i: (n, 0, KV_TILE), **once)]
        args += [mod, w_in]
        out_specs = [row_spec, sample_out, pl.BlockSpec((tm, 2 * D_KV), row),
                     pl.BlockSpec((ROWS_S, 2 * D_KV), lambda i: (0, 0))]
        out_shape = [jax.ShapeDtypeStruct((ROWS_P, D_MODEL), bf16), jax.ShapeDtypeStruct((ROWS_S, D_MODEL), bf16),
                     jax.ShapeDtypeStruct((ROWS_P, 2 * D_KV), f32), jax.ShapeDtypeStruct((ROWS_S, 2 * D_KV), f32)]
        scratch = [pltpu.VMEM((D_MODEL, TN), bf16)]
    return pl.pallas_call(
        functools.partial(_ln_kernel, n=n, tm=tm),
        grid=(blocks + 1,),
        in_specs=in_specs,
        out_specs=out_specs,
        out_shape=out_shape,
        scratch_shapes=scratch,
        compiler_params=_params(1, STREAM_VMEM_LIMIT),
        name=f"stream{n}",
    )(*args)


RING = 3


def _final_kernel(*refs):
    n = DEPTH
    xp_hbm, xs_ref = refs[0:2]
    y_refs = refs[2:2 + 2 * n]
    gate_refs = refs[2 + 2 * n:2 + 3 * n]
    gb_refs = refs[2 + 3 * n:2 + 5 * n]
    xp_out_hbm, xs_out = refs[2 + 5 * n:4 + 5 * n]
    x_buf, o_buf = refs[4 + 5 * n:6 + 5 * n]
    y_bufs = refs[6 + 5 * n:6 + 6 * n]
    in_sem, out_sem = refs[6 + 6 * n:]
    blocks = ROWS_P // LN_TM

    def rows(blk):
        return pl.ds(pl.multiple_of(blk * LN_TM, LN_TM), LN_TM)

    def in_copies(blk, slot):
        copies = [pltpu.make_async_copy(xp_hbm.at[rows(blk)], x_buf.at[slot], in_sem.at[0, slot])]
        for l in range(n):
            copies.append(pltpu.make_async_copy(y_refs[2 * l].at[rows(blk)], y_bufs[l].at[slot],
                                                in_sem.at[1 + l, slot]))
        return copies

    def out_copy(blk, slot):
        return pltpu.make_async_copy(o_buf.at[slot], xp_out_hbm.at[rows(blk)], out_sem.at[slot])

    for blk in range(RING - 1):
        for c in in_copies(blk, blk):
            c.start()

    def body(i, carry):
        slot, o_slot = i % RING, i % 2

        @pl.when(i + RING - 1 < blocks)
        def _():
            for c in in_copies(i + RING - 1, (i + RING - 1) % RING):
                c.start()

        for c in in_copies(i, slot):
            c.wait()

        @pl.when(i >= 2)
        def _():
            out_copy(i - 2, o_slot).wait()

        x = x_buf[slot]
        for l in range(n):
            gate = _prompt_mod_row(gate_refs[l], 0, i, SEQ // LN_TM)
            x = _post_norm(x, y_bufs[l][slot], gate, gb_refs[2 * l], gb_refs[2 * l + 1])
        o_buf[o_slot] = x
        out_copy(i, o_slot).start()
        return carry

    lax.fori_loop(0, blocks, body, 0)
    for blk in range(blocks - 2, blocks):
        out_copy(blk, blk % 2).wait()

    x = xs_ref[...]
    for l in range(n):
        x = _post_norm(x, y_refs[2 * l + 1][...], gate_refs[l][0, 0:ROWS_S, :], gb_refs[2 * l], gb_refs[2 * l + 1])
    xs_out[...] = x


def _final_stream(xp, xs, ys, mod, ln_gain, ln_bias):
    n = DEPTH
    hbm = pl.BlockSpec(memory_space=pl.ANY)
    whole = lambda shape: pl.BlockSpec(shape, lambda i: (0,) * len(shape))
    in_specs = [hbm, whole((ROWS_S, D_MODEL))] + [hbm, whole((ROWS_S, D_MODEL))] * n
    in_specs += [pl.BlockSpec((None, 1, MOD_ROWS, D_MODEL), functools.partial(lambda l, i: (l, 2, 0, 0), l))
                 for l in range(n)]
    in_specs += [pl.BlockSpec((None, 1, D_MODEL), functools.partial(lambda l, i: (l, 0, 0), l))
                 for l in range(n) for _ in range(2)]
    args = [xp, xs] + [y for pair in ys for y in pair] + [mod] * n
    for l in range(n):
        args += [ln_gain.reshape(DEPTH, 1, D_MODEL), ln_bias.reshape(DEPTH, 1, D_MODEL)]
    return pl.pallas_call(
        _final_kernel,
        grid=(1,),
        in_specs=in_specs,
        out_specs=[hbm, whole((ROWS_S, D_MODEL))],
        out_shape=[jax.ShapeDtypeStruct((ROWS_P, D_MODEL), f32), jax.ShapeDtypeStruct((ROWS_S, D_MODEL), f32)],
        scratch_shapes=[pltpu.VMEM((RING, LN_TM, D_MODEL), f32), pltpu.VMEM((2, LN_TM, D_MODEL), f32)]
        + [pltpu.VMEM((RING, LN_TM, D_MODEL), bf16) for _ in range(n)]
        + [pltpu.SemaphoreType.DMA((1 + n, RING)), pltpu.SemaphoreType.DMA((2,))],
        compiler_params=_params(1),
        name="stream_final",
    )(*args)


def kernel(x_prompt, x_sample, cache_k, cache_v, c_prompt, c_sample, w_ada, b_ada, w_in, attn_sinks,
           sgu_ln_gain, sgu_ln_bias, sgu_w_s, sgu_b_s, w_pa, w_pb, w_o, ln_gain, ln_bias):
    xp = x_prompt.reshape(ROWS_P, D_MODEL)
    xs = x_sample.reshape(ROWS_S, D_MODEL)
    c_all = jnp.concatenate([jnp.repeat(c_sample, DEC_SEQ, axis=0), c_prompt,
                             jnp.zeros((MOD_ROWS - ROWS_S - BATCH, D_MODEL), f32)], axis=0)
    mod = _ada(c_all, w_ada, b_ada)
    ck = cache_k.reshape(DEPTH * DEC_BATCH, WINDOW, D_KV)
    cv = cache_v.reshape(DEPTH * DEC_BATCH, WINDOW, D_KV)

    win_k, win_v, new_k, new_v, sgu_v, ys = [], [], [], [], [], []
    for l in range(DEPTH):
        hp, hs, kv_p, kv_s = _stream(xp, xs, ys, mod, ln_gain, ln_bias, w_in)
        proj_p, proj_s = _matmul(hp, hs, w_in, l, PROJ_COLS // TN, _skip_kv_tile, bf16, "proj")
        gain = sgu_ln_gain[l].reshape(1, D_SGU)
        bias = sgu_ln_bias[l].reshape(1, D_SGU)
        w_small = jnp.tile(sgu_w_s[l][:, :DEC_SEQ, :DEC_SEQ], (1, SEQ_GROUP, SEQ_GROUP))
        b_small = jnp.tile(sgu_b_s[l][:, :DEC_SEQ].T, (SEQ_GROUP, 1))
        ab_s, vn_s = _mix_sample(proj_s, kv_s, ck, cv, l, attn_sinks[l], gain, bias, w_small, b_small)
        m_p, m_s = _mix_merge(proj_p, proj_s, kv_p, ab_s, attn_sinks[l], gain, bias, sgu_w_s[l], sgu_b_s[l].T,
                              w_pa, w_pb, l)
        ys.append(_matmul(m_p, m_s, w_o, l, D_MODEL // TN, lambda j: j, bf16, "out"))

        kv_win = kv_p.reshape(BATCH, SEQ, 2 * D_KV)[:, SEQ - WINDOW:]
        kv_win = kv_win.reshape(BATCH, WINDOW, 2, N_KV_HEADS, HEAD_DIM)
        kv_new = kv_s.reshape(DEC_BATCH, DEC_SEQ, 2, N_KV_HEADS, HEAD_DIM)
        win_k.append(kv_win[:, :, 0])
        win_v.append(kv_win[:, :, 1])
        new_k.append(kv_new[:, :, 0])
        new_v.append(kv_new[:, :, 1])
        sgu_v.append(vn_s.reshape(DEC_BATCH, DEC_SEQ, D_SGU))

    xp, xs = _final_stream(xp, xs, ys, mod, ln_gain, ln_bias)
    return (xp.reshape(BATCH, SEQ, D_MODEL), xs.reshape(DEC_BATCH, DEC_SEQ, D_MODEL),
            jnp.stack(win_k), jnp.stack(win_v), jnp.stack(new_k), jnp.stack(new_v), jnp.stack(sgu_v))
```

```python
import functools

import jax
import jax.numpy as jnp
from jax import lax
from jax.experimental import pallas as pl
from jax.experimental.pallas import tpu as pltpu

D_MODEL = 4096
BATCH = 4
SEQ = 2048
DEPTH = 2
DEC_BATCH = 32
DEC_SEQ = 4
HEAD_DIM = 64
D_ATTN = D_MODEL // 2
N_Q_HEADS = D_ATTN // HEAD_DIM
N_KV_HEADS = N_Q_HEADS // 8
GQA_GROUP = N_Q_HEADS // N_KV_HEADS
D_KV = N_KV_HEADS * HEAD_DIM
WINDOW = 128
D_SGU = D_MODEL // 2
N_SGU_GROUPS = 8
SGU_GROUP_DIM = D_SGU // N_SGU_GROUPS
CHUNK = 128
ALPHA = (2 * DEPTH) ** 0.25
LN_EPS = 1e-5
IN_COLS = D_ATTN + 2 * D_KV + D_ATTN + 3 * D_SGU + 2 * D_MODEL

ROWS_P = BATCH * SEQ
ROWS_S = DEC_BATCH * DEC_SEQ
MOD_ROWS = ROWS_S + 8
PROJ_COLS = IN_COLS - 2 * D_KV

TN = 512
TM = 1024
MM_TM = 2048
ADA_TN = 1024
KV_TILE = D_ATTN // TN
LN_TM = 256
SEQ_GROUP = 8
NEG = -1e30
LOG2E = 1.4426950408889634

VMEM_LIMIT = 56 * 1024 * 1024
STREAM_VMEM_LIMIT = 62 * 1024 * 1024

bf16 = jnp.bfloat16
f32 = jnp.float32


def _params(n_axes, vmem=VMEM_LIMIT):
    return pltpu.CompilerParams(dimension_semantics=("arbitrary",) * n_axes, vmem_limit_bytes=vmem)


def _sigmoid(x):
    return 0.5 + 0.5 * jnp.tanh(0.5 * x)


def _silu(x):
    half = 0.5 * x
    return half + half * jnp.tanh(half)


def _div_pow2(x, n):
    assert n & (n - 1) == 0
    return x >> (n.bit_length() - 1)


def _mod_pow2(x, n):
    assert n & (n - 1) == 0
    return x & (n - 1)


def _ada_kernel(c_ref, w_ref, b_ref, o_ref):
    a = _silu(c_ref[...]).astype(bf16)
    o_ref[...] = jnp.dot(a, w_ref[...].astype(bf16), preferred_element_type=f32) + b_ref[...]


def _ada(c_all, w_ada, b_ada):
    tiles_per_part = D_MODEL // ADA_TN
    return pl.pallas_call(
        _ada_kernel,
        grid=(DEPTH, 3 * tiles_per_part),
        in_specs=[
            pl.BlockSpec((MOD_ROWS, D_MODEL), lambda l, j: (0, 0)),
            pl.BlockSpec((None, D_MODEL, ADA_TN), lambda l, j: (l, 0, j)),
            pl.BlockSpec((None, 1, ADA_TN), lambda l, j: (l, 0, j)),
        ],
        out_specs=pl.BlockSpec((None, None, MOD_ROWS, ADA_TN),
                               lambda l, j: (l, j // tiles_per_part, 0, j % tiles_per_part)),
        out_shape=jax.ShapeDtypeStruct((DEPTH, 3, MOD_ROWS, D_MODEL), f32),
        compiler_params=_params(2),
        name="ada",
    )(c_all, w_ada, b_ada.reshape(DEPTH, 1, 3 * D_MODEL))


def _prompt_mod_row(ref, part, blk, blocks_per_batch):
    return ref[part, pl.ds(ROWS_S + blk // blocks_per_batch, 1), :]


def _matmul_kernel(xp_ref, xs_ref, w_ref, op_ref, os_ref, w_bf):
    @pl.when(pl.program_id(1) == 0)
    def _():
        w_bf[...] = w_ref[...].astype(bf16)
        os_ref[...] = jnp.dot(xs_ref[...], w_bf[...], preferred_element_type=f32).astype(os_ref.dtype)

    op_ref[...] = jnp.dot(xp_ref[...], w_bf[...], preferred_element_type=f32).astype(op_ref.dtype)


def _matmul(xp, xs, w, layer, n_tiles, w_tile_of, out_dtype, name):
    k = xp.shape[1]
    out_bytes = jnp.dtype(out_dtype).itemsize
    vmem = (2 * (MM_TM + ROWS_S) * k * 2 + 2 * k * TN * 4 + k * TN * 2 + 2 * (MM_TM + ROWS_S) * TN * out_bytes
            + (4 << 20))
    return pl.pallas_call(
        _matmul_kernel,
        grid=(n_tiles, ROWS_P // MM_TM),
        in_specs=[pl.BlockSpec((MM_TM, k), lambda j, i: (i, 0)),
                  pl.BlockSpec((ROWS_S, k), lambda j, i: (0, 0)),
                  pl.BlockSpec((None, k, TN), lambda j, i: (layer, 0, w_tile_of(j)))],
        out_specs=[pl.BlockSpec((MM_TM, TN), lambda j, i: (i, j)),
                   pl.BlockSpec((ROWS_S, TN), lambda j, i: (0, j))],
        out_shape=[jax.ShapeDtypeStruct((ROWS_P, n_tiles * TN), out_dtype),
                   jax.ShapeDtypeStruct((ROWS_S, n_tiles * TN), out_dtype)],
        scratch_shapes=[pltpu.VMEM((k, TN), bf16)],
        compiler_params=_params(2, vmem),
        name=name,
    )(xp, xs, w)


def _skip_kv_tile(c):
    return c + (c >= KV_TILE).astype(jnp.int32)


GA_COL = 5 * D_ATTN
GB_COL = GA_COL + D_MODEL


def _attention_stages(q_ref, k_all, v_all, mask, slot, slot_cols, live_key, sink_ref, za_ref, ab_ref, t, lk):
    lane = lax.broadcasted_iota(jnp.int32, (lk, 2 * HEAD_DIM), 1)
    low, high = lane < HEAD_DIM, lane >= HEAD_DIM
    ones2 = jnp.concatenate([low, high], axis=0).astype(f32).astype(bf16)
    nt = (((1,), (1,)), ((), ()))
    pairs = GQA_GROUP // 2
    k_scale = HEAD_DIM ** -0.5 * LOG2E

    def head(kv):
        cols = slice((kv // 2) * 2 * HEAD_DIM, (kv // 2 + 1) * 2 * HEAD_DIM)
        kp, vp = k_all[:, cols] * k_scale, v_all[:, cols]
        keep = (low if kv % 2 == 0 else high) & live_key
        k_own = jnp.where(keep, kp, 0.0)
        v_own = jnp.where(keep, vp, 0.0)
        k_oth = pltpu.roll(k_own, HEAD_DIM, 1)
        v_oth = pltpu.roll(v_own, HEAD_DIM, 1)
        if kv % 2 == 0:
            k2 = jnp.concatenate([k_own, k_oth], axis=0).astype(bf16)
            v2 = jnp.concatenate([v_own, v_oth], axis=0).astype(bf16)
        else:
            k2 = jnp.concatenate([k_oth, k_own], axis=0).astype(bf16)
            v2 = jnp.concatenate([v_oth, v_own], axis=0).astype(bf16)
        v2 = jnp.concatenate([v2, ones2], axis=1)
        base = kv * GQA_GROUP * HEAD_DIM
        q4 = jnp.concatenate(
            [q_ref[:, base + p * 128: base + (p + 1) * 128] for p in range(pairs)], axis=0)
        s_all = lax.dot_general(q4, k2, nt, preferred_element_type=f32)
        rows = []
        for p in range(pairs):
            halves = []
            for h in range(2):
                sink = sink_ref[kv * GQA_GROUP + 2 * p + h] * LOG2E
                fill = jnp.where(slot, sink, NEG)
                if slot_cols < lk:
                    fill = jnp.concatenate([fill, jnp.full((t, lk - slot_cols), NEG, f32)], axis=1)
                s = jnp.where(mask, s_all[p * t:(p + 1) * t, h * lk:(h + 1) * lk], fill)
                m = jnp.max(s, axis=1, keepdims=True)
                halves.append(jnp.exp2(s - m).astype(bf16))
            rows.append(jnp.concatenate(halves, axis=1))
        probs = jnp.concatenate(rows, axis=0)
        o = jnp.dot(probs, v2, preferred_element_type=f32)
        o = o[:, 0:128] / o[:, 128:256]
        for p in range(pairs):
            cols = slice(base + p * 128, base + (p + 1) * 128)
            ab_ref[:, cols] = (o[p * t:(p + 1) * t, :] * _silu(za_ref[:, cols].astype(f32))).astype(bf16)

    return [functools.partial(head, kv) for kv in range(N_KV_HEADS)]


def _sgu_norm(vb_ref, gain_ref, bias_ref):
    vb = vb_ref[...].astype(f32)
    mu = jnp.mean(vb, axis=1, keepdims=True)
    d = vb - mu
    var = jnp.mean(d * d, axis=1, keepdims=True)
    return d * lax.rsqrt(var + LN_EPS) * gain_ref[...] + bias_ref[...]


def _sgu_stages(u_ref, zb_ref, ws_ref, bs_ref, vn, w_mask, ab_ref):
    vn_bf = vn.astype(bf16)

    def group(g):
        cols = slice(g * SGU_GROUP_DIM, (g + 1) * SGU_GROUP_DIM)
        w_g = jnp.where(w_mask, ws_ref[g], 0.0).astype(bf16)
        s = jnp.dot(w_g, vn_bf[:, cols], preferred_element_type=f32) + bs_ref[:, g:g + 1]
        out_b = u_ref[:, cols].astype(f32) * s * _silu(zb_ref[:, cols].astype(f32))
        ab_ref[:, D_ATTN + g * SGU_GROUP_DIM: D_ATTN + (g + 1) * SGU_GROUP_DIM] = out_b.astype(bf16)

    return [functools.partial(group, g) for g in range(N_SGU_GROUPS)]


def _interleave(attention, sgu, others=()):
    per_head = len(sgu) // len(attention)
    others = list(others)
    if others:
        others.pop(0)()
    for kv, head in enumerate(attention):
        head()
        if others:
            others.pop(0)()
        for stage in sgu[kv * per_head:(kv + 1) * per_head]:
            stage()


def _mix_prompt_block(n, sink_ref, q_ref, za_ref, u_ref, vb_ref, zb_ref, kvc_ref, kvp_ref,
                      gain_ref, bias_ref, ws_ref, bs_ref, ab_ref, others=()):
    vn = _sgu_norm(vb_ref, gain_ref, bias_ref)
    kvc, kvp = kvc_ref[...], kvp_ref[...]
    k_all = jnp.concatenate([kvp[:, 0:D_KV], kvc[:, 0:D_KV]], axis=0)
    v_all = jnp.concatenate([kvp[:, D_KV:], kvc[:, D_KV:]], axis=0)
    qi = lax.broadcasted_iota(jnp.int32, (WINDOW, 2 * WINDOW), 0)
    kj = lax.broadcasted_iota(jnp.int32, (WINDOW, 2 * WINDOW), 1)
    no_prev = jnp.where(n > 0, 0, 2 * WINDOW)
    mask = ((kj < WINDOW) & (kj > qi + no_prev)) | ((kj >= WINDOW) & (kj - WINDOW <= qi))
    slot = lax.broadcasted_iota(jnp.int32, (WINDOW, WINDOW), 1) == 0
    live_key = lax.broadcasted_iota(jnp.int32, (2 * WINDOW, 2 * HEAD_DIM), 0) != 0
    heads = _attention_stages(q_ref, k_all, v_all, mask, slot, WINDOW, live_key, sink_ref, za_ref, ab_ref,
                              WINDOW, 2 * WINDOW)
    r = lax.broadcasted_iota(jnp.int32, (CHUNK, CHUNK), 0)
    c = lax.broadcasted_iota(jnp.int32, (CHUNK, CHUNK), 1)
    _interleave(heads, _sgu_stages(u_ref, zb_ref, ws_ref, bs_ref, vn, c <= r, ab_ref), others)


ROW_TILES = ROWS_P // TM
TILE_BLOCKS = TM // WINDOW
assert TILE_BLOCKS == D_MODEL // TN


def _mix_merge_kernel(sink_ref, q_ref, za_ref, u_ref, vb_ref, zb_ref, kvc_ref, kvp_ref,
                      gain_ref, bias_ref, ws_ref, bs_ref,
                      abs_ref, gap_ref, gbp_ref, gas_ref, gbs_ref, wa_ref, wb_ref,
                      op_ref, os_ref, ab_even, ab_odd, wa_bf, wb_bf):
    r, c = pl.program_id(0), pl.program_id(1)
    blk = jnp.minimum(r, ROW_TILES - 1) * TILE_BLOCKS + c

    def cast_weights():
        wa_bf[...] = wa_ref[...].astype(bf16)
        wb_bf[...] = wb_ref[...].astype(bf16)

    def mix(ab_tile, others=()):
        mix_out = ab_tile.at[pl.ds(pl.multiple_of(c * WINDOW, WINDOW), WINDOW), :]
        _mix_prompt_block(_mod_pow2(blk, SEQ // WINDOW), sink_ref, q_ref, za_ref, u_ref, vb_ref, zb_ref,
                          kvc_ref, kvp_ref, gain_ref, bias_ref, ws_ref, bs_ref, mix_out, others)

    def merged(ab_ref, ga_ref, gb_ref):
        ya = jnp.dot(ab_ref[:, 0:D_ATTN], wa_bf[...], preferred_element_type=f32)
        yb = jnp.dot(ab_ref[:, D_ATTN:], wb_bf[...], preferred_element_type=f32)
        return (_sigmoid(ga_ref[...].astype(f32)) * ya + _sigmoid(gb_ref[...].astype(f32)) * yb).astype(bf16)

    def mix_and_project(mix_tile, done_tile):
        cast_weights()
        half = TN // 2
        parts = {}

        def piece(lhs_cols, w_bf, key, cols):
            parts[key] = jnp.dot(done_tile[:, lhs_cols], w_bf[:, cols], preferred_element_type=f32)

        pieces = [functools.partial(piece, lhs_cols, w_bf, (name, hc), slice(hc * half, (hc + 1) * half))
                  for name, lhs_cols, w_bf in (("a", slice(0, D_ATTN), wa_bf), ("b", slice(D_ATTN, D_MODEL), wb_bf))
                  for hc in range(2)]
        mix(mix_tile, pieces)
        for hc in range(2):
            cols = slice(hc * half, (hc + 1) * half)
            op_ref[:, cols] = (_sigmoid(gap_ref[:, cols].astype(f32)) * parts["a", hc]
                               + _sigmoid(gbp_ref[:, cols].astype(f32)) * parts["b", hc]).astype(bf16)

    @pl.when(r == 0)
    def _():
        cast_weights()

        def project_sample():
            os_ref[...] = merged(abs_ref, gas_ref, gbs_ref)

        mix(ab_even, [project_sample])

    @pl.when((r < ROW_TILES) & (r % 2 == 1))
    def _():
        mix_and_project(ab_odd, ab_even)

    @pl.when((r > 0) & (r < ROW_TILES) & (r % 2 == 0))
    def _():
        mix_and_project(ab_even, ab_odd)

    @pl.when(r == ROW_TILES)
    def _():
        cast_weights()
        op_ref[...] = merged(ab_odd if (ROW_TILES - 1) % 2 else ab_even, gap_ref, gbp_ref)


def _mix_merge(proj_p, proj_s, kv_p, ab_s, sinks, gain, bias, w_s, b_s, w_pa, w_pb, layer):
    seq_blocks = SEQ // WINDOW
    ga_tile, gb_tile = GA_COL // TN, GB_COL // TN
    blk_of = lambda r, c: jnp.minimum(r, ROW_TILES - 1) * TILE_BLOCKS + c
    prev_of = lambda r, c: blk_of(r, c) - (_mod_pow2(blk_of(r, c), seq_blocks) > 0).astype(jnp.int32)
    tile_of = lambda r: jnp.maximum(r - 1, 0)
    col_of = lambda r, c: jnp.where(r == 0, 0, c)
    scol_of = lambda r, c: jnp.where(r == 0, c, TILE_BLOCKS - 1)
    full = lambda shape: pl.BlockSpec(shape, lambda r, c: (0,) * len(shape))
    return pl.pallas_call(
        _mix_merge_kernel,
        grid=(ROW_TILES + 1, TILE_BLOCKS),
        in_specs=[pl.BlockSpec(memory_space=pltpu.SMEM)] + _proj_specs(WINDOW, blk_of) + [
            pl.BlockSpec((WINDOW, 2 * D_KV), lambda r, c: (blk_of(r, c), 0)),
            pl.BlockSpec((WINDOW, 2 * D_KV), lambda r, c: (prev_of(r, c), 0)),
            full((1, D_SGU)), full((1, D_SGU)),
            full((N_SGU_GROUPS, CHUNK, CHUNK)), full((CHUNK, N_SGU_GROUPS)),
            full((ROWS_S, D_MODEL)),
            pl.BlockSpec((TM, TN), lambda r, c: (tile_of(r), ga_tile + col_of(r, c))),
            pl.BlockSpec((TM, TN), lambda r, c: (tile_of(r), gb_tile + col_of(r, c))),
            pl.BlockSpec((ROWS_S, TN), lambda r, c: (0, ga_tile + scol_of(r, c))),
            pl.BlockSpec((ROWS_S, TN), lambda r, c: (0, gb_tile + scol_of(r, c))),
            pl.BlockSpec((None, D_ATTN, TN), lambda r, c: (layer, 0, c)),
            pl.BlockSpec((None, D_SGU, TN), lambda r, c: (layer, 0, c)),
        ],
        out_specs=[pl.BlockSpec((TM, TN), lambda r, c: (tile_of(r), col_of(r, c))),
                   pl.BlockSpec((ROWS_S, TN), lambda r, c: (0, scol_of(r, c)))],
        out_shape=[jax.ShapeDtypeStruct((ROWS_P, D_MODEL), bf16), jax.ShapeDtypeStruct((ROWS_S, D_MODEL), bf16)],
        scratch_shapes=[pltpu.VMEM((TM, D_MODEL), bf16), pltpu.VMEM((TM, D_MODEL), bf16),
                        pltpu.VMEM((D_ATTN, TN), bf16), pltpu.VMEM((D_SGU, TN), bf16)],
        compiler_params=_params(2),
        name="mix_merge",
    )(sinks, proj_p, proj_p, proj_p, proj_p, proj_p, kv_p, kv_p, gain, bias, w_s, b_s,
      ab_s, proj_p, proj_p, proj_s, proj_s, w_pa, w_pb)


def _mix_sample_kernel(sink_ref, q_ref, za_ref, u_ref, vb_ref, zb_ref, kvn_ref, ck_ref, cv_ref,
                       gain_ref, bias_ref, ws_ref, bs_ref, ab_ref, vn_ref):
    t = SEQ_GROUP * DEC_SEQ
    n_cache = SEQ_GROUP * WINDOW
    lk = n_cache + WINDOW
    kvn = kvn_ref[...]
    pad = jnp.zeros((WINDOW - t, D_KV), f32)
    k_all = jnp.concatenate([ck_ref[...].reshape(n_cache, D_KV), kvn[:, 0:D_KV], pad], axis=0)
    v_all = jnp.concatenate([cv_ref[...].reshape(n_cache, D_KV), kvn[:, D_KV:], pad], axis=0)
    qi = lax.broadcasted_iota(jnp.int32, (t, lk), 0)
    kj = lax.broadcasted_iota(jnp.int32, (t, lk), 1)
    q_seq, q_tok = _div_pow2(qi, DEC_SEQ), _mod_pow2(qi, DEC_SEQ)
    in_cache = (_div_pow2(kj, WINDOW) == q_seq) & (_mod_pow2(kj, WINDOW) > q_tok)
    kn = jnp.maximum(kj - n_cache, 0)
    in_new = (kn < t) & (_div_pow2(kn, DEC_SEQ) == q_seq) & (_mod_pow2(kn, DEC_SEQ) <= q_tok)
    mask = ((kj < n_cache) & in_cache) | ((kj >= n_cache) & in_new)
    slot_seq = _div_pow2(lax.broadcasted_iota(jnp.int32, (t, n_cache), 0), DEC_SEQ)
    slot = lax.broadcasted_iota(jnp.int32, (t, n_cache), 1) == slot_seq * WINDOW
    key_row = lax.broadcasted_iota(jnp.int32, (lk, 2 * HEAD_DIM), 0)
    live_key = (key_row >= n_cache) | (_mod_pow2(key_row, WINDOW) != 0)
    heads = _attention_stages(q_ref, k_all, v_all, mask, slot, n_cache, live_key, sink_ref, za_ref, ab_ref, t, lk)

    vn = _sgu_norm(vb_ref, gain_ref, bias_ref)
    vn_ref[...] = vn
    r = lax.broadcasted_iota(jnp.int32, (t, t), 0)
    c = lax.broadcasted_iota(jnp.int32, (t, t), 1)
    w_mask = (_div_pow2(r, DEC_SEQ) == _div_pow2(c, DEC_SEQ)) & (c <= r)
    _interleave(heads, _sgu_stages(u_ref, zb_ref, ws_ref, bs_ref, vn, w_mask, ab_ref))


def _proj_specs(rows, row_of):
    return [pl.BlockSpec((rows, D_ATTN), functools.partial(lambda s, *g: (row_of(*g), s), s))
            for s in range(5)]


def _mix_sample(proj, kv, cache_k, cache_v, layer, sinks, gain, bias, w_s, b_s):
    t = SEQ_GROUP * DEC_SEQ
    groups = DEC_BATCH // SEQ_GROUP
    full = lambda shape: pl.BlockSpec(shape, lambda g: (0,) * len(shape))
    cache_spec = pl.BlockSpec((SEQ_GROUP, WINDOW, D_KV), lambda g: (layer * groups + g, 0, 0))
    return pl.pallas_call(
        _mix_sample_kernel,
        grid=(groups,),
        in_specs=[pl.BlockSpec(memory_space=pltpu.SMEM)] + _proj_specs(t, lambda g: g) + [
            pl.BlockSpec((t, 2 * D_KV), lambda g: (g, 0)),
            cache_spec, cache_spec,
            full((1, D_SGU)), full((1, D_SGU)),
            full((N_SGU_GROUPS, t, t)), full((t, N_SGU_GROUPS)),
        ],
        out_specs=[pl.BlockSpec((t, D_MODEL), lambda g: (g, 0)),
                   pl.BlockSpec((t, D_SGU), lambda g: (g, 0))],
        out_shape=[jax.ShapeDtypeStruct((ROWS_S, D_MODEL), bf16),
                   jax.ShapeDtypeStruct((ROWS_S, D_SGU), f32)],
        compiler_params=_params(1),
        name="mix_sample",
    )(sinks, proj, proj, proj, proj, proj, kv, cache_k, cache_v, gain, bias, w_s, b_s)


def _post_norm(x, y, gate, g_ref, b_ref):
    t = x + (gate * (1.0 / ALPHA)) * y.astype(f32)
    mu = jnp.mean(t, axis=1, keepdims=True)
    d = t - mu
    var = jnp.mean(d * d, axis=1, keepdims=True)
    return d * lax.rsqrt(var + LN_EPS / ALPHA ** 2) * g_ref[...] + b_ref[...]


def _ln_kernel(*refs, n, tm):
    xp_ref, xs_ref = refs[0:2]
    y_refs = refs[2:2 + 2 * n]
    gate_refs = refs[2 + 2 * n:2 + 3 * n]
    gb_refs = refs[2 + 3 * n:2 + 5 * n]
    rest = refs[2 + 5 * n:]
    i = pl.program_id(0)
    blocks = ROWS_P // tm

    def stream(x, sample):
        for l in range(n):
            if sample:
                gate = gate_refs[l][0, 0:ROWS_S, :]
            else:
                gate = _prompt_mod_row(gate_refs[l], 0, i, SEQ // tm)
            x = _post_norm(x, y_refs[2 * l + sample][...], gate, gb_refs[2 * l], gb_refs[2 * l + 1])
        return x

    if n == DEPTH:
        xp_out, xs_out = rest

        @pl.when(i < blocks)
        def _():
            xp_out[...] = stream(xp_ref[...], 0)

        @pl.when(i == blocks)
        def _():
            xs_out[...] = stream(xs_ref[...], 1)
    else:
        nxt_ref, w_ref, hp_out, hs_out, kvp_out, kvs_out, w_bf = rest

        @pl.when(i == 0)
        def _():
            w_bf[...] = w_ref[...].astype(bf16)

        @pl.when(i < blocks)
        def _():
            shift = _prompt_mod_row(nxt_ref, 0, i, SEQ // tm)
            scale = _prompt_mod_row(nxt_ref, 1, i, SEQ // tm)
            h = (stream(xp_ref[...], 0) * (1.0 + scale) + shift).astype(bf16)
            hp_out[...] = h
            kvp_out[...] = jnp.dot(h, w_bf[...], preferred_element_type=f32)

        @pl.when(i == blocks)
        def _():
            h = (stream(xs_ref[...], 1) * (1.0 + nxt_ref[1, 0:ROWS_S, :]) + nxt_ref[0, 0:ROWS_S, :]).astype(bf16)
            hs_out[...] = h
            kvs_out[...] = jnp.dot(h, w_bf[...], preferred_element_type=f32)


def _stream(xp, xs, ys, mod, ln_gain, ln_bias, w_in):
    n = len(ys)
    tm = 2 * LN_TM if n == 0 else LN_TM
    blocks = ROWS_P // tm
    row = lambda i: (jnp.minimum(i, blocks - 1), 0)
    once = dict(pipeline_mode=pl.Buffered(1))
    row_spec = pl.BlockSpec((tm, D_MODEL), row)
    sample_in = pl.BlockSpec((ROWS_S, D_MODEL), lambda i: (0, 0), **once)
    sample_out = pl.BlockSpec((ROWS_S, D_MODEL), lambda i: (0, 0))
    in_specs = [row_spec, sample_in] + [row_spec, sample_in] * n
    in_specs += [pl.BlockSpec((None, 1, MOD_ROWS, D_MODEL), functools.partial(lambda l, i: (l, 2, 0, 0), l), **once)
                 for l in range(n)]
    in_specs += [pl.BlockSpec((None, 1, D_MODEL), functools.partial(lambda l, i: (l, 0, 0), l))
                 for l in range(n) for _ in range(2)]
    args = [xp, xs] + [y for pair in ys for y in pair] + [mod] * n
    for l in range(n):
        args += [ln_gain.reshape(DEPTH, 1, D_MODEL), ln_bias.reshape(DEPTH, 1, D_MODEL)]
    if n == DEPTH:
        out_specs = [row_spec, sample_out]
        out_shape = [jax.ShapeDtypeStruct((ROWS_P, D_MODEL), f32), jax.ShapeDtypeStruct((ROWS_S, D_MODEL), f32)]
        scratch = []
    else:
        in_specs += [pl.BlockSpec((None, 2, MOD_ROWS, D_MODEL), lambda i: (n, 0, 0, 0), **once),
                     pl.BlockSpec((None, D_MODEL, TN), lambda i: (n, 0, KV_TILE), **once)]
        args += [mod, w_in]
        out_specs = [row_spec, sample_out, pl.BlockSpec((tm, 2 * D_KV), row),
                     pl.BlockSpec((ROWS_S, 2 * D_KV), lambda i: (0, 0))]
        out_shape = [jax.ShapeDtypeStruct((ROWS_P, D_MODEL), bf16), jax.ShapeDtypeStruct((ROWS_S, D_MODEL), bf16),
                     jax.ShapeDtypeStruct((ROWS_P, 2 * D_KV), f32), jax.ShapeDtypeStruct((ROWS_S, 2 * D_KV), f32)]
        scratch = [pltpu.VMEM((D_MODEL, TN), bf16)]
    return pl.pallas_call(
        functools.partial(_ln_kernel, n=n, tm=tm),
        grid=(blocks + 1,),
        in_specs=in_specs,
        out_specs=out_specs,
        out_shape=out_shape,
        scratch_shapes=scratch,
        compiler_params=_params(1, STREAM_VMEM_LIMIT),
        name=f"stream{n}",
    )(*args)


RING = 3


def _final_kernel(*refs):
    n = DEPTH
    xp_hbm, xs_ref = refs[0:2]
    y_refs = refs[2:2 + 2 * n]
    gate_refs = refs[2 + 2 * n:2 + 3 * n]
    gb_refs = refs[2 + 3 * n:2 + 5 * n]
    xp_out_hbm, xs_out = refs[2 + 5 * n:4 + 5 * n]
    x_buf, o_buf = refs[4 + 5 * n:6 + 5 * n]
    y_bufs = refs[6 + 5 * n:6 + 6 * n]
    in_sem, out_sem = refs[6 + 6 * n:]
    blocks = ROWS_P // LN_TM

    def rows(blk):
        return pl.ds(pl.multiple_of(blk * LN_TM, LN_TM), LN_TM)

    def in_copies(blk, slot):
        copies = [pltpu.make_async_copy(xp_hbm.at[rows(blk)], x_buf.at[slot], in_sem.at[0, slot])]
        for l in range(n):
            copies.append(pltpu.make_async_copy(y_refs[2 * l].at[rows(blk)], y_bufs[l].at[slot],
                                                in_sem.at[1 + l, slot]))
        return copies

    def out_copy(blk, slot):
        return pltpu.make_async_copy(o_buf.at[slot], xp_out_hbm.at[rows(blk)], out_sem.at[slot])

    for blk in range(RING - 1):
        for c in in_copies(blk, blk):
            c.start()

    def body(i, carry):
        slot, o_slot = i % RING, i % 2

        @pl.when(i + RING - 1 < blocks)
        def _():
            for c in in_copies(i + RING - 1, (i + RING - 1) % RING):
                c.start()

        for c in in_copies(i, slot):
            c.wait()

        @pl.when(i >= 2)
        def _():
            out_copy(i - 2, o_slot).wait()

        x = x_buf[slot]
        for l in range(n):
            gate = _prompt_mod_row(gate_refs[l], 0, i, SEQ // LN_TM)
            x = _post_norm(x, y_bufs[l][slot], gate, gb_refs[2 * l], gb_refs[2 * l + 1])
        o_buf[o_slot] = x
        out_copy(i, o_slot).start()
        return carry

    lax.fori_loop(0, blocks, body, 0)
    for blk in range(blocks - 2, blocks):
        out_copy(blk, blk % 2).wait()

    x = xs_ref[...]
    for l in range(n):
        x = _post_norm(x, y_refs[2 * l + 1][...], gate_refs[l][0, 0:ROWS_S, :], gb_refs[2 * l], gb_refs[2 * l + 1])
    xs_out[...] = x


def _ring_kernel(*refs, n):
    xp_hbm, xs_ref = refs[0:2]
    y_refs = refs[2:2 + 2 * n]
    gate_refs = refs[2 + 2 * n:2 + 3 * n]
    gb_refs = refs[2 + 3 * n:2 + 5 * n]
    nxt_ref, w_ref, hp_hbm, hs_out, kvp_hbm, kvs_out, x_buf, h_buf, kv_buf = refs[2 + 5 * n:11 + 5 * n]
    y_bufs = refs[11 + 5 * n:11 + 6 * n]
    w_bf, in_sem, out_sem = refs[11 + 6 * n:]
    blocks = ROWS_P // LN_TM
    w_bf[...] = w_ref[...].astype(bf16)

    def rows(blk):
        return pl.ds(pl.multiple_of(blk * LN_TM, LN_TM), LN_TM)

    def in_copies(blk, slot):
        copies = [pltpu.make_async_copy(xp_hbm.at[rows(blk)], x_buf.at[slot], in_sem.at[0, slot])]
        for l in range(n):
            copies.append(pltpu.make_async_copy(y_refs[2 * l].at[rows(blk)], y_bufs[l].at[slot],
                                                in_sem.at[1 + l, slot]))
        return copies

    def out_copies(blk, slot):
        return [pltpu.make_async_copy(h_buf.at[slot], hp_hbm.at[rows(blk)], out_sem.at[0, slot]),
                pltpu.make_async_copy(kv_buf.at[slot], kvp_hbm.at[rows(blk)], out_sem.at[1, slot])]

    def stream(x, ys, gates):
        for l in range(n):
            x = _post_norm(x, ys[l], gates[l], gb_refs[2 * l], gb_refs[2 * l + 1])
        return x

    for blk in range(RING - 1):
        for c in in_copies(blk, blk):
            c.start()

    def body(i, carry):
        slot, o_slot = i % RING, i % 2

        @pl.when(i + RING - 1 < blocks)
        def _():
            for c in in_copies(i + RING - 1, (i + RING - 1) % RING):
                c.start()

        for c in in_copies(i, slot):
            c.wait()

        @pl.when(i >= 2)
        def _():
            for c in out_copies(i - 2, o_slot):
                c.wait()

        x = stream(x_buf[slot], [y_bufs[l][slot] for l in range(n)],
                   [_prompt_mod_row(gate_refs[l], 0, i, SEQ // LN_TM) for l in range(n)])
        shift = _prompt_mod_row(nxt_ref, 0, i, SEQ // LN_TM)
        scale = _prompt_mod_row(nxt_ref, 1, i, SEQ // LN_TM)
        h = (x * (1.0 + scale) + shift).astype(bf16)
        h_buf[o_slot] = h
        kv_buf[o_slot] = jnp.dot(h, w_bf[...], preferred_element_type=f32)
        for c in out_copies(i, o_slot):
            c.start()
        return carry

    lax.fori_loop(0, blocks, body, 0)
    for blk in range(blocks - 2, blocks):
        for c in out_copies(blk, blk % 2):
            c.wait()

    x = stream(xs_ref[...], [y_refs[2 * l + 1][...] for l in range(n)],
               [gate_refs[l][0, 0:ROWS_S, :] for l in range(n)])
    h = (x * (1.0 + nxt_ref[1, 0:ROWS_S, :]) + nxt_ref[0, 0:ROWS_S, :]).astype(bf16)
    hs_out[...] = h
    kvs_out[...] = jnp.dot(h, w_bf[...], preferred_element_type=f32)


def _ring_stream(xp, xs, ys, mod, ln_gain, ln_bias, w_in):
    n = len(ys)
    hbm = pl.BlockSpec(memory_space=pl.ANY)
    once = dict(pipeline_mode=pl.Buffered(1))
    whole = lambda shape: pl.BlockSpec(shape, lambda i: (0,) * len(shape), **once)
    in_specs = [hbm, whole((ROWS_S, D_MODEL))] + [hbm, whole((ROWS_S, D_MODEL))] * n
    in_specs += [pl.BlockSpec((None, 1, MOD_ROWS, D_MODEL), functools.partial(lambda l, i: (l, 2, 0, 0), l), **once)
                 for l in range(n)]
    in_specs += [pl.BlockSpec((None, 1, D_MODEL), functools.partial(lambda l, i: (l, 0, 0), l))
                 for l in range(n) for _ in range(2)]
    in_specs += [pl.BlockSpec((None, 2, MOD_ROWS, D_MODEL), lambda i: (n, 0, 0, 0), **once),
                 pl.BlockSpec((None, D_MODEL, TN), lambda i: (n, 0, KV_TILE), **once)]
    args = [xp, xs] + [y for pair in ys for y in pair] + [mod] * n
    for l in range(n):
        args += [ln_gain.reshape(DEPTH, 1, D_MODEL), ln_bias.reshape(DEPTH, 1, D_MODEL)]
    args += [mod, w_in]
    plain = lambda shape: pl.BlockSpec(shape, lambda i: (0,) * len(shape))
    return pl.pallas_call(
        functools.partial(_ring_kernel, n=n),
        grid=(1,),
        in_specs=in_specs,
        out_specs=[hbm, plain((ROWS_S, D_MODEL)), hbm, plain((ROWS_S, 2 * D_KV))],
        out_shape=[jax.ShapeDtypeStruct((ROWS_P, D_MODEL), bf16), jax.ShapeDtypeStruct((ROWS_S, D_MODEL), bf16),
                   jax.ShapeDtypeStruct((ROWS_P, 2 * D_KV), f32), jax.ShapeDtypeStruct((ROWS_S, 2 * D_KV), f32)],
        scratch_shapes=[pltpu.VMEM((RING, LN_TM, D_MODEL), f32), pltpu.VMEM((2, LN_TM, D_MODEL), bf16),
                        pltpu.VMEM((2, LN_TM, 2 * D_KV), f32)]
        + [pltpu.VMEM((RING, LN_TM, D_MODEL), bf16) for _ in range(n)]
        + [pltpu.VMEM((D_MODEL, TN), bf16), pltpu.SemaphoreType.DMA((1 + n, RING)), pltpu.SemaphoreType.DMA((2, 2))],
        compiler_params=_params(1),
        name=f"stream{n}",
    )(*args)


def _final_stream(xp, xs, ys, mod, ln_gain, ln_bias):
    n = DEPTH
    hbm = pl.BlockSpec(memory_space=pl.ANY)
    whole = lambda shape: pl.BlockSpec(shape, lambda i: (0,) * len(shape))
    in_specs = [hbm, whole((ROWS_S, D_MODEL))] + [hbm, whole((ROWS_S, D_MODEL))] * n
    in_specs += [pl.BlockSpec((None, 1, MOD_ROWS, D_MODEL), functools.partial(lambda l, i: (l, 2, 0, 0), l))
                 for l in range(n)]
    in_specs += [pl.BlockSpec((None, 1, D_MODEL), functools.partial(lambda l, i: (l, 0, 0), l))
                 for l in range(n) for _ in range(2)]
    args = [xp, xs] + [y for pair in ys for y in pair] + [mod] * n
    for l in range(n):
        args += [ln_gain.reshape(DEPTH, 1, D_MODEL), ln_bias.reshape(DEPTH, 1, D_MODEL)]
    return pl.pallas_call(
        _final_kernel,
        grid=(1,),
        in_specs=in_specs,
        out_specs=[hbm, whole((ROWS_S, D_MODEL))],
        out_shape=[jax.ShapeDtypeStruct((ROWS_P, D_MODEL), f32), jax.ShapeDtypeStruct((ROWS_S, D_MODEL), f32)],
        scratch_shapes=[pltpu.VMEM((RING, LN_TM, D_MODEL), f32), pltpu.VMEM((2, LN_TM, D_MODEL), f32)]
        + [pltpu.VMEM((RING, LN_TM, D_MODEL), bf16) for _ in range(n)]
        + [pltpu.SemaphoreType.DMA((1 + n, RING)), pltpu.SemaphoreType.DMA((2,))],
        compiler_params=_params(1),
        name="stream_final",
    )(*args)


def kernel(x_prompt, x_sample, cache_k, cache_v, c_prompt, c_sample, w_ada, b_ada, w_in, attn_sinks,
           sgu_ln_gain, sgu_ln_bias, sgu_w_s, sgu_b_s, w_pa, w_pb, w_o, ln_gain, ln_bias):
    xp = x_prompt.reshape(ROWS_P, D_MODEL)
    xs = x_sample.reshape(ROWS_S, D_MODEL)
    c_all = jnp.concatenate([jnp.repeat(c_sample, DEC_SEQ, axis=0), c_prompt,
                             jnp.zeros((MOD_ROWS - ROWS_S - BATCH, D_MODEL), f32)], axis=0)
    mod = _ada(c_all, w_ada, b_ada)
    ck = cache_k.reshape(DEPTH * DEC_BATCH, WINDOW, D_KV)
    cv = cache_v.reshape(DEPTH * DEC_BATCH, WINDOW, D_KV)

    win_k, win_v, new_k, new_v, sgu_v, ys = [], [], [], [], [], []
    for l in range(DEPTH):
        hp, hs, kv_p, kv_s = _ring_stream(xp, xs, ys, mod, ln_gain, ln_bias, w_in)
        proj_p, proj_s = _matmul(hp, hs, w_in, l, PROJ_COLS // TN, _skip_kv_tile, bf16, "proj")
        gain = sgu_ln_gain[l].reshape(1, D_SGU)
        bias = sgu_ln_bias[l].reshape(1, D_SGU)
        w_small = jnp.tile(sgu_w_s[l][:, :DEC_SEQ, :DEC_SEQ], (1, SEQ_GROUP, SEQ_GROUP))
        b_small = jnp.tile(sgu_b_s[l][:, :DEC_SEQ].T, (SEQ_GROUP, 1))
        ab_s, vn_s = _mix_sample(proj_s, kv_s, ck, cv, l, attn_sinks[l], gain, bias, w_small, b_small)
        m_p, m_s = _mix_merge(proj_p, proj_s, kv_p, ab_s, attn_sinks[l], gain, bias, sgu_w_s[l], sgu_b_s[l].T,
                              w_pa, w_pb, l)
        ys.append(_matmul(m_p, m_s, w_o, l, D_MODEL // TN, lambda j: j, bf16, "out"))

        kv_win = kv_p.reshape(BATCH, SEQ, 2 * D_KV)[:, SEQ - WINDOW:]
        kv_win = kv_win.reshape(BATCH, WINDOW, 2, N_KV_HEADS, HEAD_DIM)
        kv_new = kv_s.reshape(DEC_BATCH, DEC_SEQ, 2, N_KV_HEADS, HEAD_DIM)
        win_k.append(kv_win[:, :, 0])
        win_v.append(kv_win[:, :, 1])
        new_k.append(kv_new[:, :, 0])
        new_v.append(kv_new[:, :, 1])
        sgu_v.append(vn_s.reshape(DEC_BATCH, DEC_SEQ, D_SGU))

    xp, xs = _final_stream(xp, xs, ys, mod, ln_gain, ln_bias)
    return (xp.reshape(BATCH, SEQ, D_MODEL), xs.reshape(DEC_BATCH, DEC_SEQ, D_MODEL),
            jnp.stack(win_k), jnp.stack(win_v), jnp.stack(new_k), jnp.stack(new_v), jnp.stack(sgu_v))
```
